```python
import math
import jax
import jax.numpy as jnp
from jax import lax
import numpy as np

D_MODEL = 4096
BATCH = 2
SEQ = 8192
DEPTH = 2

CTX_LEN = 256
GRID_W = 64
HEAD_DIM = 128
N_MIX_HEADS = D_MODEL // HEAD_DIM
A_HEADS = 3 * N_MIX_HEADS // 8
A_KV_HEADS = A_HEADS // 3
A_GROUP = A_HEADS // A_KV_HEADS
B_HEADS = (N_MIX_HEADS - A_HEADS) // 2
C_HEADS = N_MIX_HEADS - A_HEADS - B_HEADS
C_QK_DIM = HEAD_DIM // 2
HG_DK = HEAD_DIM
HG_DV = HEAD_DIM
A_W = A_HEADS * HEAD_DIM
B_W = B_HEADS * HG_DV
C_W = C_HEADS * HEAD_DIM
MIX_W = A_W + B_W + C_W
D_FF = ((8 * D_MODEL + 3 * 256 - 1) // (3 * 256)) * 256
ATTN_BLOCK = 128
HG_CHUNK = 64
ROPE_THETA = 10000.0
EPS = 1e-6

Q_SPLITS = (A_W, C_HEADS * 2 * C_QK_DIM, B_HEADS * HG_DK, B_W)
KV_SPLITS = (A_KV_HEADS * HEAD_DIM, A_KV_HEADS * HEAD_DIM, C_HEADS * 2 * C_QK_DIM, C_W,
             B_W, B_HEADS * HG_DK, B_HEADS * HG_DK)
Q_COLS = sum(Q_SPLITS)
IN_COLS = Q_COLS + sum(KV_SPLITS)

kernel_name = 'hybrid_headgroup_diffusion_block'


def _rms(x, gain=None):
    xf = x.astype(jnp.float32)
    y = xf * lax.rsqrt(jnp.mean(xf * xf, axis=-1, keepdims=True) + EPS)
    if gain is not None:
        y = y * gain.astype(jnp.float32)
    return y.astype(x.dtype)


def _split(a, sizes):
    return jnp.split(a, np.cumsum(sizes)[:-1].tolist(), axis=-1)


def _axial_rope(row, col, dim):
    half = dim // 2
    inv = ROPE_THETA ** (-jnp.arange(0, half, 2, dtype=jnp.float32) / half)
    ang = jnp.concatenate([row[:, None] * inv, col[:, None] * inv], axis=-1)
    return jnp.cos(ang), jnp.sin(ang)


def _rope(x, cos, sin):
    shp = (cos.shape[0],) + (1,) * (x.ndim - 3) + (cos.shape[1],)
    cos = cos.reshape(shp)
    sin = sin.reshape(shp)
    xf = x.astype(jnp.float32).reshape(x.shape[:-1] + (x.shape[-1] // 2, 2))
    x0, x1 = xf[..., 0], xf[..., 1]
    out = jnp.stack([x0 * cos - x1 * sin, x0 * sin + x1 * cos], axis=-1).reshape(x.shape)
    return out.astype(x.dtype)


def _sweep(q, fn):
    bsz, T = q.shape[:2]
    nb = T // ATTN_BLOCK
    qb = jnp.moveaxis(q.reshape((bsz, nb, ATTN_BLOCK) + q.shape[2:]), 1, 0)
    o = lax.map(fn, qb)
    return jnp.moveaxis(o, 0, 1).reshape((bsz, T) + o.shape[3:])


def _gqa(q, k, v):
    scale = q.shape[-1] ** -0.5

    def block(qb):
        s = jnp.einsum('bqkgd,bskd->bkgqs', qb, k).astype(jnp.float32) * scale
        p = jax.nn.softmax(s, axis=-1).astype(v.dtype)
        return jnp.einsum('bkgqs,bskd->bqkgd', p, v)

    return _sweep(q, block)


def _diff(q, k, v, lam):
    scale = q.shape[-1] ** -0.5

    def block(qb):
        s = jnp.einsum('bqhmd,bshmd->bhmqs', qb, k).astype(jnp.float32) * scale
        p = jax.nn.softmax(s, axis=-1)
        w = (p[:, :, 0] - lam * p[:, :, 1]).astype(v.dtype)
        return jnp.einsum('bhqs,bshd->bqhd', w, v)

    return _sweep(q, block)


def _logf(z, lb):
    zf = z.astype(jnp.float32).reshape(z.shape[0], z.shape[1], B_HEADS, HG_DK)
    return jnp.logaddexp(jnp.log(lb), jnp.log1p(-lb) + jax.nn.log_sigmoid(zf))


def _gla_scan(q, v, logf, s0):
    bsz, T, H, _ = q.shape
    n = T // HG_CHUNK

    def chunks(a):
        return a.reshape(bsz, n, HG_CHUNK, H, a.shape[-1]).transpose(1, 0, 3, 2, 4)

    mask = jnp.tril(jnp.ones((HG_CHUNK, HG_CHUNK), dtype=bool))[:, :, None]

    def step(S, inp):
        qc, vc, gc = inp
        kc = -jnp.expm1(gc)
        b = jnp.cumsum(gc, axis=2)
        rel = b[:, :, :, None, :] - b[:, :, None, :, :]
        dec = jnp.exp(jnp.where(mask, rel, -jnp.inf))
        a = jnp.einsum('bhtd,bhsd,bhtsd->bhts', qc, kc, dec)
        o = jnp.einsum('bhts,bhsv->bhtv', a, vc) + jnp.einsum('bhtd,bhdv->bhtv', qc * jnp.exp(b), S)
        b_end = b[:, :, -1:, :]
        S = jnp.exp(b_end[:, :, 0, :])[..., None] * S + jnp.einsum('bhsd,bhsv->bhdv', kc * jnp.exp(b_end - b), vc)
        return S, o

    S, o = lax.scan(step, s0, (chunks(q), chunks(v), chunks(logf)))
    return o.transpose(1, 0, 3, 2, 4).reshape(bsz, T, H, v.shape[-1]), S


def _final_state(v, logf):
    b = jnp.cumsum(logf, axis=1)
    w = -jnp.expm1(logf) * jnp.exp(b[:, -1:] - b)
    return jnp.einsum('blhd,blhv->bhdv', w, v)


def _swiglu(h, w_gate, w_up, w_down):
    return (jax.nn.silu(h @ w_gate) * (h @ w_up)) @ w_down


def _layer(x, y, c, c_ctx, rope_a, rope_c, lb, lam_init, last,
           w_ada, b_ada, w_in, w_out, aq_g, ak_g, ho_g, dq_g, dk_g, d_lam, do_g,
           w_gate, w_up, w_down):
    bsz, T, D = x.shape
    L = y.shape[1]
    dt = x.dtype
    f32 = jnp.float32
    flip = lambda a: jnp.flip(a, axis=1)

    mx = [m[:, None, :] for m in jnp.split(jax.nn.silu(c) @ w_ada + b_ada, 6, axis=-1)]
    n_y = 2 if last else 6
    my = jnp.split(jax.nn.silu(c_ctx) @ w_ada[:, :n_y * D] + b_ada[:n_y * D], n_y, axis=-1)

    hx = _rms(x) * (1 + mx[1]) + mx[0]
    hy = _rms(y) * (1 + my[1]) + my[0]

    xa_q, xc_q, xb_q, xb_g, xa_k, xa_v, xc_k, xc_v, xb_i, xb_ff, xb_fb = _split(hx @ w_in, Q_SPLITS + KV_SPLITS)
    if last:
        yparts = _split(hy @ w_in[:, Q_COLS:], KV_SPLITS)
    else:
        yparts = _split(hy @ w_in, Q_SPLITS + KV_SPLITS)
        ya_q, yc_q, yb_q, yb_g = yparts[:4]
        yparts = yparts[4:]
    ya_k, ya_v, yc_k, yc_v, yb_i, yb_ff, yb_fb = yparts

    ka_x = _rope(_rms(xa_k.reshape(bsz, T, A_KV_HEADS, HEAD_DIM), ak_g), *rope_a)
    ka_y = _rms(ya_k.reshape(bsz, L, A_KV_HEADS, HEAD_DIM), ak_g)
    va_x = xa_v.reshape(bsz, T, A_KV_HEADS, HEAD_DIM)
    va_y = ya_v.reshape(bsz, L, A_KV_HEADS, HEAD_DIM)
    qa_x = _rope(_rms(xa_q.reshape(bsz, T, A_KV_HEADS, A_GROUP, HEAD_DIM), aq_g), *rope_a)
    oa_x = _gqa(qa_x, jnp.concatenate([ka_y, ka_x], axis=1),
                jnp.concatenate([va_y, va_x], axis=1)).reshape(bsz, T, A_W)

    dl = d_lam.astype(f32)
    lam = jnp.exp(jnp.sum(dl[0] * dl[1])) - jnp.exp(jnp.sum(dl[2] * dl[3])) + lam_init
    qc_x = _rope(_rms(xc_q.reshape(bsz, T, C_HEADS, 2, C_QK_DIM), dq_g), *rope_c)
    kc_x = _rope(_rms(xc_k.reshape(bsz, T, C_HEADS, 2, C_QK_DIM), dk_g), *rope_c)
    kc_y = _rms(yc_k.reshape(bsz, L, C_HEADS, 2, C_QK_DIM), dk_g)
    vc_x = xc_v.reshape(bsz, T, C_HEADS, HEAD_DIM)
    vc_y = yc_v.reshape(bsz, L, C_HEADS, HEAD_DIM)
    oc_x = _diff(qc_x, jnp.concatenate([kc_y, kc_x], axis=1), jnp.concatenate([vc_y, vc_x], axis=1), lam)
    oc_x = (_rms(oc_x, do_g) * (1.0 - lam_init)).reshape(bsz, T, C_W)

    gf_x, gb_x = _logf(xb_ff, lb[0]), _logf(xb_fb, lb[1])
    gf_y, gb_y = _logf(yb_ff, lb[0]), _logf(yb_fb, lb[1])
    ib_x = yb_i if False else xb_i.astype(f32).reshape(bsz, T, B_HEADS, HG_DV)
    ib_y = yb_i.astype(f32).reshape(bsz, L, B_HEADS, HG_DV)
    qb_x = jax.nn.silu(xb_q.astype(f32)).reshape(bsz, T, B_HEADS, HG_DK)
    if last:
        s_f = _final_state(ib_y, gf_y)
        s_b = _final_state(flip(ib_y), flip(gb_y))
    else:
        s0 = jnp.zeros((bsz, B_HEADS, HG_DK, HG_DV), f32)
        qb_y = jax.nn.silu(yb_q.astype(f32)).reshape(bsz, L, B_HEADS, HG_DK)
        ob_yf, s_f = _gla_scan(qb_y, ib_y, gf_y, s0)
        ob_yb, s_b = _gla_scan(flip(qb_y), flip(ib_y), flip(gb_y), s0)
        ob_y = ob_yf + flip(ob_yb)
    ob_xf, _ = _gla_scan(qb_x, ib_x, gf_x, s_f)
    ob_xb, _ = _gla_scan(flip(qb_x), flip(ib_x), flip(gb_x), s_b)
    ob_x = _rms(ob_xf + flip(ob_xb), ho_g) * jax.nn.silu(xb_g.astype(f32)).reshape(bsz, T, B_HEADS, HG_DV)
    ob_x = ob_x.reshape(bsz, T, B_W).astype(dt)

    mix = jnp.concatenate([oa_x.astype(dt), ob_x, oc_x.astype(dt)], axis=-1) @ w_out
    x = x + mx[2] * mix
    x = x + mx[5] * _swiglu(_rms(x) * (1 + mx[4]) + mx[3], w_gate, w_up, w_down)
    if last:
        return x, None

    qa_y = _rms(ya_q.reshape(bsz, L, A_KV_HEADS, A_GROUP, HEAD_DIM), aq_g)
    oa_y = _gqa(qa_y, ka_y, va_y).reshape(bsz, L, A_W)
    qc_y = _rms(yc_q.reshape(bsz, L, C_HEADS, 2, C_QK_DIM), dq_g)
    oc_y = (_rms(_diff(qc_y, kc_y, vc_y, lam), do_g) * (1.0 - lam_init)).reshape(bsz, L, C_W)
    ob_y = _rms(ob_y, ho_g) * jax.nn.silu(yb_g.astype(f32)).reshape(bsz, L, B_HEADS, HG_DV)
    ob_y = ob_y.reshape(bsz, L, B_W).astype(dt)
    mix_y = jnp.concatenate([oa_y.astype(dt), ob_y, oc_y.astype(dt)], axis=-1) @ w_out
    y = y + my[2] * mix_y
    y = y + my[5] * _swiglu(_rms(y) * (1 + my[4]) + my[3], w_gate, w_up, w_down)
    return x, y


def setup_inputs(seed: int = 0) -> dict:
    key = jax.random.key(seed)
    ks = jax.random.split(key, 20)

    def nrm(k, shape, s):
        return jax.random.normal(k, shape, jnp.float32) * s

    return {
        'x': nrm(ks[0], (BATCH, SEQ, D_MODEL), 1.0),
        'c': nrm(ks[1], (BATCH, D_MODEL), 1.0),
        'ctx': nrm(ks[2], (BATCH, CTX_LEN, D_MODEL), 1.0),
        'c_ctx': nrm(ks[3], (D_MODEL,), 1.0),
        'w_ada': nrm(ks[4], (DEPTH, D_MODEL, 6 * D_MODEL), 0.5 * D_MODEL ** -0.5),
        'b_ada': nrm(ks[5], (DEPTH, 6 * D_MODEL), 0.02),
        'w_in': nrm(ks[6], (DEPTH, D_MODEL, IN_COLS), D_MODEL ** -0.5),
        'w_out': nrm(ks[7], (DEPTH, MIX_W, D_MODEL), MIX_W ** -0.5),
        'attn_q_gain': 1.0 + nrm(ks[8], (DEPTH, HEAD_DIM), 0.02),
        'attn_k_gain': 1.0 + nrm(ks[9], (DEPTH, HEAD_DIM), 0.02),
        'hgrn_lb_logits': nrm(ks[10], (DEPTH, 2, B_HEADS * HG_DK), 0.5),
        'hgrn_o_gain': 1.0 + nrm(ks[11], (DEPTH, HG_DV), 0.02),
        'diff_q_gain': 1.0 + nrm(ks[12], (DEPTH, C_QK_DIM), 0.02),
        'diff_k_gain': 1.0 + nrm(ks[13], (DEPTH, C_QK_DIM), 0.02),
        'diff_lambda': nrm(ks[14], (DEPTH, 4, C_QK_DIM), 0.1),
        'diff_o_gain': 1.0 + nrm(ks[15], (DEPTH, HEAD_DIM), 0.02),
        'w_gate': nrm(ks[16], (DEPTH, D_MODEL, D_FF), D_MODEL ** -0.5),
        'w_up': nrm(ks[17], (DEPTH, D_MODEL, D_FF), D_MODEL ** -0.5),
        'w_down': nrm(ks[18], (DEPTH, D_FF, D_MODEL), D_FF ** -0.5),
    }


def reference(x, c, ctx, c_ctx, w_ada, b_ada, w_in, w_out, attn_q_gain, attn_k_gain,
              hgrn_lb_logits, hgrn_o_gain, diff_q_gain, diff_k_gain, diff_lambda, diff_o_gain,
              w_gate, w_up, w_down):
    T = x.shape[1]
    rows = T // GRID_W
    row = jnp.repeat(jnp.arange(rows, dtype=jnp.float32), GRID_W)
    col = jnp.tile(jnp.arange(GRID_W, dtype=jnp.float32), rows)
    rope_a = _axial_rope(row, col, HEAD_DIM)
    rope_c = _axial_rope(row, col, C_QK_DIM)

    sm = jax.nn.softmax(hgrn_lb_logits.astype(jnp.float32), axis=0)
    lb = jnp.concatenate([jnp.zeros_like(sm[:1]), jnp.cumsum(sm[1:], axis=0)], axis=0)
    lb = lb.reshape(DEPTH, 2, B_HEADS, HG_DK)

    y = ctx
    for l in range(DEPTH):
        lam_init = 0.8 - 0.6 * math.exp(-0.3 * l)
        x, y = _layer(x, y, c, c_ctx, rope_a, rope_c, lb[l], lam_init, l == DEPTH - 1,
                      w_ada[l], b_ada[l], w_in[l], w_out[l], attn_q_gain[l], attn_k_gain[l],
                      hgrn_o_gain[l], diff_q_gain[l], diff_k_gain[l], diff_lambda[l], diff_o_gain[l],
                      w_gate[l], w_up[l], w_down[l])
    return x
```

```python
import functools
import math

import numpy as np
import jax
import jax.numpy as jnp
from jax import lax
from jax.experimental import pallas as pl
from jax.experimental.pallas import tpu as pltpu

F32 = jnp.float32
BF16 = jnp.bfloat16

HEAD_DIM = 128
GRID_W = 64
HG_CHUNK = 64
ROPE_THETA = 10000.0
EPS = 1e-6
FF_PAD = 1024
VMEM_LIMIT = 56 * 1024 * 1024


def _pick(dim, pref, align):
    t = min(pref, dim)
    t -= t % align
    while t >= align:
        if dim % t == 0:
            return t
        t -= align
    return dim


def _params(*sem):
    return pltpu.CompilerParams(dimension_semantics=sem, vmem_limit_bytes=VMEM_LIMIT)


def _mm_kernel(a_ref, w_ref, o_ref):
    o_ref[...] = jnp.dot(a_ref[...], w_ref[...], preferred_element_type=F32).astype(o_ref.dtype)


def _mm(a, w, out_dtype, tm=1024, tn=512):
    M, K = a.shape
    N = w.shape[1]
    tm = _pick(M, tm, 8)
    tn = _pick(N, tn, 128)
    return pl.pallas_call(
        _mm_kernel,
        grid=(M // tm, N // tn),
        in_specs=[pl.BlockSpec((tm, K), lambda i, j: (i, 0)),
                  pl.BlockSpec((K, tn), lambda i, j: (0, j))],
        out_specs=pl.BlockSpec((tm, tn), lambda i, j: (i, j)),
        out_shape=jax.ShapeDtypeStruct((M, N), out_dtype),
        compiler_params=_params("parallel", "arbitrary"),
        name="proj_in",
    )(a, w)


def _mm_res_kernel(a_ref, w_ref, r_ref, g_ref, o_ref, *scratch, nk):
    d = jnp.dot(a_ref[...], w_ref[...], preferred_element_type=F32)
    if nk == 1:
        o_ref[...] = r_ref[...] + g_ref[...] * d
        return
    acc_ref, = scratch
    k = pl.program_id(2)

    @pl.when(k == 0)
    def _():
        acc_ref[...] = d

    @pl.when(k > 0)
    def _():
        acc_ref[...] += d

    @pl.when(k == nk - 1)
    def _():
        o_ref[...] = r_ref[...] + g_ref[...] * acc_ref[...]


def _mm_res(a, w, res, gate, tm=1024, tn=1024, tk_max=4096):
    M, K = a.shape
    N = w.shape[1]
    G = gate.shape[0]
    tm = _pick(M // G, tm, 8)
    tn = _pick(N, tn, 128)
    tk = K if K <= tk_max else _pick(K, tk_max, 256)
    nk = K // tk
    tiles_per_group = (M // G) // tm
    scratch = [] if nk == 1 else [pltpu.VMEM((tm, tn), F32)]
    return pl.pallas_call(
        functools.partial(_mm_res_kernel, nk=nk),
        grid=(M // tm, N // tn, nk),
        in_specs=[pl.BlockSpec((tm, tk), lambda i, j, k: (i, k)),
                  pl.BlockSpec((tk, tn), lambda i, j, k: (k, j)),
                  pl.BlockSpec((tm, tn), lambda i, j, k: (i, j)),
                  pl.BlockSpec((None, 1, tn), lambda i, j, k: (i // tiles_per_group, 0, j))],
        out_specs=pl.BlockSpec((tm, tn), lambda i, j, k: (i, j)),
        out_shape=jax.ShapeDtypeStruct((M, N), F32),
        scratch_shapes=scratch,
        compiler_params=_params("parallel", "arbitrary", "arbitrary"),
        name="proj_res",
    )(a, w, res, gate)


def _mm_swiglu_kernel(a_ref, wg_ref, wu_ref, o_ref):
    a = a_ref[...]
    g = jnp.dot(a, wg_ref[...], preferred_element_type=F32)
    u = jnp.dot(a, wu_ref[...], preferred_element_type=F32)
    o_ref[...] = (g * jax.nn.sigmoid(g) * u).astype(o_ref.dtype)


def _mm_swiglu(a, wg, wu, tm=1024, tn=512):
    M, K = a.shape
    N = wg.shape[1]
    tm = _pick(M, tm, 8)
    tn = _pick(N, tn, 128)
    return pl.pallas_call(
        _mm_swiglu_kernel,
        grid=(M // tm, N // tn),
        in_specs=[pl.BlockSpec((tm, K), lambda i, j: (i, 0)),
                  pl.BlockSpec((K, tn), lambda i, j: (0, j)),
                  pl.BlockSpec((K, tn), lambda i, j: (0, j))],
        out_specs=pl.BlockSpec((tm, tn), lambda i, j: (i, j)),
        out_shape=jax.ShapeDtypeStruct((M, N), BF16),
        compiler_params=_params("parallel", "arbitrary"),
        name="ffn_up",
    )(a, wg, wu)


def _ada_kernel(c_ref, w_ref, b_ref, o_ref):
    o_ref[...] = jnp.dot(c_ref[...], w_ref[...], preferred_element_type=F32,
                         precision=lax.Precision.HIGHEST) + b_ref[...]


def _ada(cs, w, b, tn=512):
    R, D = cs.shape
    N = w.shape[1]
    tn = _pick(N, tn, 128)
    return pl.pallas_call(
        _ada_kernel,
        grid=(N // tn,),
        in_specs=[pl.BlockSpec((R, D), lambda j: (0, 0)),
                  pl.BlockSpec((D, tn), lambda j: (0, j)),
                  pl.BlockSpec((1, tn), lambda j: (0, j))],
        out_specs=pl.BlockSpec((R, tn), lambda j: (0, j)),
        out_shape=jax.ShapeDtypeStruct((R, N), F32),
        compiler_params=_params("arbitrary"),
        name="adaln",
    )(cs, w, b)


def _norm_mod_kernel(x_ref, sh_ref, sc_ref, o_ref):
    x = x_ref[...]
    y = x * lax.rsqrt(jnp.mean(x * x, axis=-1, keepdims=True) + EPS)
    o_ref[...] = (y * (1.0 + sc_ref[...]) + sh_ref[...]).astype(o_ref.dtype)


def _norm_mod(x, shift, scale, tm=256):
    M, D = x.shape
    G = shift.shape[0]
    tm = _pick(M // G, tm, 8)
    tiles_per_group = (M // G) // tm
    mod_spec = pl.BlockSpec((None, 1, D), lambda i: (i // tiles_per_group, 0, 0))
    return pl.pallas_call(
        _norm_mod_kernel,
        grid=(M // tm,),
        in_specs=[pl.BlockSpec((tm, D), lambda i: (i, 0)), mod_spec, mod_spec],
        out_specs=pl.BlockSpec((tm, D), lambda i: (i, 0)),
        out_shape=jax.ShapeDtypeStruct((M, D), BF16),
        compiler_params=_params("parallel"),
        name="norm_mod",
    )(x, shift, scale)


def _attn_kernel(*refs, G, tq, diff, has_x, post_scale):
    if has_x:
        q_ref, kc_ref, vc_ref, kx_ref, vx_ref, lam_ref, gain_ref, o_ref, qs_ref, m_ref, l_ref, acc_ref = refs
    else:
        q_ref, kc_ref, vc_ref, lam_ref, gain_ref, o_ref, qs_ref, m_ref, l_ref, acc_ref = refs
    j = pl.program_id(3)
    nj = pl.num_programs(3)

    def scores(k):
        return lax.dot_general(qs_ref[...], k, (((1,), (1,)), ((), ())), preferred_element_type=F32)

    @pl.when(j == 0)
    def _():
        q = q_ref[...]
        if diff:
            lane = lax.broadcasted_iota(jnp.int32, q.shape, 1)
            zero = jnp.zeros_like(q)
            qs_ref[0:tq, :] = jnp.where(lane < HEAD_DIM // 2, q, zero)
            qs_ref[tq:2 * tq, :] = jnp.where(lane >= HEAD_DIM // 2, q, zero)
        else:
            for g in range(G):
                qs_ref[g * tq:(g + 1) * tq, :] = q[:, g * HEAD_DIM:(g + 1) * HEAD_DIM]
        s = scores(kc_ref[...])
        m = jnp.max(s, axis=-1, keepdims=True)
        p = jnp.exp(s - m)
        m_ref[...] = m
        l_ref[...] = jnp.sum(p, axis=-1, keepdims=True)
        acc_ref[...] = jnp.dot(p.astype(BF16), vc_ref[...], preferred_element_type=F32)

    if has_x:
        s = scores(kx_ref[...])
        m_prev = m_ref[...]
        m_new = jnp.maximum(m_prev, jnp.max(s, axis=-1, keepdims=True))
        alpha = jnp.exp(m_prev - m_new)
        p = jnp.exp(s - m_new)
        m_ref[...] = m_new
        l_ref[...] = alpha * l_ref[...] + jnp.sum(p, axis=-1, keepdims=True)
        acc_ref[...] = alpha * acc_ref[...] + jnp.dot(p.astype(BF16), vx_ref[...], preferred_element_type=F32)

    @pl.when(j == nj - 1)
    def _():
        o = acc_ref[...] * (1.0 / l_ref[...])
        if diff:
            d = o[0:tq, :] - lam_ref[0] * o[tq:2 * tq, :]
            d = d * lax.rsqrt(jnp.mean(d * d, axis=-1, keepdims=True) + EPS)
            o_ref[...] = (d * gain_ref[...] * post_scale).astype(o_ref.dtype)
        else:
            for g in range(G):
                o_ref[:, g * HEAD_DIM:(g + 1) * HEAD_DIM] = o[g * tq:(g + 1) * tq, :].astype(o_ref.dtype)


def _attention(q, kc, vc, kx, vx, *, G, diff=False, lam=None, gain=None, post_scale=1.0, tq=512, tk=1024):
    B, T, _ = q.shape
    L = kc.shape[1]
    H = kc.shape[2] // HEAD_DIM
    has_x = kx is not None
    gq = 1 if diff else G
    rows = (2 if diff else G)
    tq = _pick(T, tq, 8)
    if has_x:
        tk = _pick(kx.shape[1], tk, 128)
        nj = kx.shape[1] // tk
    else:
        nj = 1
    if lam is None:
        lam = jnp.zeros((1,), F32)
    if gain is None:
        gain = jnp.ones((1, HEAD_DIM), F32)
    in_specs = [pl.BlockSpec((None, tq, gq * HEAD_DIM), lambda b, h, i, j: (b, i, h)),
                pl.BlockSpec((None, L, HEAD_DIM), lambda b, h, i, j: (b, 0, h)),
                pl.BlockSpec((None, L, HEAD_DIM), lambda b, h, i, j: (b, 0, h))]
    args = [q, kc, vc]
    if has_x:
        in_specs += [pl.BlockSpec((None, tk, HEAD_DIM), lambda b, h, i, j: (b, j, h)),
                     pl.BlockSpec((None, tk, HEAD_DIM), lambda b, h, i, j: (b, j, h))]
        args += [kx, vx]
    in_specs += [pl.BlockSpec(memory_space=pltpu.SMEM),
                 pl.BlockSpec((1, HEAD_DIM), lambda b, h, i, j: (0, 0))]
    args += [lam.reshape(1).astype(F32), gain.reshape(1, HEAD_DIM).astype(F32)]
    return pl.pallas_call(
        functools.partial(_attn_kernel, G=G, tq=tq, diff=diff, has_x=has_x, post_scale=post_scale),
        grid=(B, H, T // tq, nj),
        in_specs=in_specs,
        out_specs=pl.BlockSpec((None, tq, gq * HEAD_DIM), lambda b, h, i, j: (b, i, h)),
        out_shape=jax.ShapeDtypeStruct((B, T, H * gq * HEAD_DIM), BF16),
        scratch_shapes=[pltpu.VMEM((rows * tq, HEAD_DIM), BF16),
                        pltpu.VMEM((rows * tq, 1), F32),
                        pltpu.VMEM((rows * tq, 1), F32),
                        pltpu.VMEM((rows * tq, HEAD_DIM), F32)],
        compiler_params=_params("parallel", "parallel", "parallel", "arbitrary"),
        name="attn_diff" if diff else "attn_gqa",
    )(*args)


_HG_LEVELS = (32, 16, 8, 4, 2, 1)


def _hgrn_consts(reverse):
    C = HG_CHUNK
    mst = np.zeros((2 + len(_HG_LEVELS), C, C), np.float32)
    msk = np.zeros((len(_HG_LEVELS) + 1, C, C), np.float32)
    for t in range(C):
        if not reverse:
            mst[0, t, :t + 1] = 1.0
            mst[1, t, t + 1:] = 1.0
        else:
            mst[0, t, t:] = 1.0
            mst[1, t, :t] = 1.0
        msk[len(_HG_LEVELS), t, t] = 1.0
    for l, m in enumerate(_HG_LEVELS):
        for t in range(C):
            start = (t // (2 * m)) * 2 * m
            mid = start + m
            second = t >= mid
            if not reverse:
                if second:
                    mst[2 + l, t, mid:t + 1] = 1.0
                    msk[l, t, start:mid] = 1.0
                else:
                    mst[2 + l, t, t + 1:mid] = 1.0
            else:
                if not second:
                    mst[2 + l, t, t:mid] = 1.0
                    msk[l, t, mid:start + 2 * m] = 1.0
                else:
                    mst[2 + l, t, mid:t] = 1.0
    return mst.reshape(-1, C), msk


def _hgrn_kernel(q_ref, v_ref, g_ref, s0_ref, mst_ref, msk_ref, o_ref, sf_ref, st_ref, *, n_chunks, reverse):
    C = HG_CHUNK
    tb = pl.program_id(2)
    ntb = pl.num_programs(2)

    @pl.when(tb == 0)
    def _():
        st_ref[...] = s0_ref[...]

    mst = mst_ref[...]
    row = lax.broadcasted_iota(jnp.int32, (C, HEAD_DIM), 0)
    nt = (((1,), (1,)), ((), ()))
    tn = (((0,), (0,)), ((), ()))

    def chunk(ci, carry):
        c = (n_chunks - 1 - ci) if reverse else ci
        r0 = pl.multiple_of(c * C, C)
        q = q_ref[pl.ds(r0, C), :]
        v = v_ref[pl.ds(r0, C), :].astype(BF16)
        g = g_ref[pl.ds(r0, C), :]
        k = 1.0 - jnp.exp(g)
        g_hi = g.astype(BF16)
        g_lo = (g - g_hi.astype(F32)).astype(BF16)
        e = jnp.exp(jnp.dot(mst, g_hi, preferred_element_type=F32)
                    + jnp.dot(mst, g_lo, preferred_element_type=F32))
        e_in = e[0:C]
        e_out = e[C:2 * C]
        e_tot = e_in[0:1] if reverse else e_in[C - 1:C]
        a = msk_ref[len(_HG_LEVELS)] * lax.dot_general(q.astype(BF16), k.astype(BF16), nt,
                                                       preferred_element_type=F32)
        for l, m in enumerate(_HG_LEVELS):
            is_q = ((row // m) % 2) == (0 if reverse else 1)
            x = (jnp.where(is_q, q, k) * e[(2 + l) * C:(3 + l) * C]).astype(BF16)
            a = a + msk_ref[l] * lax.dot_general(x, x, nt, preferred_element_type=F32)
        st = st_ref[...]
        o = jnp.dot(a.astype(BF16), v, preferred_element_type=F32)
        o = o + lax.dot_general((q * e_in).astype(BF16), st.astype(BF16), nt, preferred_element_type=F32)
        o_ref[pl.ds(r0, C), :] = o
        st_ref[...] = st * e_tot + lax.dot_general(v, (k * e_out).astype(BF16), tn, preferred_element_type=F32)
        return carry

    lax.fori_loop(0, n_chunks, chunk, 0)

    @pl.when(tb == ntb - 1)
    def _():
        sf_ref[...] = st_ref[...]


def _hgrn_scan(q, v, g, s0, *, reverse, tb=512):
    B, T, W = q.shape
    H = W // HEAD_DIM
    tb = _pick(T, tb, HG_CHUNK)
    ntb = T // tb
    mst, msk = _hgrn_consts(reverse)
    if reverse:
        seq_spec = pl.BlockSpec((None, tb, HEAD_DIM), lambda b, h, t: (b, ntb - 1 - t, h))
    else:
        seq_spec = pl.BlockSpec((None, tb, HEAD_DIM), lambda b, h, t: (b, t, h))
    st_spec = pl.BlockSpec((None, None, HEAD_DIM, HEAD_DIM), lambda b, h, t: (b, h, 0, 0))
    return pl.pallas_call(
        functools.partial(_hgrn_kernel, n_chunks=tb // HG_CHUNK, reverse=reverse),
        grid=(B, H, ntb),
        in_specs=[seq_spec, seq_spec, seq_spec, st_spec,
                  pl.BlockSpec(mst.shape, lambda b, h, t: (0, 0)),
                  pl.BlockSpec(msk.shape, lambda b, h, t: (0, 0, 0))],
        out_specs=[seq_spec, st_spec],
        out_shape=[jax.ShapeDtypeStruct((B, T, W), F32),
                   jax.ShapeDtypeStruct((B, H, HEAD_DIM, HEAD_DIM), F32)],
        scratch_shapes=[pltpu.VMEM((HEAD_DIM, HEAD_DIM), F32)],
        compiler_params=_params("parallel", "parallel", "arbitrary"),
        name="hgrn_bwd" if reverse else "hgrn_fwd",
    )(q, v, g, s0, jnp.asarray(mst, BF16), jnp.asarray(msk, F32))


def _rms(x, gain=None):
    y = x * lax.rsqrt(jnp.mean(x * x, axis=-1, keepdims=True) + EPS)
    if gain is not None:
        y = y * gain
    return y


def _axial_rope(row, col, dim):
    half = dim // 2
    inv = ROPE_THETA ** (-jnp.arange(0, half, 2, dtype=F32) / half)
    ang = jnp.concatenate([row[:, None] * inv, col[:, None] * inv], axis=-1)
    return jnp.cos(ang), jnp.sin(ang)


def _rope(x, cos, sin):
    shp = (cos.shape[0],) + (1,) * (x.ndim - 3) + (cos.shape[1],)
    cos = cos.reshape(shp)
    sin = sin.reshape(shp)
    xf = x.reshape(x.shape[:-1] + (x.shape[-1] // 2, 2))
    x0, x1 = xf[..., 0], xf[..., 1]
    return jnp.stack([x0 * cos - x1 * sin, x0 * sin + x1 * cos], axis=-1).reshape(x.shape)


def _logf(z, lb):
    zf = z.reshape(z.shape[0], z.shape[1], lb.shape[0], lb.shape[1])
    return jnp.logaddexp(jnp.log(lb), jnp.log1p(-lb) + jax.nn.log_sigmoid(zf)).reshape(z.shape)


def _dims(D):
    n = D // HEAD_DIM
    a_heads = 3 * n // 8
    a_kv = a_heads // 3
    b_heads = (n - a_heads) // 2
    c_heads = n - a_heads - b_heads
    return a_heads, a_kv, b_heads, c_heads


def _split(a, sizes):
    return jnp.split(a, np.cumsum(sizes)[:-1].tolist(), axis=-1)


def _layer(x, y, mod, rope_a, rope_c, lb, lam_init, last,
           w_in, w_out, aq_g, ak_g, ho_g, dq_g, dk_g, d_lam, do_g, w_gate, w_up, w_down):
    B, T, D = x.shape
    L = y.shape[1]
    a_heads, a_kv, b_heads, c_heads = _dims(D)
    a_group = a_heads // a_kv
    a_w, b_w, c_w = a_heads * HEAD_DIM, b_heads * HEAD_DIM, c_heads * HEAD_DIM
    half = HEAD_DIM // 2
    splits = (a_w, c_w, b_w, b_w, a_kv * HEAD_DIM, a_kv * HEAD_DIM, c_w, c_w, b_w, b_w, b_w)

    mx = [m[:, None, :] for m in jnp.split(mod[0:B], 6, axis=-1)]
    my = [m[:, None, :] for m in jnp.split(mod[B:B + 1], 6, axis=-1)]

    x2 = x.reshape(B * T, D)
    y2 = y.reshape(B * L, D)
    px = _mm(_norm_mod(x2, mx[0], mx[1]), w_in, F32).reshape(B, T, -1)
    py = _mm(_norm_mod(y2, my[0], my[1]), w_in, F32).reshape(B, L, -1)
    xa_q, xc_q, xb_q, xb_g, xa_k, xa_v, xc_k, xc_v, xb_i, xb_ff, xb_fb = _split(px, splits)
    ya_q, yc_q, yb_q, yb_g, ya_k, ya_v, yc_k, yc_v, yb_i, yb_ff, yb_fb = _split(py, splits)

    a_scale = HEAD_DIM ** -0.5
    ka_x = _rope(_rms(xa_k.reshape(B, T, a_kv, HEAD_DIM), ak_g), *rope_a).reshape(B, T, -1).astype(BF16)
    ka_y = _rms(ya_k.reshape(B, L, a_kv, HEAD_DIM), ak_g).reshape(B, L, -1).astype(BF16)
    qa_x = _rope(_rms(xa_q.reshape(B, T, a_heads, HEAD_DIM), aq_g), *rope_a) * a_scale
    oa_x = _attention(qa_x.reshape(B, T, -1).astype(BF16), ka_y, ya_v.astype(BF16),
                      ka_x, xa_v.astype(BF16), G=a_group)

    c_scale = half ** -0.5
    lam = jnp.exp(jnp.sum(d_lam[0] * d_lam[1])) - jnp.exp(jnp.sum(d_lam[2] * d_lam[3])) + lam_init
    qc_x = _rope(_rms(xc_q.reshape(B, T, c_heads, 2, half), dq_g), *rope_c) * c_scale
    kc_x = _rope(_rms(xc_k.reshape(B, T, c_heads, 2, half), dk_g), *rope_c).reshape(B, T, -1).astype(BF16)
    kc_y = _rms(yc_k.reshape(B, L, c_heads, 2, half), dk_g).reshape(B, L, -1).astype(BF16)
    oc_x = _attention(qc_x.reshape(B, T, -1).astype(BF16), kc_y, yc_v.astype(BF16),
                      kc_x, xc_v.astype(BF16), G=2, diff=True, lam=lam, gain=do_g,
                      post_scale=1.0 - lam_init)

    s0 = jnp.zeros((B, b_heads, HEAD_DIM, HEAD_DIM), F32)
    qb_y = jax.nn.silu(yb_q)
    ob_yf, s_f = _hgrn_scan(qb_y, yb_i, _logf(yb_ff, lb[0]), s0, reverse=False)
    ob_yb, s_b = _hgrn_scan(qb_y, yb_i, _logf(yb_fb, lb[1]), s0, reverse=True)
    qb_x = jax.nn.silu(xb_q)
    ob_xf, _ = _hgrn_scan(qb_x, xb_i, _logf(xb_ff, lb[0]), s_f, reverse=False)
    ob_xb, _ = _hgrn_scan(qb_x, xb_i, _logf(xb_fb, lb[1]), s_b, reverse=True)
    ob_x = _rms((ob_xf + ob_xb).reshape(B, T, b_heads, HEAD_DIM), ho_g).reshape(B, T, -1) * jax.nn.silu(xb_g)

    mix = jnp.concatenate([oa_x, ob_x.astype(BF16), oc_x], axis=-1).reshape(B * T, -1)
    x2 = _mm_res(mix, w_out, x2, mx[2])
    hff = _mm_swiglu(_norm_mod(x2, mx[3], mx[4]), w_gate, w_up)
    x2 = _mm_res(hff, w_down, x2, mx[5])
    x = x2.reshape(B, T, D)
    if last:
        return x, None

    qa_y = _rms(ya_q.reshape(B, L, a_heads, HEAD_DIM), aq_g) * a_scale
    oa_y = _attention(qa_y.reshape(B, L, -1).astype(BF16), ka_y, ya_v.astype(BF16), None, None, G=a_group)
    qc_y = _rms(yc_q.reshape(B, L, c_heads, 2, half), dq_g) * c_scale
    oc_y = _attention(qc_y.reshape(B, L, -1).astype(BF16), kc_y, yc_v.astype(BF16), None, None,
                      G=2, diff=True, lam=lam, gain=do_g, post_scale=1.0 - lam_init)
    ob_y = _rms((ob_yf + ob_yb).reshape(B, L, b_heads, HEAD_DIM), ho_g).reshape(B, L, -1) * jax.nn.silu(yb_g)
    mix_y = jnp.concatenate([oa_y, ob_y.astype(BF16), oc_y], axis=-1).reshape(B * L, -1)
    y2 = _mm_res(mix_y, w_out, y2, my[2])
    hff_y = _mm_swiglu(_norm_mod(y2, my[3], my[4]), w_gate, w_up)
    y2 = _mm_res(hff_y, w_down, y2, my[5])
    return x, y2.reshape(B, L, D)


def kernel(x, c, ctx, c_ctx, w_ada, b_ada, w_in, w_out, attn_q_gain, attn_k_gain,
           hgrn_lb_logits, hgrn_o_gain, diff_q_gain, diff_k_gain, diff_lambda, diff_o_gain,
           w_gate, w_up, w_down):
    B, T, D = x.shape
    depth = w_ada.shape[0]
    _, _, b_heads, _ = _dims(D)
    rows = T // GRID_W
    row = jnp.repeat(jnp.arange(rows, dtype=F32), GRID_W)
    col = jnp.tile(jnp.arange(GRID_W, dtype=F32), rows)
    rope_a = _axial_rope(row, col, HEAD_DIM)
    rope_c = _axial_rope(row, col, HEAD_DIM // 2)

    sm = jax.nn.softmax(hgrn_lb_logits.astype(F32), axis=0)
    lb = jnp.concatenate([jnp.zeros_like(sm[:1]), jnp.cumsum(sm[1:], axis=0)], axis=0)
    lb = lb.reshape(depth, 2, b_heads, HEAD_DIM)

    cs = jnp.concatenate([c, c_ctx[None, :]], axis=0)
    cs = jax.nn.silu(jnp.pad(cs, ((0, 8 - (B + 1)), (0, 0))))

    d_ff = w_gate.shape[-1]
    ff_pad = (-d_ff) % FF_PAD

    y = ctx
    for l in range(depth):
        lam_init = 0.8 - 0.6 * math.exp(-0.3 * l)
        mod = _ada(cs, w_ada[l], b_ada[l][None, :])
        wg = jnp.pad(w_gate[l].astype(BF16), ((0, 0), (0, ff_pad)))
        wu = jnp.pad(w_up[l].astype(BF16), ((0, 0), (0, ff_pad)))
        wd = jnp.pad(w_down[l].astype(BF16), ((0, ff_pad), (0, 0)))
        x, y = _layer(x, y, mod, rope_a, rope_c, lb[l], lam_init, l == depth - 1,
                      w_in[l].astype(BF16), w_out[l].astype(BF16),
                      attn_q_gain[l], attn_k_gain[l], hgrn_o_gain[l], diff_q_gain[l], diff_k_gain[l],
                      diff_lambda[l], diff_o_gain[l], wg, wu, wd)
    return x
```

```python
import functools
import math

import numpy as np
import jax
import jax.numpy as jnp
from jax import lax
from jax.experimental import pallas as pl
from jax.experimental.pallas import tpu as pltpu

F32 = jnp.float32
BF16 = jnp.bfloat16

HEAD_DIM = 128
GRID_W = 64
HG_CHUNK = 64
HG_UNROLL = 4
ROPE_THETA = 10000.0
EPS = 1e-6
FF_PAD = 1024
VMEM_LIMIT = 56 * 1024 * 1024


def _pick(dim, pref, align):
    t = min(pref, dim)
    t -= t % align
    while t >= align:
        if dim % t == 0:
            return t
        t -= align
    return dim


def _params(*sem):
    return pltpu.CompilerParams(dimension_semantics=sem, vmem_limit_bytes=VMEM_LIMIT)


def _mm_kernel(a_ref, w_ref, o_ref):
    o_ref[...] = jnp.dot(a_ref[...], w_ref[...], preferred_element_type=F32).astype(o_ref.dtype)


def _mm(a, w, out_dtype, tm=1024, tn=512):
    M, K = a.shape
    N = w.shape[1]
    tm = _pick(M, tm, 8)
    tn = _pick(N, tn, 128)
    return pl.pallas_call(
        _mm_kernel,
        grid=(M // tm, N // tn),
        in_specs=[pl.BlockSpec((tm, K), lambda i, j: (i, 0)),
                  pl.BlockSpec((K, tn), lambda i, j: (0, j))],
        out_specs=pl.BlockSpec((tm, tn), lambda i, j: (i, j)),
        out_shape=jax.ShapeDtypeStruct((M, N), out_dtype),
        compiler_params=_params("parallel", "arbitrary"),
        name="proj_in",
    )(a, w)


def _mm_res_kernel(*refs, nk, n_in):
    a_refs, w_refs = refs[:n_in], refs[n_in:2 * n_in]
    r_ref, g_ref, o_ref = refs[2 * n_in:2 * n_in + 3]
    d = jnp.dot(a_refs[0][...], w_refs[0][...], preferred_element_type=F32)
    for a_ref, w_ref in zip(a_refs[1:], w_refs[1:]):
        d = d + jnp.dot(a_ref[...], w_ref[...], preferred_element_type=F32)
    if nk == 1:
        o_ref[...] = r_ref[...] + g_ref[...] * d
        return
    acc_ref, = refs[2 * n_in + 3:]
    k = pl.program_id(2)

    @pl.when(k == 0)
    def _():
        acc_ref[...] = d

    @pl.when(k > 0)
    def _():
        acc_ref[...] += d

    @pl.when(k == nk - 1)
    def _():
        o_ref[...] = r_ref[...] + g_ref[...] * acc_ref[...]


def _mm_res(a_list, w_list, res, gate, tm=1024, tn=1024, tk_max=4096):
    M = a_list[0].shape[0]
    N = w_list[0].shape[1]
    G = gate.shape[0]
    n_in = len(a_list)
    tm = _pick(M // G, tm, 8)
    tn = _pick(N, tn, 128)
    K = a_list[0].shape[1]
    tk = K if (n_in > 1 or K <= tk_max) else _pick(K, tk_max, 256)
    nk = K // tk
    tiles_per_group = (M // G) // tm
    scratch = [] if nk == 1 else [pltpu.VMEM((tm, tn), F32)]
    if n_in == 1:
        a_specs = [pl.BlockSpec((tm, tk), lambda i, j, k: (i, k))]
        w_specs = [pl.BlockSpec((tk, tn), lambda i, j, k: (k, j))]
    else:
        a_specs = [pl.BlockSpec((tm, a.shape[1]), lambda i, j, k: (i, 0)) for a in a_list]
        w_specs = [pl.BlockSpec((w.shape[0], tn), lambda i, j, k: (0, j)) for w in w_list]
    return pl.pallas_call(
        functools.partial(_mm_res_kernel, nk=nk, n_in=n_in),
        grid=(M // tm, N // tn, nk),
        in_specs=a_specs + w_specs + [
            pl.BlockSpec((tm, tn), lambda i, j, k: (i, j)),
            pl.BlockSpec((None, 1, tn), lambda i, j, k: (i // tiles_per_group, 0, j))],
        out_specs=pl.BlockSpec((tm, tn), lambda i, j, k: (i, j)),
        out_shape=jax.ShapeDtypeStruct((M, N), F32),
        scratch_shapes=scratch,
        compiler_params=_params("parallel", "arbitrary", "arbitrary"),
        name="proj_res",
    )(*a_list, *w_list, res, gate)


def _mm_swiglu_kernel(a_ref, wg_ref, wu_ref, o_ref):
    a = a_ref[...]
    g = jnp.dot(a, wg_ref[...], preferred_element_type=F32)
    u = jnp.dot(a, wu_ref[...], preferred_element_type=F32)
    o_ref[...] = (g * jax.nn.sigmoid(g) * u).astype(o_ref.dtype)


def _mm_swiglu(a, wg, wu, tm=1024, tn=512):
    M, K = a.shape
    N = wg.shape[1]
    tm = _pick(M, tm, 8)
    tn = _pick(N, tn, 128)
    return pl.pallas_call(
        _mm_swiglu_kernel,
        grid=(M // tm, N // tn),
        in_specs=[pl.BlockSpec((tm, K), lambda i, j: (i, 0)),
                  pl.BlockSpec((K, tn), lambda i, j: (0, j)),
                  pl.BlockSpec((K, tn), lambda i, j: (0, j))],
        out_specs=pl.BlockSpec((tm, tn), lambda i, j: (i, j)),
        out_shape=jax.ShapeDtypeStruct((M, N), BF16),
        compiler_params=_params("parallel", "arbitrary"),
        name="ffn_up",
    )(a, wg, wu)


def _ada_kernel(c_ref, w_ref, b_ref, o_ref):
    o_ref[...] = jnp.dot(c_ref[...], w_ref[...], preferred_element_type=F32,
                         precision=lax.Precision.HIGHEST) + b_ref[...]


def _ada(cs, w, b, tn=512):
    R, D = cs.shape
    N = w.shape[1]
    tn = _pick(N, tn, 128)
    return pl.pallas_call(
        _ada_kernel,
        grid=(N // tn,),
        in_specs=[pl.BlockSpec((R, D), lambda j: (0, 0)),
                  pl.BlockSpec((D, tn), lambda j: (0, j)),
                  pl.BlockSpec((1, tn), lambda j: (0, j))],
        out_specs=pl.BlockSpec((R, tn), lambda j: (0, j)),
        out_shape=jax.ShapeDtypeStruct((R, N), F32),
        compiler_params=_params("arbitrary"),
        name="adaln",
    )(cs, w, b)


def _norm_mod_kernel(x_ref, sh_ref, sc_ref, o_ref):
    x = x_ref[...]
    y = x * lax.rsqrt(jnp.mean(x * x, axis=-1, keepdims=True) + EPS)
    o_ref[...] = (y * (1.0 + sc_ref[...]) + sh_ref[...]).astype(o_ref.dtype)


def _norm_mod(x, shift, scale, tm=256):
    M, D = x.shape
    G = shift.shape[0]
    tm = _pick(M // G, tm, 8)
    tiles_per_group = (M // G) // tm
    mod_spec = pl.BlockSpec((None, 1, D), lambda i: (i // tiles_per_group, 0, 0))
    return pl.pallas_call(
        _norm_mod_kernel,
        grid=(M // tm,),
        in_specs=[pl.BlockSpec((tm, D), lambda i: (i, 0)), mod_spec, mod_spec],
        out_specs=pl.BlockSpec((tm, D), lambda i: (i, 0)),
        out_shape=jax.ShapeDtypeStruct((M, D), BF16),
        compiler_params=_params("parallel"),
        name="norm_mod",
    )(x, shift, scale)


ATTN_PANEL = 512
LOG2E = math.log2(math.e)


def _attn_kernel(*refs, G, tq, diff, has_x, post_scale):
    if has_x:
        q_ref, kc_ref, vc_ref, kx_ref, vx_ref, lam_ref, gain_ref, o_ref, qt_ref, m_ref, l_ref, acc_ref = refs
    else:
        q_ref, kc_ref, vc_ref, lam_ref, gain_ref, o_ref, qt_ref, m_ref, l_ref, acc_ref = refs
    j = pl.program_id(3)
    nj = pl.num_programs(3)
    n_rows = 2 if diff else G
    width = n_rows * tq
    pw = min(ATTN_PANEL, width)
    tn = (((0,), (0,)), ((), ()))

    def scores(k, c0):
        return jnp.dot(k, qt_ref[:, c0:c0 + pw], preferred_element_type=F32)

    def update(s, v, c0, first):
        cols = slice(c0, c0 + pw)
        m_cur = jnp.max(s, axis=0, keepdims=True)
        if first:
            m_new = m_cur
            p = jnp.exp2(s - m_new)
            l_ref[:, cols] = jnp.sum(p, axis=0, keepdims=True)
            acc_ref[:, cols] = lax.dot_general(v, p.astype(BF16), tn, preferred_element_type=F32)
        else:
            m_prev = m_ref[:, cols]
            m_new = jnp.maximum(m_prev, m_cur)
            alpha = jnp.exp2(m_prev - m_new)
            p = jnp.exp2(s - m_new)
            l_ref[:, cols] = alpha * l_ref[:, cols] + jnp.sum(p, axis=0, keepdims=True)
            acc_ref[:, cols] = alpha * acc_ref[:, cols] + lax.dot_general(
                v, p.astype(BF16), tn, preferred_element_type=F32)
        m_ref[:, cols] = m_new

    def sweep(k, v, first):
        starts = list(range(0, width, pw))
        s = scores(k, starts[0])
        for n, c0 in enumerate(starts):
            s_next = scores(k, starts[n + 1]) if n + 1 < len(starts) else None
            update(s, v, c0, first)
            s = s_next

    @pl.when(j == 0)
    def _():
        q = q_ref[...].astype(F32)
        if diff:
            lane = lax.broadcasted_iota(jnp.int32, q.shape, 1)
            qt_ref[:, 0:tq] = jnp.where(lane < HEAD_DIM // 2, q, 0.0).T.astype(BF16)
            qt_ref[:, tq:2 * tq] = jnp.where(lane >= HEAD_DIM // 2, q, 0.0).T.astype(BF16)
        else:
            for g in range(G):
                qt_ref[:, g * tq:(g + 1) * tq] = q[:, g * HEAD_DIM:(g + 1) * HEAD_DIM].T.astype(BF16)
        sweep(kc_ref[...], vc_ref[...].astype(BF16), True)

    if has_x:
        sweep(kx_ref[...], vx_ref[...].astype(BF16), False)

    @pl.when(j == nj - 1)
    def _():
        o = acc_ref[...] * (1.0 / l_ref[...])
        if diff:
            d = o[:, 0:tq] - lam_ref[0] * o[:, tq:2 * tq]
            d = d * lax.rsqrt(jnp.mean(d * d, axis=0, keepdims=True) + EPS)
            o_ref[...] = (d * gain_ref[...] * post_scale).T.astype(o_ref.dtype)
        else:
            for g in range(G):
                o_ref[:, g * HEAD_DIM:(g + 1) * HEAD_DIM] = o[:, g * tq:(g + 1) * tq].T.astype(o_ref.dtype)


def _attention(q, kc, vc, kx, vx, *, G, v_slot, diff=False, lam=None, gain=None, post_scale=1.0, tq=1024, tk=2048):
    B, T, _ = q.shape
    L = kc.shape[1]
    H = kc.shape[2] // HEAD_DIM
    has_x = kx is not None
    gq = 1 if diff else G
    n_rows = 2 if diff else G
    tq = _pick(T, tq, 128)
    if has_x:
        tk = _pick(kx.shape[1], tk, 128)
        nj = kx.shape[1] // tk
    else:
        nj = 1
    if lam is None:
        lam = jnp.zeros((1,), F32)
    if gain is None:
        gain = jnp.ones((HEAD_DIM,), F32)
    in_specs = [pl.BlockSpec((None, tq, gq * HEAD_DIM), lambda b, h, i, j: (b, i, h)),
                pl.BlockSpec((None, L, HEAD_DIM), lambda b, h, i, j: (b, 0, h)),
                pl.BlockSpec((None, L, HEAD_DIM), lambda b, h, i, j: (b, 0, v_slot + h))]
    args = [q, kc, vc]
    if has_x:
        in_specs += [pl.BlockSpec((None, tk, HEAD_DIM), lambda b, h, i, j: (b, j, h)),
                     pl.BlockSpec((None, tk, HEAD_DIM), lambda b, h, i, j: (b, j, v_slot + h))]
        args += [kx, vx]
    in_specs += [pl.BlockSpec(memory_space=pltpu.SMEM),
                 pl.BlockSpec((HEAD_DIM, 1), lambda b, h, i, j: (0, 0))]
    args += [lam.reshape(1).astype(F32), gain.reshape(HEAD_DIM, 1).astype(F32)]
    return pl.pallas_call(
        functools.partial(_attn_kernel, G=G, tq=tq, diff=diff, has_x=has_x, post_scale=post_scale),
        grid=(B, H, T // tq, nj),
        in_specs=in_specs,
        out_specs=pl.BlockSpec((None, tq, gq * HEAD_DIM), lambda b, h, i, j: (b, i, h)),
        out_shape=jax.ShapeDtypeStruct((B, T, H * gq * HEAD_DIM), BF16),
        scratch_shapes=[pltpu.VMEM((HEAD_DIM, n_rows * tq), BF16),
                        pltpu.VMEM((1, n_rows * tq), F32),
                        pltpu.VMEM((1, n_rows * tq), F32),
                        pltpu.VMEM((HEAD_DIM, n_rows * tq), F32)],
        compiler_params=_params("parallel", "parallel", "parallel", "arbitrary"),
        name="attn_diff" if diff else "attn_gqa",
    )(*args)


_HG_LEVELS = (32, 16, 8, 4, 2, 1)


def _hgrn_consts(reverse):
    C = HG_CHUNK
    mst = np.zeros((2 + len(_HG_LEVELS), C, C), np.float32)
    msk = np.zeros((len(_HG_LEVELS) + 1, C, C), np.float32)
    for t in range(C):
        if not reverse:
            mst[0, t, :t + 1] = 1.0
            mst[1, t, t + 1:] = 1.0
        else:
            mst[0, t, t:] = 1.0
            mst[1, t, :t] = 1.0
        msk[len(_HG_LEVELS), t, t] = 1.0
    for l, m in enumerate(_HG_LEVELS):
        for t in range(C):
            start = (t // (2 * m)) * 2 * m
            mid = start + m
            second = t >= mid
            if not reverse:
                if second:
                    mst[2 + l, t, mid:t + 1] = 1.0
                    msk[l, t, start:mid] = 1.0
                else:
                    mst[2 + l, t, t + 1:mid] = 1.0
            else:
                if not second:
                    mst[2 + l, t, t:mid] = 1.0
                    msk[l, t, mid:start + 2 * m] = 1.0
                else:
                    mst[2 + l, t, mid:t] = 1.0
    return mst.reshape(-1, C), msk


def _hgrn_kernel(*refs, n_chunks, unroll, reverse, nh, final):
    q_ref, v_ref, z_ref, llb_ref, l1p_ref, s0_ref, mst_ref, msk_ref = refs[:8]
    if final:
        of_ref, gate_ref, hog_ref, o_ref, sf_ref, st_ref = refs[8:]
    else:
        o_ref, sf_ref, st_ref = refs[8:]
    C = HG_CHUNK
    tb = pl.program_id(2)
    ntb = pl.num_programs(2)

    @pl.when(tb == 0)
    def _():
        st_ref[...] = s0_ref[...]

    mst = mst_ref[...]
    row = lax.broadcasted_iota(jnp.int32, (C, HEAD_DIM), 0)
    nt = (((1,), (1,)), ((), ()))
    tn = (((0,), (0,)), ((), ()))

    def chunks(ci, carry):
        lanes = []
        for u in range(unroll):
            c = ci * unroll + u
            c = (n_chunks - 1 - c) if reverse else c
            rows = pl.ds(pl.multiple_of(c * C, C), C)
            lanes += [(rows, hh, slice(hh * HEAD_DIM, (hh + 1) * HEAD_DIM)) for hh in range(nh)]

        qs, vs, ks, gparts = [], [], [], []
        for rows, hh, cols in lanes:
            qr = q_ref[rows, cols]
            qs.append(qr * jax.nn.sigmoid(qr))
            vs.append(v_ref[rows, cols].astype(BF16))
            z = z_ref[rows, cols]
            t = l1p_ref[:, cols] + (jnp.minimum(z, 0.0) - jnp.log1p(jnp.exp(-jnp.abs(z))))
            llb = llb_ref[:, cols]
            g = jnp.maximum(llb, t) + jnp.log1p(jnp.exp(-jnp.abs(llb - t)))
            ks.append(1.0 - jnp.exp(g))
            g_hi = g.astype(BF16)
            gparts += [g_hi, (g - g_hi.astype(F32)).astype(BF16)]

        sums = jnp.dot(mst, jnp.concatenate(gparts, axis=1), preferred_element_type=F32)

        es, amats = [], []
        for n, (q, k) in enumerate(zip(qs, ks)):
            c0 = 2 * n * HEAD_DIM
            e = jnp.exp(sums[:, c0:c0 + HEAD_DIM] + sums[:, c0 + HEAD_DIM:c0 + 2 * HEAD_DIM])
            a = msk_ref[len(_HG_LEVELS)] * lax.dot_general(q.astype(BF16), k.astype(BF16), nt,
                                                           preferred_element_type=F32)
            for l, m in enumerate(_HG_LEVELS):
                is_q = ((row // m) % 2) == (0 if reverse else 1)
                x = (jnp.where(is_q, q, k) * e[(2 + l) * C:(3 + l) * C]).astype(BF16)
                a = a + msk_ref[l] * lax.dot_general(x, x, nt, preferred_element_type=F32)
            es.append(e)
            amats.append(a.astype(BF16))

        for (rows, hh, cols), q, v, k, e, a in zip(lanes, qs, vs, ks, es, amats):
            e_in = e[0:C]
            e_out = e[C:2 * C]
            e_tot = e_in[0:1] if reverse else e_in[C - 1:C]
            st = st_ref[hh]
            o = jnp.dot(a, v, preferred_element_type=F32)
            o = o + lax.dot_general((q * e_in).astype(BF16), st.astype(BF16), nt, preferred_element_type=F32)
            st_ref[hh] = st * e_tot + lax.dot_general(v, (k * e_out).astype(BF16), tn,
                                                     preferred_element_type=F32)
            if final:
                o = o + of_ref[rows, cols]
                o = o * lax.rsqrt(jnp.mean(o * o, axis=-1, keepdims=True) + EPS) * hog_ref[...]
                gt = gate_ref[rows, cols]
                o = o * (gt * jax.nn.sigmoid(gt))
            o_ref[rows, cols] = o.astype(o_ref.dtype)
        return carry

    lax.fori_loop(0, n_chunks // unroll, chunks, 0)

    @pl.when(tb == ntb - 1)
    def _():
        sf_ref[...] = st_ref[...]


def _hgrn_scan(p, slots, llb, l1p, s0, *, reverse, o_fwd=None, o_gain=None, tb=512):
    B, T, _ = p.shape
    W = llb.shape[1]
    H = W // HEAD_DIM
    final = o_fwd is not None
    nh = 2 if (H % 2 == 0 and all(s % 2 == 0 for s in slots)) else 1
    wb = nh * HEAD_DIM
    tb = _pick(T, tb, HG_CHUNK)
    n_chunks = tb // HG_CHUNK
    ntb = T // tb
    mst, msk = _hgrn_consts(reverse)

    def seq_spec(slot):
        s = slot // nh
        if reverse:
            return pl.BlockSpec((None, tb, wb), lambda b, h, t: (b, ntb - 1 - t, s + h))
        return pl.BlockSpec((None, tb, wb), lambda b, h, t: (b, t, s + h))

    vec_spec = pl.BlockSpec((1, wb), lambda b, h, t: (0, h))
    st_spec = pl.BlockSpec((None, nh, HEAD_DIM, HEAD_DIM), lambda b, h, t: (b, h, 0, 0))
    in_specs = [seq_spec(slots[0]), seq_spec(slots[1]), seq_spec(slots[2]), vec_spec, vec_spec, st_spec,
                pl.BlockSpec(mst.shape, lambda b, h, t: (0, 0)),
                pl.BlockSpec(msk.shape, lambda b, h, t: (0, 0, 0))]
    args = [p, p, p, llb, l1p, s0, jnp.asarray(mst, BF16), jnp.asarray(msk, F32)]
    if final:
        in_specs += [seq_spec(0), seq_spec(slots[3]), pl.BlockSpec((1, HEAD_DIM), lambda b, h, t: (0, 0))]
        args += [o_fwd, p, o_gain.reshape(1, HEAD_DIM).astype(F32)]
    return pl.pallas_call(
        functools.partial(_hgrn_kernel, n_chunks=n_chunks, unroll=math.gcd(n_chunks, HG_UNROLL), reverse=reverse,
                          nh=nh, final=final),
        grid=(B, H // nh, ntb),
        in_specs=in_specs,
        out_specs=[seq_spec(0), st_spec],
        out_shape=[jax.ShapeDtypeStruct((B, T, W), BF16 if final else F32),
                   jax.ShapeDtypeStruct((B, H, HEAD_DIM, HEAD_DIM), F32)],
        scratch_shapes=[pltpu.VMEM((nh, HEAD_DIM, HEAD_DIM), F32)],
        compiler_params=_params("parallel", "parallel", "arbitrary"),
        name="hgrn_bwd" if reverse else "hgrn_fwd",
    )(*args)


def _prep_kernel(x_ref, gain_ref, cos_ref, sin_ref, o_ref, *, ns, halves, rope, scale):
    gain = gain_ref[...]
    lane = lax.broadcasted_iota(jnp.int32, (x_ref.shape[0], HEAD_DIM), 1)
    low = lane < HEAD_DIM // 2
    even = (lane % 2) == 0
    for s in range(ns):
        cols = slice(s * HEAD_DIM, (s + 1) * HEAD_DIM)
        x = x_ref[:, cols]
        xx = x * x
        if halves:
            m0 = jnp.sum(jnp.where(low, xx, 0.0), axis=-1, keepdims=True) * (2.0 / HEAD_DIM)
            m1 = jnp.sum(jnp.where(low, 0.0, xx), axis=-1, keepdims=True) * (2.0 / HEAD_DIM)
            inv = jnp.where(low, lax.rsqrt(m0 + EPS), lax.rsqrt(m1 + EPS))
        else:
            inv = lax.rsqrt(jnp.mean(xx, axis=-1, keepdims=True) + EPS)
        y = x * inv * gain
        if rope:
            partner = jnp.where(even, pltpu.roll(y, HEAD_DIM - 1, 1), pltpu.roll(y, 1, 1))
            y = y * cos_ref[...] + partner * sin_ref[...]
        o_ref[:, cols] = (y * scale).astype(o_ref.dtype)


def _prep(p, slot, n_slots, gain, tables, *, halves, scale, seq_len, tm=1024):
    M = p.shape[0]
    ns = math.gcd(slot, n_slots)
    tm = _pick(seq_len, tm, 8)
    tiles_per_seq = seq_len // tm
    rope = tables is not None
    if rope:
        cos, sin = tables
        tab_spec = pl.BlockSpec((tm, HEAD_DIM), lambda i, j: (i % tiles_per_seq, 0))
    else:
        cos = sin = jnp.zeros((8, HEAD_DIM), F32)
        tab_spec = pl.BlockSpec((8, HEAD_DIM), lambda i, j: (0, 0))
    return pl.pallas_call(
        functools.partial(_prep_kernel, ns=ns, halves=halves, rope=rope, scale=scale),
        grid=(M // tm, n_slots // ns),
        in_specs=[pl.BlockSpec((tm, ns * HEAD_DIM), lambda i, j: (i, slot // ns + j)),
                  pl.BlockSpec((1, HEAD_DIM), lambda i, j: (0, 0)), tab_spec, tab_spec],
        out_specs=pl.BlockSpec((tm, ns * HEAD_DIM), lambda i, j: (i, j)),
        out_shape=jax.ShapeDtypeStruct((M, n_slots * HEAD_DIM), BF16),
        compiler_params=_params("parallel", "parallel"),
        name="prep_qk",
    )(p, gain.reshape(1, HEAD_DIM).astype(F32), cos, sin)


def _rope_tables(seq_len, dim):
    rows = seq_len // GRID_W
    row = jnp.repeat(jnp.arange(rows, dtype=F32), GRID_W)
    col = jnp.tile(jnp.arange(GRID_W, dtype=F32), rows)
    half = dim // 2
    inv = ROPE_THETA ** (-jnp.arange(0, half, 2, dtype=F32) / half)
    ang = jnp.concatenate([row[:, None] * inv, col[:, None] * inv], axis=-1)
    cos = jnp.repeat(jnp.cos(ang), 2, axis=-1)
    sin = jnp.repeat(jnp.sin(ang), 2, axis=-1) * jnp.tile(jnp.asarray([-1.0, 1.0], F32), half)
    reps = HEAD_DIM // dim
    return jnp.tile(cos, (1, reps)), jnp.tile(sin, (1, reps))


def _dims(D):
    n = D // HEAD_DIM
    a_heads = 3 * n // 8
    a_kv = a_heads // 3
    b_heads = (n - a_heads) // 2
    c_heads = n - a_heads - b_heads
    return a_heads, a_kv, b_heads, c_heads


def _layer(x2, y2, B, mod, rope_a, rope_c, lb, lam_init, last,
           w_in, w_out, aq_g, ak_g, ho_g, dq_g, dk_g, d_lam, do_g, w_gate, w_up, w_down):
    D = x2.shape[1]
    T = x2.shape[0] // B
    L = y2.shape[0] // B
    a_heads, a_kv, b_heads, c_heads = _dims(D)
    a_group = a_heads // a_kv
    half = HEAD_DIM // 2
    widths = (a_heads, c_heads, b_heads, b_heads, a_kv, a_kv, c_heads, c_heads, b_heads, b_heads, b_heads)
    (s_aq, s_cq, s_bq, s_bg, s_ak, s_av, s_ck, s_cv, s_bi, s_bff, s_bfb) = (
        np.concatenate([[0], np.cumsum(widths)[:-1]]).tolist())

    mx = [m[:, None, :] for m in jnp.split(mod[0:B], 6, axis=-1)]
    my = [m[:, None, :] for m in jnp.split(mod[B:B + 1], 6, axis=-1)]

    px = _mm(_norm_mod(x2, mx[0], mx[1]), w_in, F32)
    py = _mm(_norm_mod(y2, my[0], my[1]), w_in, F32)
    px3 = px.reshape(B, T, -1)
    py3 = py.reshape(B, L, -1)

    a_scale = HEAD_DIM ** -0.5 * LOG2E
    qa_x = _prep(px, s_aq, a_heads, aq_g, rope_a, halves=False, scale=a_scale, seq_len=T).reshape(B, T, -1)
    ka_x = _prep(px, s_ak, a_kv, ak_g, rope_a, halves=False, scale=1.0, seq_len=T).reshape(B, T, -1)
    ka_y = _prep(py, s_ak, a_kv, ak_g, None, halves=False, scale=1.0, seq_len=L).reshape(B, L, -1)
    oa_x = _attention(qa_x, ka_y, py3, ka_x, px3, G=a_group, v_slot=s_av)

    c_scale = half ** -0.5 * LOG2E
    dl = d_lam.astype(F32)
    lam = jnp.exp(jnp.sum(dl[0] * dl[1])) - jnp.exp(jnp.sum(dl[2] * dl[3])) + lam_init
    dq_g2 = jnp.tile(dq_g, 2)
    dk_g2 = jnp.tile(dk_g, 2)
    qc_x = _prep(px, s_cq, c_heads, dq_g2, rope_c, halves=True, scale=c_scale, seq_len=T).reshape(B, T, -1)
    kc_x = _prep(px, s_ck, c_heads, dk_g2, rope_c, halves=True, scale=1.0, seq_len=T).reshape(B, T, -1)
    kc_y = _prep(py, s_ck, c_heads, dk_g2, None, halves=True, scale=1.0, seq_len=L).reshape(B, L, -1)
    oc_x = _attention(qc_x, kc_y, py3, kc_x, px3, G=2, v_slot=s_cv, diff=True, lam=lam, gain=do_g,
                      post_scale=1.0 - lam_init)

    llb = jnp.log(lb).reshape(2, 1, -1)
    l1p = jnp.log1p(-lb).reshape(2, 1, -1)
    s0 = jnp.zeros((B, b_heads, HEAD_DIM, HEAD_DIM), F32)
    f_slots = (s_bq, s_bi, s_bff, s_bg)
    b_slots = (s_bq, s_bi, s_bfb, s_bg)
    of_y, s_f = _hgrn_scan(py3, f_slots, llb[0], l1p[0], s0, reverse=False)
    ob_y, s_b = _hgrn_scan(py3, b_slots, llb[1], l1p[1], s0, reverse=True, o_fwd=of_y, o_gain=ho_g)
    of_x, _ = _hgrn_scan(px3, f_slots, llb[0], l1p[0], s_f, reverse=False)
    ob_x, _ = _hgrn_scan(px3, b_slots, llb[1], l1p[1], s_b, reverse=True, o_fwd=of_x, o_gain=ho_g)

    x2 = _mm_res([oa_x.reshape(B * T, -1), ob_x.reshape(B * T, -1), oc_x.reshape(B * T, -1)], w_out, x2, mx[2])
    x2 = _mm_res([_mm_swiglu(_norm_mod(x2, mx[3], mx[4]), w_gate, w_up)], [w_down], x2, mx[5])
    if last:
        return x2, None

    qa_y = _prep(py, s_aq, a_heads, aq_g, None, halves=False, scale=a_scale, seq_len=L).reshape(B, L, -1)
    oa_y = _attention(qa_y, ka_y, py3, None, None, G=a_group, v_slot=s_av)
    qc_y = _prep(py, s_cq, c_heads, dq_g2, None, halves=True, scale=c_scale, seq_len=L).reshape(B, L, -1)
    oc_y = _attention(qc_y, kc_y, py3, None, None, G=2, v_slot=s_cv, diff=True, lam=lam, gain=do_g,
                      post_scale=1.0 - lam_init)
    y2 = _mm_res([oa_y.reshape(B * L, -1), ob_y.reshape(B * L, -1), oc_y.reshape(B * L, -1)], w_out, y2, my[2])
    y2 = _mm_res([_mm_swiglu(_norm_mod(y2, my[3], my[4]), w_gate, w_up)], [w_down], y2, my[5])
    return x2, y2


def kernel(x, c, ctx, c_ctx, w_ada, b_ada, w_in, w_out, attn_q_gain, attn_k_gain,
           hgrn_lb_logits, hgrn_o_gain, diff_q_gain, diff_k_gain, diff_lambda, diff_o_gain,
           w_gate, w_up, w_down):
    B, T, D = x.shape
    L = ctx.shape[1]
    depth = w_ada.shape[0]
    a_heads, _, b_heads, _ = _dims(D)
    rope_a = _rope_tables(T, HEAD_DIM)
    rope_c = _rope_tables(T, HEAD_DIM // 2)

    sm = jax.nn.softmax(hgrn_lb_logits.astype(F32), axis=0)
    lb = jnp.concatenate([jnp.zeros_like(sm[:1]), jnp.cumsum(sm[1:], axis=0)], axis=0)

    cs = jnp.concatenate([c, c_ctx[None, :]], axis=0)
    cs = jax.nn.silu(jnp.pad(cs, ((0, 8 - (B + 1)), (0, 0))))

    d_ff = w_gate.shape[-1]
    ff_pad = (-d_ff) % FF_PAD
    a_w = a_heads * HEAD_DIM
    b_w = b_heads * HEAD_DIM

    x2 = x.reshape(B * T, D)
    y2 = ctx.reshape(B * L, D)
    for l in range(depth):
        lam_init = 0.8 - 0.6 * math.exp(-0.3 * l)
        mod = _ada(cs, w_ada[l], b_ada[l][None, :])
        wg = jnp.concatenate([w_gate[l].astype(BF16), jnp.zeros((D, ff_pad), BF16)], axis=1)
        wu = jnp.concatenate([w_up[l].astype(BF16), jnp.zeros((D, ff_pad), BF16)], axis=1)
        wd = jnp.concatenate([w_down[l].astype(BF16), jnp.zeros((ff_pad, D), BF16)], axis=0)
        wo = w_out[l].astype(BF16)
        wo = [wo[:a_w], wo[a_w:a_w + b_w], wo[a_w + b_w:]]
        x2, y2 = _layer(x2, y2, B, mod, rope_a, rope_c, lb[l], lam_init, l == depth - 1,
                        w_in[l].astype(BF16), wo,
                        attn_q_gain[l], attn_k_gain[l], hgrn_o_gain[l], diff_q_gain[l], diff_k_gain[l],
                        diff_lambda[l], diff_o_gain[l], wg, wu, wd)
    return x2.reshape(B, T, D)
```

```python
import functools
import math

import numpy as np
import jax
import jax.numpy as jnp
from jax import lax
from jax.experimental import pallas as pl
from jax.experimental.pallas import tpu as pltpu

F32 = jnp.float32
BF16 = jnp.bfloat16

HEAD_DIM = 128
GRID_W = 64
HG_CHUNK = 64
HG_UNROLL = 4
ROPE_THETA = 10000.0
EPS = 1e-6
FF_PAD = 1024
FFN_DOWN_TN = 512
FFN_DOWN_TK = 5632
VMEM_LIMIT = 56 * 1024 * 1024


def _pick(dim, pref, align):
    t = min(pref, dim)
    t -= t % align
    while t >= align:
        if dim % t == 0:
            return t
        t -= align
    return dim


def _params(*sem):
    return pltpu.CompilerParams(dimension_semantics=sem, vmem_limit_bytes=VMEM_LIMIT)


def _mm_kernel(a_ref, w_ref, o_ref):
    o_ref[...] = jnp.dot(a_ref[...], w_ref[...], preferred_element_type=F32).astype(o_ref.dtype)


def _mm(a, w, out_dtype, tm=1024, tn=512):
    M, K = a.shape
    N = w.shape[1]
    tm = _pick(M, tm, 8)
    tn = _pick(N, tn, 128)
    return pl.pallas_call(
        _mm_kernel,
        grid=(M // tm, N // tn),
        in_specs=[pl.BlockSpec((tm, K), lambda i, j: (i, 0)),
                  pl.BlockSpec((K, tn), lambda i, j: (0, j))],
        out_specs=pl.BlockSpec((tm, tn), lambda i, j: (i, j)),
        out_shape=jax.ShapeDtypeStruct((M, N), out_dtype),
        compiler_params=_params("parallel", "arbitrary"),
        name="proj_in",
    )(a, w)


def _mm_res_kernel(*refs, nk, n_in):
    a_refs, w_refs = refs[:n_in], refs[n_in:2 * n_in]
    r_ref, g_ref, o_ref = refs[2 * n_in:2 * n_in + 3]
    d = jnp.dot(a_refs[0][...], w_refs[0][...], preferred_element_type=F32)
    for a_ref, w_ref in zip(a_refs[1:], w_refs[1:]):
        d = d + jnp.dot(a_ref[...], w_ref[...], preferred_element_type=F32)
    if nk == 1:
        o_ref[...] = r_ref[...] + g_ref[...] * d
        return
    acc_ref, = refs[2 * n_in + 3:]
    k = pl.program_id(2)

    @pl.when(k == 0)
    def _():
        acc_ref[...] = d

    @pl.when(k > 0)
    def _():
        acc_ref[...] += d

    @pl.when(k == nk - 1)
    def _():
        o_ref[...] = r_ref[...] + g_ref[...] * acc_ref[...]


def _mm_res(a_list, w_list, res, gate, tm=1024, tn=1024, tk_max=4096):
    M = a_list[0].shape[0]
    N = w_list[0].shape[1]
    G = gate.shape[0]
    n_in = len(a_list)
    tm = _pick(M // G, tm, 8)
    tn = _pick(N, tn, 128)
    K = a_list[0].shape[1]
    tk = K if (n_in > 1 or K <= tk_max) else _pick(K, tk_max, 256)
    nk = K // tk
    tiles_per_group = (M // G) // tm
    scratch = [] if nk == 1 else [pltpu.VMEM((tm, tn), F32)]
    if n_in == 1:
        a_specs = [pl.BlockSpec((tm, tk), lambda i, j, k: (i, k))]
        w_specs = [pl.BlockSpec((tk, tn), lambda i, j, k: (k, j))]
    else:
        a_specs = [pl.BlockSpec((tm, a.shape[1]), lambda i, j, k: (i, 0)) for a in a_list]
        w_specs = [pl.BlockSpec((w.shape[0], tn), lambda i, j, k: (0, j)) for w in w_list]
    return pl.pallas_call(
        functools.partial(_mm_res_kernel, nk=nk, n_in=n_in),
        grid=(M // tm, N // tn, nk),
        in_specs=a_specs + w_specs + [
            pl.BlockSpec((tm, tn), lambda i, j, k: (i, j)),
            pl.BlockSpec((None, 1, tn), lambda i, j, k: (i // tiles_per_group, 0, j))],
        out_specs=pl.BlockSpec((tm, tn), lambda i, j, k: (i, j)),
        out_shape=jax.ShapeDtypeStruct((M, N), F32),
        scratch_shapes=scratch,
        compiler_params=_params("parallel", "arbitrary", "arbitrary"),
        name="proj_res",
    )(*a_list, *w_list, res, gate)


def _mm_swiglu_kernel(a_ref, wg_ref, wu_ref, o_ref):
    a = a_ref[...]
    g = jnp.dot(a, wg_ref[...], preferred_element_type=F32)
    u = jnp.dot(a, wu_ref[...], preferred_element_type=F32)
    o_ref[...] = (g * jax.nn.sigmoid(g) * u).astype(o_ref.dtype)


def _mm_swiglu(a, wg, wu, tm=1024, tn=512):
    M, K = a.shape
    N = wg.shape[1]
    tm = _pick(M, tm, 8)
    tn = _pick(N, tn, 128)
    return pl.pallas_call(
        _mm_swiglu_kernel,
        grid=(M // tm, N // tn),
        in_specs=[pl.BlockSpec((tm, K), lambda i, j: (i, 0)),
                  pl.BlockSpec((K, tn), lambda i, j: (0, j)),
                  pl.BlockSpec((K, tn), lambda i, j: (0, j))],
        out_specs=pl.BlockSpec((tm, tn), lambda i, j: (i, j)),
        out_shape=jax.ShapeDtypeStruct((M, N), BF16),
        compiler_params=_params("parallel", "arbitrary"),
        name="ffn_up",
    )(a, wg, wu)


def _ada_kernel(c_ref, w_ref, b_ref, o_ref):
    o_ref[...] = jnp.dot(c_ref[...], w_ref[...], preferred_element_type=F32,
                         precision=lax.Precision.HIGHEST) + b_ref[...]


def _ada(cs, w, b, layer, tn=512):
    R, D = cs.shape
    N = w.shape[2]
    tn = _pick(N, tn, 128)
    return pl.pallas_call(
        _ada_kernel,
        grid=(N // tn,),
        in_specs=[pl.BlockSpec((R, D), lambda j: (0, 0)),
                  pl.BlockSpec((None, D, tn), lambda j: (layer, 0, j)),
                  pl.BlockSpec((None, 1, tn), lambda j: (layer, 0, j))],
        out_specs=pl.BlockSpec((R, tn), lambda j: (0, j)),
        out_shape=jax.ShapeDtypeStruct((R, N), F32),
        compiler_params=_params("arbitrary"),
        name="adaln",
    )(cs, w, b)


def _cast_kernel(w_ref, o_ref, *, rows_valid, cols_valid):
    tr, tc = o_ref.shape
    r = pl.program_id(0) * tr + lax.broadcasted_iota(jnp.int32, (tr, tc), 0)
    c = pl.program_id(1) * tc + lax.broadcasted_iota(jnp.int32, (tr, tc), 1)
    o_ref[...] = jnp.where((r < rows_valid) & (c < cols_valid), w_ref[...], 0.0).astype(o_ref.dtype)


def _cast_w(w, layer, row0, rows, rows_out, cols_out, tc=1024):
    cols = w.shape[2]
    tr = math.gcd(math.gcd(row0, rows_out), 512)
    tc = _pick(cols_out, tc, 128)
    assert (rows_out - tr) < rows and (cols_out - tc) < cols and row0 + rows <= w.shape[1]
    return pl.pallas_call(
        functools.partial(_cast_kernel, rows_valid=rows, cols_valid=cols),
        grid=(rows_out // tr, cols_out // tc),
        in_specs=[pl.BlockSpec((None, tr, tc), lambda i, j: (layer, row0 // tr + i, j))],
        out_specs=pl.BlockSpec((tr, tc), lambda i, j: (i, j)),
        out_shape=jax.ShapeDtypeStruct((rows_out, cols_out), BF16),
        compiler_params=_params("parallel", "parallel"),
        name="cast_w",
    )(w)


def _norm_mod_kernel(x_ref, sh_ref, sc_ref, o_ref):
    x = x_ref[...]
    y = x * lax.rsqrt(jnp.mean(x * x, axis=-1, keepdims=True) + EPS)
    o_ref[...] = (y * (1.0 + sc_ref[...]) + sh_ref[...]).astype(o_ref.dtype)


def _norm_mod(x, shift, scale, tm=256):
    M, D = x.shape
    G = shift.shape[0]
    tm = _pick(M // G, tm, 8)
    tiles_per_group = (M // G) // tm
    mod_spec = pl.BlockSpec((None, 1, D), lambda i: (i // tiles_per_group, 0, 0))
    return pl.pallas_call(
        _norm_mod_kernel,
        grid=(M // tm,),
        in_specs=[pl.BlockSpec((tm, D), lambda i: (i, 0)), mod_spec, mod_spec],
        out_specs=pl.BlockSpec((tm, D), lambda i: (i, 0)),
        out_shape=jax.ShapeDtypeStruct((M, D), BF16),
        compiler_params=_params("parallel"),
        name="norm_mod",
    )(x, shift, scale)


ATTN_PANEL = 512
LOG2E = math.log2(math.e)


def _attn_kernel(*refs, G, tq, diff, has_x, post_scale):
    if has_x:
        q_ref, kc_ref, vc_ref, kx_ref, vx_ref, lam_ref, gain_ref, o_ref, qt_ref, m_ref, l_ref, acc_ref = refs
    else:
        q_ref, kc_ref, vc_ref, lam_ref, gain_ref, o_ref, qt_ref, m_ref, l_ref, acc_ref = refs
    j = pl.program_id(3)
    nj = pl.num_programs(3)
    n_rows = 2 if diff else G
    width = n_rows * tq
    pw = min(ATTN_PANEL, width)
    tn = (((0,), (0,)), ((), ()))

    def scores(k, c0):
        return jnp.dot(k, qt_ref[:, c0:c0 + pw], preferred_element_type=F32)

    def update(s, v, c0, first):
        cols = slice(c0, c0 + pw)
        m_cur = jnp.max(s, axis=0, keepdims=True)
        if first:
            m_new = m_cur
            p = jnp.exp2(s - m_new)
            l_ref[:, cols] = jnp.sum(p, axis=0, keepdims=True)
            acc_ref[:, cols] = lax.dot_general(v, p.astype(BF16), tn, preferred_element_type=F32)
        else:
            m_prev = m_ref[:, cols]
            m_new = jnp.maximum(m_prev, m_cur)
            alpha = jnp.exp2(m_prev - m_new)
            p = jnp.exp2(s - m_new)
            l_ref[:, cols] = alpha * l_ref[:, cols] + jnp.sum(p, axis=0, keepdims=True)
            acc_ref[:, cols] = alpha * acc_ref[:, cols] + lax.dot_general(
                v, p.astype(BF16), tn, preferred_element_type=F32)
        m_ref[:, cols] = m_new

    def sweep(k, v, first):
        starts = list(range(0, width, pw))
        s = scores(k, starts[0])
        for n, c0 in enumerate(starts):
            s_next = scores(k, starts[n + 1]) if n + 1 < len(starts) else None
            update(s, v, c0, first)
            s = s_next

    @pl.when(j == 0)
    def _():
        q = q_ref[...].astype(F32)
        if diff:
            lane = lax.broadcasted_iota(jnp.int32, q.shape, 1)
            qt_ref[:, 0:tq] = jnp.where(lane < HEAD_DIM // 2, q, 0.0).T.astype(BF16)
            qt_ref[:, tq:2 * tq] = jnp.where(lane >= HEAD_DIM // 2, q, 0.0).T.astype(BF16)
        else:
            for g in range(G):
                qt_ref[:, g * tq:(g + 1) * tq] = q[:, g * HEAD_DIM:(g + 1) * HEAD_DIM].T.astype(BF16)
        sweep(kc_ref[...], vc_ref[...].astype(BF16), True)

    if has_x:
        sweep(kx_ref[...], vx_ref[...].astype(BF16), False)

    @pl.when(j == nj - 1)
    def _():
        o = acc_ref[...] * (1.0 / l_ref[...])
        if diff:
            d = o[:, 0:tq] - lam_ref[0] * o[:, tq:2 * tq]
            d = d * lax.rsqrt(jnp.mean(d * d, axis=0, keepdims=True) + EPS)
            o_ref[...] = (d * gain_ref[...] * post_scale).T.astype(o_ref.dtype)
        else:
            for g in range(G):
                o_ref[:, g * HEAD_DIM:(g + 1) * HEAD_DIM] = o[:, g * tq:(g + 1) * tq].T.astype(o_ref.dtype)


def _attention(q, kc, vc, kx, vx, *, G, v_slot, diff=False, lam=None, gain=None, post_scale=1.0, tq=2048, tk=1024):
    B, T, _ = q.shape
    L = kc.shape[1]
    H = kc.shape[2] // HEAD_DIM
    has_x = kx is not None
    gq = 1 if diff else G
    n_rows = 2 if diff else G
    tq = _pick(T, tq, 128)
    if has_x:
        tk = _pick(kx.shape[1], tk, 128)
        nj = kx.shape[1] // tk
    else:
        nj = 1
    if lam is None:
        lam = jnp.zeros((1,), F32)
    if gain is None:
        gain = jnp.ones((HEAD_DIM,), F32)
    in_specs = [pl.BlockSpec((None, tq, gq * HEAD_DIM), lambda b, h, i, j: (b, i, h)),
                pl.BlockSpec((None, L, HEAD_DIM), lambda b, h, i, j: (b, 0, h)),
                pl.BlockSpec((None, L, HEAD_DIM), lambda b, h, i, j: (b, 0, v_slot + h))]
    args = [q, kc, vc]
    if has_x:
        in_specs += [pl.BlockSpec((None, tk, HEAD_DIM), lambda b, h, i, j: (b, j, h)),
                     pl.BlockSpec((None, tk, HEAD_DIM), lambda b, h, i, j: (b, j, v_slot + h))]
        args += [kx, vx]
    in_specs += [pl.BlockSpec(memory_space=pltpu.SMEM),
                 pl.BlockSpec((HEAD_DIM, 1), lambda b, h, i, j: (0, 0))]
    args += [lam.reshape(1).astype(F32), gain.reshape(HEAD_DIM, 1).astype(F32)]
    return pl.pallas_call(
        functools.partial(_attn_kernel, G=G, tq=tq, diff=diff, has_x=has_x, post_scale=post_scale),
        grid=(B, H, T // tq, nj),
        in_specs=in_specs,
        out_specs=pl.BlockSpec((None, tq, gq * HEAD_DIM), lambda b, h, i, j: (b, i, h)),
        out_shape=jax.ShapeDtypeStruct((B, T, H * gq * HEAD_DIM), BF16),
        scratch_shapes=[pltpu.VMEM((HEAD_DIM, n_rows * tq), BF16),
                        pltpu.VMEM((1, n_rows * tq), F32),
                        pltpu.VMEM((1, n_rows * tq), F32),
                        pltpu.VMEM((HEAD_DIM, n_rows * tq), F32)],
        compiler_params=_params("parallel", "parallel", "parallel", "arbitrary"),
        name="attn_diff" if diff else "attn_gqa",
    )(*args)


_HG_LEVELS = (32, 16, 8, 4, 2, 1)


def _hgrn_consts(reverse):
    C = HG_CHUNK
    mst = np.zeros((2 + len(_HG_LEVELS), C, C), np.float32)
    msk = np.zeros((len(_HG_LEVELS) + 1, C, C), np.float32)
    for t in range(C):
        if not reverse:
            mst[0, t, :t + 1] = 1.0
            mst[1, t, t + 1:] = 1.0
        else:
            mst[0, t, t:] = 1.0
            mst[1, t, :t] = 1.0
        msk[len(_HG_LEVELS), t, t] = 1.0
    for l, m in enumerate(_HG_LEVELS):
        for t in range(C):
            start = (t // (2 * m)) * 2 * m
            mid = start + m
            second = t >= mid
            if not reverse:
                if second:
                    mst[2 + l, t, mid:t + 1] = 1.0
                    msk[l, t, start:mid] = 1.0
                else:
                    mst[2 + l, t, t + 1:mid] = 1.0
            else:
                if not second:
                    mst[2 + l, t, t:mid] = 1.0
                    msk[l, t, mid:start + 2 * m] = 1.0
                else:
                    mst[2 + l, t, mid:t] = 1.0
    return mst.reshape(-1, C), msk


def _hgrn_kernel(*refs, n_chunks, unroll, reverse, nh, final):
    q_ref, v_ref, z_ref, llb_ref, l1p_ref, s0_ref, mst_ref, msk_ref = refs[:8]
    if final:
        of_ref, gate_ref, hog_ref, o_ref, sf_ref, st_ref = refs[8:]
    else:
        o_ref, sf_ref, st_ref = refs[8:]
    C = HG_CHUNK
    tb = pl.program_id(2)
    ntb = pl.num_programs(2)

    @pl.when(tb == 0)
    def _():
        st_ref[...] = s0_ref[...]

    mst = mst_ref[...]
    row = lax.broadcasted_iota(jnp.int32, (C, HEAD_DIM), 0)
    nt = (((1,), (1,)), ((), ()))
    tn = (((0,), (0,)), ((), ()))

    def chunks(ci, carry):
        lanes = []
        for u in range(unroll):
            c = ci * unroll + u
            c = (n_chunks - 1 - c) if reverse else c
            rows = pl.ds(pl.multiple_of(c * C, C), C)
            lanes += [(rows, hh, slice(hh * HEAD_DIM, (hh + 1) * HEAD_DIM)) for hh in range(nh)]

        qs, vs, ks, gparts = [], [], [], []
        for rows, hh, cols in lanes:
            qr = q_ref[rows, cols]
            qs.append(qr * jax.nn.sigmoid(qr))
            vs.append(v_ref[rows, cols].astype(BF16))
            z = z_ref[rows, cols]
            t = l1p_ref[:, cols] + (jnp.minimum(z, 0.0) - jnp.log1p(jnp.exp(-jnp.abs(z))))
            llb = llb_ref[:, cols]
            g = jnp.maximum(llb, t) + jnp.log1p(jnp.exp(-jnp.abs(llb - t)))
            ks.append(1.0 - jnp.exp(g))
            g_hi = g.astype(BF16)
            gparts += [g_hi, (g - g_hi.astype(F32)).astype(BF16)]

        sums = jnp.dot(mst, jnp.concatenate(gparts, axis=1), preferred_element_type=F32)

        es, amats = [], []
        for n, (q, k) in enumerate(zip(qs, ks)):
            c0 = 2 * n * HEAD_DIM
            e = jnp.exp(sums[:, c0:c0 + HEAD_DIM] + sums[:, c0 + HEAD_DIM:c0 + 2 * HEAD_DIM])
            a = msk_ref[len(_HG_LEVELS)] * lax.dot_general(q.astype(BF16), k.astype(BF16), nt,
                                                           preferred_element_type=F32)
            for l, m in enumerate(_HG_LEVELS):
                is_q = ((row // m) % 2) == (0 if reverse else 1)
                x = (jnp.where(is_q, q, k) * e[(2 + l) * C:(3 + l) * C]).astype(BF16)
                a = a + msk_ref[l] * lax.dot_general(x, x, nt, preferred_element_type=F32)
            es.append(e)
            amats.append(a.astype(BF16))

        for (rows, hh, cols), q, v, k, e, a in zip(lanes, qs, vs, ks, es, amats):
            e_in = e[0:C]
            e_out = e[C:2 * C]
            e_tot = e_in[0:1] if reverse else e_in[C - 1:C]
            st = st_ref[hh]
            o = jnp.dot(a, v, preferred_element_type=F32)
            o = o + lax.dot_general((q * e_in).astype(BF16), st.astype(BF16), nt, preferred_element_type=F32)
            st_ref[hh] = st * e_tot + lax.dot_general(v, (k * e_out).astype(BF16), tn,
                                                     preferred_element_type=F32)
            if final:
                o = o + of_ref[rows, cols]
                o = o * lax.rsqrt(jnp.mean(o * o, axis=-1, keepdims=True) + EPS) * hog_ref[...]
                gt = gate_ref[rows, cols]
                o = o * (gt * jax.nn.sigmoid(gt))
            o_ref[rows, cols] = o.astype(o_ref.dtype)
        return carry

    lax.fori_loop(0, n_chunks // unroll, chunks, 0)

    @pl.when(tb == ntb - 1)
    def _():
        sf_ref[...] = st_ref[...]


def _hgrn_scan(p, slots, llb, l1p, s0, *, reverse, o_fwd=None, o_gain=None, tb=512):
    B, T, _ = p.shape
    W = llb.shape[1]
    H = W // HEAD_DIM
    final = o_fwd is not None
    nh = 2 if (H % 2 == 0 and all(s % 2 == 0 for s in slots)) else 1
    wb = nh * HEAD_DIM
    tb = _pick(T, tb, HG_CHUNK)
    n_chunks = tb // HG_CHUNK
    ntb = T // tb
    mst, msk = _hgrn_consts(reverse)

    def seq_spec(slot):
        s = slot // nh
        if reverse:
            return pl.BlockSpec((None, tb, wb), lambda b, h, t: (b, ntb - 1 - t, s + h))
        return pl.BlockSpec((None, tb, wb), lambda b, h, t: (b, t, s + h))

    vec_spec = pl.BlockSpec((1, wb), lambda b, h, t: (0, h))
    st_spec = pl.BlockSpec((None, nh, HEAD_DIM, HEAD_DIM), lambda b, h, t: (b, h, 0, 0))
    in_specs = [seq_spec(slots[0]), seq_spec(slots[1]), seq_spec(slots[2]), vec_spec, vec_spec, st_spec,
                pl.BlockSpec(mst.shape, lambda b, h, t: (0, 0)),
                pl.BlockSpec(msk.shape, lambda b, h, t: (0, 0, 0))]
    args = [p, p, p, llb, l1p, s0, jnp.asarray(mst, BF16), jnp.asarray(msk, F32)]
    if final:
        in_specs += [seq_spec(0), seq_spec(slots[3]), pl.BlockSpec((1, HEAD_DIM), lambda b, h, t: (0, 0))]
        args += [o_fwd, p, o_gain.reshape(1, HEAD_DIM).astype(F32)]
    return pl.pallas_call(
        functools.partial(_hgrn_kernel, n_chunks=n_chunks, unroll=math.gcd(n_chunks, HG_UNROLL), reverse=reverse,
                          nh=nh, final=final),
        grid=(B, H // nh, ntb),
        in_specs=in_specs,
        out_specs=[seq_spec(0), st_spec],
        out_shape=[jax.ShapeDtypeStruct((B, T, W), BF16 if final else F32),
                   jax.ShapeDtypeStruct((B, H, HEAD_DIM, HEAD_DIM), F32)],
        scratch_shapes=[pltpu.VMEM((nh, HEAD_DIM, HEAD_DIM), F32)],
        compiler_params=_params("parallel", "parallel", "arbitrary"),
        name="hgrn_bwd" if reverse else "hgrn_fwd",
    )(*args)


def _prep_kernel(x_ref, gain_ref, cos_ref, sin_ref, o_ref, *, ns, halves, rope, scale):
    gain = gain_ref[...]
    lane = lax.broadcasted_iota(jnp.int32, (x_ref.shape[0], HEAD_DIM), 1)
    low = lane < HEAD_DIM // 2
    even = (lane % 2) == 0
    for s in range(ns):
        cols = slice(s * HEAD_DIM, (s + 1) * HEAD_DIM)
        x = x_ref[:, cols]
        xx = x * x
        if halves:
            m0 = jnp.sum(jnp.where(low, xx, 0.0), axis=-1, keepdims=True) * (2.0 / HEAD_DIM)
            m1 = jnp.sum(jnp.where(low, 0.0, xx), axis=-1, keepdims=True) * (2.0 / HEAD_DIM)
            inv = jnp.where(low, lax.rsqrt(m0 + EPS), lax.rsqrt(m1 + EPS))
        else:
            inv = lax.rsqrt(jnp.mean(xx, axis=-1, keepdims=True) + EPS)
        y = x * inv * gain
        if rope:
            partner = jnp.where(even, pltpu.roll(y, HEAD_DIM - 1, 1), pltpu.roll(y, 1, 1))
            y = y * cos_ref[...] + partner * sin_ref[...]
        o_ref[:, cols] = (y * scale).astype(o_ref.dtype)


def _prep(p, slot, n_slots, gain, tables, *, halves, scale, seq_len, tm=1024):
    M = p.shape[0]
    ns = math.gcd(slot, n_slots)
    tm = _pick(seq_len, tm, 8)
    tiles_per_seq = seq_len // tm
    rope = tables is not None
    if rope:
        cos, sin = tables
        tab_spec = pl.BlockSpec((tm, HEAD_DIM), lambda i, j: (i % tiles_per_seq, 0))
    else:
        cos = sin = jnp.zeros((8, HEAD_DIM), F32)
        tab_spec = pl.BlockSpec((8, HEAD_DIM), lambda i, j: (0, 0))
    return pl.pallas_call(
        functools.partial(_prep_kernel, ns=ns, halves=halves, rope=rope, scale=scale),
        grid=(M // tm, n_slots // ns),
        in_specs=[pl.BlockSpec((tm, ns * HEAD_DIM), lambda i, j: (i, slot // ns + j)),
                  pl.BlockSpec((1, HEAD_DIM), lambda i, j: (0, 0)), tab_spec, tab_spec],
        out_specs=pl.BlockSpec((tm, ns * HEAD_DIM), lambda i, j: (i, j)),
        out_shape=jax.ShapeDtypeStruct((M, n_slots * HEAD_DIM), BF16),
        compiler_params=_params("parallel", "parallel"),
        name="prep_qk",
    )(p, gain.reshape(1, HEAD_DIM).astype(F32), cos, sin)


def _rope_tables(seq_len, dim):
    rows = seq_len // GRID_W
    row = jnp.repeat(jnp.arange(rows, dtype=F32), GRID_W)
    col = jnp.tile(jnp.arange(GRID_W, dtype=F32), rows)
    half = dim // 2
    inv = ROPE_THETA ** (-jnp.arange(0, half, 2, dtype=F32) / half)
    ang = jnp.concatenate([row[:, None] * inv, col[:, None] * inv], axis=-1)
    cos = jnp.repeat(jnp.cos(ang), 2, axis=-1)
    sin = jnp.repeat(jnp.sin(ang), 2, axis=-1) * jnp.tile(jnp.asarray([-1.0, 1.0], F32), half)
    reps = HEAD_DIM // dim
    return jnp.tile(cos, (1, reps)), jnp.tile(sin, (1, reps))


def _dims(D):
    n = D // HEAD_DIM
    a_heads = 3 * n // 8
    a_kv = a_heads // 3
    b_heads = (n - a_heads) // 2
    c_heads = n - a_heads - b_heads
    return a_heads, a_kv, b_heads, c_heads


def _layer(x2, y2, B, mod, rope_a, rope_c, lb, lam_init, last,
           w_in, w_out, aq_g, ak_g, ho_g, dq_g, dk_g, d_lam, do_g, w_gate, w_up, w_down):
    D = x2.shape[1]
    T = x2.shape[0] // B
    L = y2.shape[0] // B
    a_heads, a_kv, b_heads, c_heads = _dims(D)
    a_group = a_heads // a_kv
    half = HEAD_DIM // 2
    widths = (a_heads, c_heads, b_heads, b_heads, a_kv, a_kv, c_heads, c_heads, b_heads, b_heads, b_heads)
    (s_aq, s_cq, s_bq, s_bg, s_ak, s_av, s_ck, s_cv, s_bi, s_bff, s_bfb) = (
        np.concatenate([[0], np.cumsum(widths)[:-1]]).tolist())

    mx = [m[:, None, :] for m in jnp.split(mod[0:B], 6, axis=-1)]
    my = [m[:, None, :] for m in jnp.split(mod[B:B + 1], 6, axis=-1)]

    px = _mm(_norm_mod(x2, mx[0], mx[1]), w_in, F32)
    py = _mm(_norm_mod(y2, my[0], my[1]), w_in, F32)
    px3 = px.reshape(B, T, -1)
    py3 = py.reshape(B, L, -1)

    a_scale = HEAD_DIM ** -0.5 * LOG2E
    qa_x = _prep(px, s_aq, a_heads, aq_g, rope_a, halves=False, scale=a_scale, seq_len=T).reshape(B, T, -1)
    ka_x = _prep(px, s_ak, a_kv, ak_g, rope_a, halves=False, scale=1.0, seq_len=T).reshape(B, T, -1)
    ka_y = _prep(py, s_ak, a_kv, ak_g, None, halves=False, scale=1.0, seq_len=L).reshape(B, L, -1)
    oa_x = _attention(qa_x, ka_y, py3, ka_x, px3, G=a_group, v_slot=s_av)

    c_scale = half ** -0.5 * LOG2E
    dl = d_lam.astype(F32)
    lam = jnp.exp(jnp.sum(dl[0] * dl[1])) - jnp.exp(jnp.sum(dl[2] * dl[3])) + lam_init
    dq_g2 = jnp.tile(dq_g, 2)
    dk_g2 = jnp.tile(dk_g, 2)
    qc_x = _prep(px, s_cq, c_heads, dq_g2, rope_c, halves=True, scale=c_scale, seq_len=T).reshape(B, T, -1)
    kc_x = _prep(px, s_ck, c_heads, dk_g2, rope_c, halves=True, scale=1.0, seq_len=T).reshape(B, T, -1)
    kc_y = _prep(py, s_ck, c_heads, dk_g2, None, halves=True, scale=1.0, seq_len=L).reshape(B, L, -1)
    oc_x = _attention(qc_x, kc_y, py3, kc_x, px3, G=2, v_slot=s_cv, diff=True, lam=lam, gain=do_g,
                      post_scale=1.0 - lam_init)

    llb = jnp.log(lb).reshape(2, 1, -1)
    l1p = jnp.log1p(-lb).reshape(2, 1, -1)
    s0 = jnp.zeros((B, b_heads, HEAD_DIM, HEAD_DIM), F32)
    f_slots = (s_bq, s_bi, s_bff, s_bg)
    b_slots = (s_bq, s_bi, s_bfb, s_bg)
    of_y, s_f = _hgrn_scan(py3, f_slots, llb[0], l1p[0], s0, reverse=False)
    ob_y, s_b = _hgrn_scan(py3, b_slots, llb[1], l1p[1], s0, reverse=True, o_fwd=of_y, o_gain=ho_g)
    of_x, _ = _hgrn_scan(px3, f_slots, llb[0], l1p[0], s_f, reverse=False)
    ob_x, _ = _hgrn_scan(px3, b_slots, llb[1], l1p[1], s_b, reverse=True, o_fwd=of_x, o_gain=ho_g)

    x2 = _mm_res([oa_x.reshape(B * T, -1), ob_x.reshape(B * T, -1), oc_x.reshape(B * T, -1)], w_out, x2, mx[2])
    x2 = _mm_res([_mm_swiglu(_norm_mod(x2, mx[3], mx[4]), w_gate, w_up)], [w_down], x2, mx[5],
                 tn=FFN_DOWN_TN, tk_max=FFN_DOWN_TK)
    if last:
        return x2, None

    qa_y = _prep(py, s_aq, a_heads, aq_g, None, halves=False, scale=a_scale, seq_len=L).reshape(B, L, -1)
    oa_y = _attention(qa_y, ka_y, py3, None, None, G=a_group, v_slot=s_av)
    qc_y = _prep(py, s_cq, c_heads, dq_g2, None, halves=True, scale=c_scale, seq_len=L).reshape(B, L, -1)
    oc_y = _attention(qc_y, kc_y, py3, None, None, G=2, v_slot=s_cv, diff=True, lam=lam, gain=do_g,
                      post_scale=1.0 - lam_init)
    y2 = _mm_res([oa_y.reshape(B * L, -1), ob_y.reshape(B * L, -1), oc_y.reshape(B * L, -1)], w_out, y2, my[2])
    y2 = _mm_res([_mm_swiglu(_norm_mod(y2, my[3], my[4]), w_gate, w_up)], [w_down], y2, my[5],
                 tn=FFN_DOWN_TN, tk_max=FFN_DOWN_TK)
    return x2, y2


def kernel(x, c, ctx, c_ctx, w_ada, b_ada, w_in, w_out, attn_q_gain, attn_k_gain,
           hgrn_lb_logits, hgrn_o_gain, diff_q_gain, diff_k_gain, diff_lambda, diff_o_gain,
           w_gate, w_up, w_down):
    B, T, D = x.shape
    L = ctx.shape[1]
    depth = w_ada.shape[0]
    a_heads, _, b_heads, _ = _dims(D)
    rope_a = _rope_tables(T, HEAD_DIM)
    rope_c = _rope_tables(T, HEAD_DIM // 2)

    sm = jax.nn.softmax(hgrn_lb_logits.astype(F32), axis=0)
    lb = jnp.concatenate([jnp.zeros_like(sm[:1]), jnp.cumsum(sm[1:], axis=0)], axis=0)

    cs = jnp.concatenate([c, c_ctx[None, :]], axis=0)
    cs = jax.nn.silu(jnp.pad(cs, ((0, 8 - (B + 1)), (0, 0))))

    d_ff = w_gate.shape[-1]
    ff_pad = (-d_ff) % FF_PAD
    a_w = a_heads * HEAD_DIM
    b_w = b_heads * HEAD_DIM

    x2 = x.reshape(B * T, D)
    y2 = ctx.reshape(B * L, D)
    for l in range(depth):
        lam_init = 0.8 - 0.6 * math.exp(-0.3 * l)
        mod = _ada(cs, w_ada, b_ada.reshape(depth, 1, -1), l)
        wg = _cast_w(w_gate, l, 0, D, D, d_ff + ff_pad)
        wu = _cast_w(w_up, l, 0, D, D, d_ff + ff_pad)
        wd = _cast_w(w_down, l, 0, d_ff, d_ff + ff_pad, D)
        wo = [_cast_w(w_out, l, r0, n, n, D) for r0, n in ((0, a_w), (a_w, b_w), (a_w + b_w, D - a_w - b_w))]
        x2, y2 = _layer(x2, y2, B, mod, rope_a, rope_c, lb[l], lam_init, l == depth - 1,
                        _cast_w(w_in, l, 0, D, D, w_in.shape[2]), wo,
                        attn_q_gain[l], attn_k_gain[l], hgrn_o_gain[l], diff_q_gain[l], diff_k_gain[l],
                        diff_lambda[l], diff_o_gain[l], wg, wu, wd)
    return x2.reshape(B, T, D)
```

```python
import functools
import math

import numpy as np
import jax
import jax.numpy as jnp
from jax import lax
from jax.experimental import pallas as pl
from jax.experimental.pallas import tpu as pltpu

F32 = jnp.float32
BF16 = jnp.bfloat16

HEAD_DIM = 128
GRID_W = 64
HG_CHUNK = 64
HG_UNROLL = 4
ROPE_THETA = 10000.0
EPS = 1e-6
FF_PAD = 1024
FFN_DOWN_TN = 512
FFN_DOWN_TK = 5632
VMEM_LIMIT = 56 * 1024 * 1024


def _pick(dim, pref, align):
    t = min(pref, dim)
    t -= t % align
    while t >= align:
        if dim % t == 0:
            return t
        t -= align
    return dim


def _params(*sem):
    return pltpu.CompilerParams(dimension_semantics=sem, vmem_limit_bytes=VMEM_LIMIT)


def _mm_kernel(a_ref, w_ref, o_ref):
    o_ref[...] = jnp.dot(a_ref[...], w_ref[...], preferred_element_type=F32).astype(o_ref.dtype)


def _mm(a, w, out_dtype, tm=1024, tn=512):
    M, K = a.shape
    N = w.shape[1]
    tm = _pick(M, tm, 8)
    tn = _pick(N, tn, 128)
    return pl.pallas_call(
        _mm_kernel,
        grid=(M // tm, N // tn),
        in_specs=[pl.BlockSpec((tm, K), lambda i, j: (i, 0)),
                  pl.BlockSpec((K, tn), lambda i, j: (0, j))],
        out_specs=pl.BlockSpec((tm, tn), lambda i, j: (i, j)),
        out_shape=jax.ShapeDtypeStruct((M, N), out_dtype),
        compiler_params=_params("parallel", "arbitrary"),
        name="proj_in",
    )(a, w)


def _mm_res_kernel(*refs, nk, n_in):
    a_refs, w_refs = refs[:n_in], refs[n_in:2 * n_in]
    r_ref, g_ref, o_ref = refs[2 * n_in:2 * n_in + 3]
    d = jnp.dot(a_refs[0][...], w_refs[0][...], preferred_element_type=F32)
    for a_ref, w_ref in zip(a_refs[1:], w_refs[1:]):
        d = d + jnp.dot(a_ref[...], w_ref[...], preferred_element_type=F32)
    if nk == 1:
        o_ref[...] = r_ref[...] + g_ref[...] * d
        return
    acc_ref, = refs[2 * n_in + 3:]
    k = pl.program_id(2)

    @pl.when(k == 0)
    def _():
        acc_ref[...] = d

    @pl.when(k > 0)
    def _():
        acc_ref[...] += d

    @pl.when(k == nk - 1)
    def _():
        o_ref[...] = r_ref[...] + g_ref[...] * acc_ref[...]


def _mm_res(a_list, w_list, res, gate, tm=1024, tn=1024, tk_max=4096):
    M = a_list[0].shape[0]
    N = w_list[0].shape[1]
    G = gate.shape[0]
    n_in = len(a_list)
    tm = _pick(M // G, tm, 8)
    tn = _pick(N, tn, 128)
    K = a_list[0].shape[1]
    tk = K if (n_in > 1 or K <= tk_max) else _pick(K, tk_max, 256)
    nk = K // tk
    tiles_per_group = (M // G) // tm
    scratch = [] if nk == 1 else [pltpu.VMEM((tm, tn), F32)]
    if n_in == 1:
        a_specs = [pl.BlockSpec((tm, tk), lambda i, j, k: (i, k))]
        w_specs = [pl.BlockSpec((tk, tn), lambda i, j, k: (k, j))]
    else:
        a_specs = [pl.BlockSpec((tm, a.shape[1]), lambda i, j, k: (i, 0)) for a in a_list]
        w_specs = [pl.BlockSpec((w.shape[0], tn), lambda i, j, k: (0, j)) for w in w_list]
    return pl.pallas_call(
        functools.partial(_mm_res_kernel, nk=nk, n_in=n_in),
        grid=(M // tm, N // tn, nk),
        in_specs=a_specs + w_specs + [
            pl.BlockSpec((tm, tn), lambda i, j, k: (i, j)),
            pl.BlockSpec((None, 1, tn), lambda i, j, k: (i // tiles_per_group, 0, j))],
        out_specs=pl.BlockSpec((tm, tn), lambda i, j, k: (i, j)),
        out_shape=jax.ShapeDtypeStruct((M, N), F32),
        scratch_shapes=scratch,
        compiler_params=_params("parallel", "arbitrary", "arbitrary"),
        name="proj_res",
    )(*a_list, *w_list, res, gate)


def _mm_swiglu_kernel(a_ref, wg_ref, wu_ref, o_ref):
    a = a_ref[...]
    g = jnp.dot(a, wg_ref[...], preferred_element_type=F32)
    u = jnp.dot(a, wu_ref[...], preferred_element_type=F32)
    o_ref[...] = (g * jax.nn.sigmoid(g) * u).astype(o_ref.dtype)


def _mm_swiglu(a, wg, wu, tm=1024, tn=512):
    M, K = a.shape
    N = wg.shape[1]
    tm = _pick(M, tm, 8)
    tn = _pick(N, tn, 128)
    return pl.pallas_call(
        _mm_swiglu_kernel,
        grid=(M // tm, N // tn),
        in_specs=[pl.BlockSpec((tm, K), lambda i, j: (i, 0)),
                  pl.BlockSpec((K, tn), lambda i, j: (0, j)),
                  pl.BlockSpec((K, tn), lambda i, j: (0, j))],
        out_specs=pl.BlockSpec((tm, tn), lambda i, j: (i, j)),
        out_shape=jax.ShapeDtypeStruct((M, N), BF16),
        compiler_params=_params("parallel", "arbitrary"),
        name="ffn_up",
    )(a, wg, wu)


def _ada_kernel(c_ref, w_ref, b_ref, o_ref):
    o_ref[...] = jnp.dot(c_ref[...], w_ref[...], preferred_element_type=F32,
                         precision=lax.Precision.HIGHEST) + b_ref[...]


def _ada(cs, w, b, layer, tn=512):
    R, D = cs.shape
    N = w.shape[2]
    tn = _pick(N, tn, 128)
    return pl.pallas_call(
        _ada_kernel,
        grid=(N // tn,),
        in_specs=[pl.BlockSpec((R, D), lambda j: (0, 0)),
                  pl.BlockSpec((None, D, tn), lambda j: (layer, 0, j)),
                  pl.BlockSpec((None, 1, tn), lambda j: (layer, 0, j))],
        out_specs=pl.BlockSpec((R, tn), lambda j: (0, j)),
        out_shape=jax.ShapeDtypeStruct((R, N), F32),
        compiler_params=_params("arbitrary"),
        name="adaln",
    )(cs, w, b)


def _cast_kernel(w_ref, o_ref, *, rows_valid, cols_valid):
    tr, tc = o_ref.shape
    r = pl.program_id(0) * tr + lax.broadcasted_iota(jnp.int32, (tr, tc), 0)
    c = pl.program_id(1) * tc + lax.broadcasted_iota(jnp.int32, (tr, tc), 1)
    o_ref[...] = jnp.where((r < rows_valid) & (c < cols_valid), w_ref[...], 0.0).astype(o_ref.dtype)


def _cast_w(w, layer, row0, rows, rows_out, cols_out, tc=2048):
    cols = w.shape[2]
    tr = math.gcd(math.gcd(row0, rows_out), 512)
    tc = _pick(cols_out, tc, 128)
    assert (rows_out - tr) < rows and (cols_out - tc) < cols and row0 + rows <= w.shape[1]
    return pl.pallas_call(
        functools.partial(_cast_kernel, rows_valid=rows, cols_valid=cols),
        grid=(rows_out // tr, cols_out // tc),
        in_specs=[pl.BlockSpec((None, tr, tc), lambda i, j: (layer, row0 // tr + i, j))],
        out_specs=pl.BlockSpec((tr, tc), lambda i, j: (i, j)),
        out_shape=jax.ShapeDtypeStruct((rows_out, cols_out), BF16),
        compiler_params=_params("parallel", "parallel"),
        name="cast_w",
    )(w)


def _norm_mod_kernel(x_ref, sh_ref, sc_ref, o_ref):
    x = x_ref[...]
    y = x * lax.rsqrt(jnp.mean(x * x, axis=-1, keepdims=True) + EPS)
    o_ref[...] = (y * (1.0 + sc_ref[...]) + sh_ref[...]).astype(o_ref.dtype)


def _norm_mod(x, shift, scale, tm=256):
    M, D = x.shape
    G = shift.shape[0]
    tm = _pick(M // G, tm, 8)
    tiles_per_group = (M // G) // tm
    mod_spec = pl.BlockSpec((None, 1, D), lambda i: (i // tiles_per_group, 0, 0))
    return pl.pallas_call(
        _norm_mod_kernel,
        grid=(M // tm,),
        in_specs=[pl.BlockSpec((tm, D), lambda i: (i, 0)), mod_spec, mod_spec],
        out_specs=pl.BlockSpec((tm, D), lambda i: (i, 0)),
        out_shape=jax.ShapeDtypeStruct((M, D), BF16),
        compiler_params=_params("parallel"),
        name="norm_mod",
    )(x, shift, scale)


ATTN_PANEL = 512
LOG2E = math.log2(math.e)


def _attn_kernel(*refs, G, tq, tk, n_kv, diff, post_scale):
    if n_kv:
        q_ref, kc_ref, vc_ref, kx_ref, vx_ref, lam_ref, gain_ref, o_ref, qt_ref, m_ref, l_ref, acc_ref, s_ref = refs
    else:
        q_ref, kc_ref, vc_ref, lam_ref, gain_ref, o_ref, qt_ref, m_ref, l_ref, acc_ref = refs
    n_rows = 2 if diff else G
    width = n_rows * tq
    pw = min(ATTN_PANEL, width)
    starts = list(range(0, width, pw))
    tn = (((0,), (0,)), ((), ()))

    def scores(k, c0):
        return jnp.dot(k, qt_ref[:, c0:c0 + pw], preferred_element_type=F32)

    def update(s, v, c0, first):
        cols = slice(c0, c0 + pw)
        m_cur = jnp.max(s, axis=0, keepdims=True)
        if first:
            m_new = m_cur
            p = jnp.exp2(s - m_new)
            l_ref[:, cols] = jnp.sum(p, axis=0, keepdims=True)
            acc_ref[:, cols] = lax.dot_general(v, p.astype(BF16), tn, preferred_element_type=F32)
        else:
            m_prev = m_ref[:, cols]
            m_new = jnp.maximum(m_prev, m_cur)
            alpha = jnp.exp2(m_prev - m_new)
            p = jnp.exp2(s - m_new)
            l_ref[:, cols] = alpha * l_ref[:, cols] + jnp.sum(p, axis=0, keepdims=True)
            acc_ref[:, cols] = alpha * acc_ref[:, cols] + lax.dot_general(
                v, p.astype(BF16), tn, preferred_element_type=F32)
        m_ref[:, cols] = m_new

    q = q_ref[...].astype(F32)
    if diff:
        lane = lax.broadcasted_iota(jnp.int32, q.shape, 1)
        qt_ref[:, 0:tq] = jnp.where(lane < HEAD_DIM // 2, q, 0.0).T.astype(BF16)
        qt_ref[:, tq:2 * tq] = jnp.where(lane >= HEAD_DIM // 2, q, 0.0).T.astype(BF16)
    else:
        for g in range(G):
            qt_ref[:, g * tq:(g + 1) * tq] = q[:, g * HEAD_DIM:(g + 1) * HEAD_DIM].T.astype(BF16)

    kc = kc_ref[...]
    vc = vc_ref[...].astype(BF16)
    s = scores(kc, starts[0])
    for n, c0 in enumerate(starts):
        if n + 1 < len(starts):
            s_next = scores(kc, starts[n + 1])
        elif n_kv:
            s_next = scores(kx_ref[0:tk, :], starts[0])
        else:
            s_next = None
        update(s, vc, c0, True)
        s = s_next

    if n_kv:
        s_ref[...] = s

        def block(jb, carry):
            r0 = pl.multiple_of(jb * tk, tk)
            r1 = pl.multiple_of(jnp.minimum(jb + 1, n_kv - 1) * tk, tk)
            k = kx_ref[pl.ds(r0, tk), :]
            v = vx_ref[pl.ds(r0, tk), :].astype(BF16)
            s = s_ref[...]
            for n, c0 in enumerate(starts):
                if n + 1 < len(starts):
                    s_next = scores(k, starts[n + 1])
                else:
                    s_next = scores(kx_ref[pl.ds(r1, tk), :], starts[0])
                update(s, v, c0, False)
                s = s_next
            s_ref[...] = s
            return carry

        lax.fori_loop(0, n_kv, block, 0)

    o = acc_ref[...] * (1.0 / l_ref[...])
    if diff:
        d = o[:, 0:tq] - lam_ref[0] * o[:, tq:2 * tq]
        d = d * lax.rsqrt(jnp.mean(d * d, axis=0, keepdims=True) + EPS)
        o_ref[...] = (d * gain_ref[...] * post_scale).T.astype(o_ref.dtype)
    else:
        for g in range(G):
            o_ref[:, g * HEAD_DIM:(g + 1) * HEAD_DIM] = o[:, g * tq:(g + 1) * tq].T.astype(o_ref.dtype)


def _attention(q, kc, vc, kx, vx, *, G, v_slot, diff=False, lam=None, gain=None, post_scale=1.0, tq=2048, tk=1024):
    B, T, _ = q.shape
    L = kc.shape[1]
    H = kc.shape[2] // HEAD_DIM
    gq = 1 if diff else G
    n_rows = 2 if diff else G
    tq = _pick(T, tq, 128)
    if lam is None:
        lam = jnp.zeros((1,), F32)
    if gain is None:
        gain = jnp.ones((HEAD_DIM,), F32)
    in_specs = [pl.BlockSpec((None, tq, gq * HEAD_DIM), lambda b, h, i: (b, i, h)),
                pl.BlockSpec((None, L, HEAD_DIM), lambda b, h, i: (b, 0, h)),
                pl.BlockSpec((None, L, HEAD_DIM), lambda b, h, i: (b, 0, v_slot + h))]
    args = [q, kc, vc]
    scratch = [pltpu.VMEM((HEAD_DIM, n_rows * tq), BF16),
               pltpu.VMEM((1, n_rows * tq), F32),
               pltpu.VMEM((1, n_rows * tq), F32),
               pltpu.VMEM((HEAD_DIM, n_rows * tq), F32)]
    if kx is not None:
        S = kx.shape[1]
        tk = _pick(S, tk, 128)
        n_kv = S // tk
        in_specs += [pl.BlockSpec((None, S, HEAD_DIM), lambda b, h, i: (b, 0, h)),
                     pl.BlockSpec((None, S, HEAD_DIM), lambda b, h, i: (b, 0, v_slot + h))]
        args += [kx, vx]
        scratch += [pltpu.VMEM((tk, min(ATTN_PANEL, n_rows * tq)), F32)]
    else:
        n_kv = 0
    in_specs += [pl.BlockSpec(memory_space=pltpu.SMEM),
                 pl.BlockSpec((HEAD_DIM, 1), lambda b, h, i: (0, 0))]
    args += [lam.reshape(1).astype(F32), gain.reshape(HEAD_DIM, 1).astype(F32)]
    return pl.pallas_call(
        functools.partial(_attn_kernel, G=G, tq=tq, tk=tk, n_kv=n_kv, diff=diff, post_scale=post_scale),
        grid=(B, H, T // tq),
        in_specs=in_specs,
        out_specs=pl.BlockSpec((None, tq, gq * HEAD_DIM), lambda b, h, i: (b, i, h)),
        out_shape=jax.ShapeDtypeStruct((B, T, H * gq * HEAD_DIM), BF16),
        scratch_shapes=scratch,
        compiler_params=_params("parallel", "parallel", "arbitrary"),
        name="attn_diff" if diff else "attn_gqa",
    )(*args)


_HG_LEVELS = (32, 16, 8, 4, 2, 1)


def _hgrn_consts(reverse):
    C = HG_CHUNK
    mst = np.zeros((2 + len(_HG_LEVELS), C, C), np.float32)
    msk = np.zeros((len(_HG_LEVELS) + 1, C, C), np.float32)
    for t in range(C):
        if not reverse:
            mst[0, t, :t + 1] = 1.0
            mst[1, t, t + 1:] = 1.0
        else:
            mst[0, t, t:] = 1.0
            mst[1, t, :t] = 1.0
        msk[len(_HG_LEVELS), t, t] = 1.0
    for l, m in enumerate(_HG_LEVELS):
        for t in range(C):
            start = (t // (2 * m)) * 2 * m
            mid = start + m
            second = t >= mid
            if not reverse:
                if second:
                    mst[2 + l, t, mid:t + 1] = 1.0
                    msk[l, t, start:mid] = 1.0
                else:
                    mst[2 + l, t, t + 1:mid] = 1.0
            else:
                if not second:
                    mst[2 + l, t, t:mid] = 1.0
                    msk[l, t, mid:start + 2 * m] = 1.0
                else:
                    mst[2 + l, t, mid:t] = 1.0
    return mst.reshape(-1, C), msk


def _hgrn_kernel(*refs, n_chunks, unroll, reverse, nh, final):
    q_ref, v_ref, z_ref, llb_ref, l1p_ref, s0_ref, mst_ref, msk_ref = refs[:8]
    if final:
        of_ref, gate_ref, hog_ref, o_ref, sf_ref, st_ref = refs[8:]
    else:
        o_ref, sf_ref, st_ref = refs[8:]
    C = HG_CHUNK
    tb = pl.program_id(2)
    ntb = pl.num_programs(2)

    @pl.when(tb == 0)
    def _():
        st_ref[...] = s0_ref[...]

    mst = mst_ref[...]
    row = lax.broadcasted_iota(jnp.int32, (C, HEAD_DIM), 0)
    nt = (((1,), (1,)), ((), ()))
    tn = (((0,), (0,)), ((), ()))

    def chunks(ci, carry):
        lanes = []
        for u in range(unroll):
            c = ci * unroll + u
            c = (n_chunks - 1 - c) if reverse else c
            rows = pl.ds(pl.multiple_of(c * C, C), C)
            lanes += [(rows, hh, slice(hh * HEAD_DIM, (hh + 1) * HEAD_DIM)) for hh in range(nh)]

        qs, vs, ks, gparts = [], [], [], []
        for rows, hh, cols in lanes:
            qr = q_ref[rows, cols]
            qs.append(qr * jax.nn.sigmoid(qr))
            vs.append(v_ref[rows, cols].astype(BF16))
            z = z_ref[rows, cols]
            t = l1p_ref[:, cols] + (jnp.minimum(z, 0.0) - jnp.log1p(jnp.exp(-jnp.abs(z))))
            llb = llb_ref[:, cols]
            g = jnp.maximum(llb, t) + jnp.log1p(jnp.exp(-jnp.abs(llb - t)))
            ks.append(1.0 - jnp.exp(g))
            g_hi = g.astype(BF16)
            gparts += [g_hi, (g - g_hi.astype(F32)).astype(BF16)]

        sums = jnp.dot(mst, jnp.concatenate(gparts, axis=1), preferred_element_type=F32)

        es, amats = [], []
        for n, (q, k) in enumerate(zip(qs, ks)):
            c0 = 2 * n * HEAD_DIM
            e = jnp.exp(sums[:, c0:c0 + HEAD_DIM] + sums[:, c0 + HEAD_DIM:c0 + 2 * HEAD_DIM])
            a = msk_ref[len(_HG_LEVELS)] * lax.dot_general(q.astype(BF16), k.astype(BF16), nt,
                                                           preferred_element_type=F32)
            for l, m in enumerate(_HG_LEVELS):
                is_q = ((row // m) % 2) == (0 if reverse else 1)
                x = (jnp.where(is_q, q, k) * e[(2 + l) * C:(3 + l) * C]).astype(BF16)
                a = a + msk_ref[l] * lax.dot_general(x, x, nt, preferred_element_type=F32)
            es.append(e)
            amats.append(a.astype(BF16))

        for (rows, hh, cols), q, v, k, e, a in zip(lanes, qs, vs, ks, es, amats):
            e_in = e[0:C]
            e_out = e[C:2 * C]
            e_tot = e_in[0:1] if reverse else e_in[C - 1:C]
            st = st_ref[hh]
            o = jnp.dot(a, v, preferred_element_type=F32)
            o = o + lax.dot_general((q * e_in).astype(BF16), st.astype(BF16), nt, preferred_element_type=F32)
            st_ref[hh] = st * e_tot + lax.dot_general(v, (k * e_out).astype(BF16), tn,
                                                     preferred_element_type=F32)
            if final:
                o = o + of_ref[rows, cols]
                o = o * lax.rsqrt(jnp.mean(o * o, axis=-1, keepdims=True) + EPS) * hog_ref[...]
                gt = gate_ref[rows, cols]
                o = o * (gt * jax.nn.sigmoid(gt))
            o_ref[rows, cols] = o.astype(o_ref.dtype)
        return carry

    lax.fori_loop(0, n_chunks // unroll, chunks, 0)

    @pl.when(tb == ntb - 1)
    def _():
        sf_ref[...] = st_ref[...]


def _hgrn_scan(p, slots, llb, l1p, s0, *, reverse, o_fwd=None, o_gain=None, tb=512):
    B, T, _ = p.shape
    W = llb.shape[1]
    H = W // HEAD_DIM
    final = o_fwd is not None
    nh = 2 if (H % 2 == 0 and all(s % 2 == 0 for s in slots)) else 1
    wb = nh * HEAD_DIM
    tb = _pick(T, tb, HG_CHUNK)
    n_chunks = tb // HG_CHUNK
    ntb = T // tb
    mst, msk = _hgrn_consts(reverse)

    def seq_spec(slot):
        s = slot // nh
        if reverse:
            return pl.BlockSpec((None, tb, wb), lambda b, h, t: (b, ntb - 1 - t, s + h))
        return pl.BlockSpec((None, tb, wb), lambda b, h, t: (b, t, s + h))

    vec_spec = pl.BlockSpec((1, wb), lambda b, h, t: (0, h))
    st_spec = pl.BlockSpec((None, nh, HEAD_DIM, HEAD_DIM), lambda b, h, t: (b, h, 0, 0))
    in_specs = [seq_spec(slots[0]), seq_spec(slots[1]), seq_spec(slots[2]), vec_spec, vec_spec, st_spec,
                pl.BlockSpec(mst.shape, lambda b, h, t: (0, 0)),
                pl.BlockSpec(msk.shape, lambda b, h, t: (0, 0, 0))]
    args = [p, p, p, llb, l1p, s0, jnp.asarray(mst, BF16), jnp.asarray(msk, F32)]
    if final:
        in_specs += [seq_spec(0), seq_spec(slots[3]), pl.BlockSpec((1, HEAD_DIM), lambda b, h, t: (0, 0))]
        args += [o_fwd, p, o_gain.reshape(1, HEAD_DIM).astype(F32)]
    return pl.pallas_call(
        functools.partial(_hgrn_kernel, n_chunks=n_chunks, unroll=math.gcd(n_chunks, HG_UNROLL), reverse=reverse,
                          nh=nh, final=final),
        grid=(B, H // nh, ntb),
        in_specs=in_specs,
        out_specs=[seq_spec(0), st_spec],
        out_shape=[jax.ShapeDtypeStruct((B, T, W), BF16 if final else F32),
                   jax.ShapeDtypeStruct((B, H, HEAD_DIM, HEAD_DIM), F32)],
        scratch_shapes=[pltpu.VMEM((nh, HEAD_DIM, HEAD_DIM), F32)],
        compiler_params=_params("parallel", "parallel", "arbitrary"),
        name="hgrn_bwd" if reverse else "hgrn_fwd",
    )(*args)


def _prep_kernel(x_ref, gain_ref, cos_ref, sin_ref, o_ref, *, ns, halves, rope, scale):
    gain = gain_ref[...]
    lane = lax.broadcasted_iota(jnp.int32, (x_ref.shape[0], HEAD_DIM), 1)
    low = lane < HEAD_DIM // 2
    even = (lane % 2) == 0
    for s in range(ns):
        cols = slice(s * HEAD_DIM, (s + 1) * HEAD_DIM)
        x = x_ref[:, cols]
        xx = x * x
        if halves:
            m0 = jnp.sum(jnp.where(low, xx, 0.0), axis=-1, keepdims=True) * (2.0 / HEAD_DIM)
            m1 = jnp.sum(jnp.where(low, 0.0, xx), axis=-1, keepdims=True) * (2.0 / HEAD_DIM)
            inv = jnp.where(low, lax.rsqrt(m0 + EPS), lax.rsqrt(m1 + EPS))
        else:
            inv = lax.rsqrt(jnp.mean(xx, axis=-1, keepdims=True) + EPS)
        y = x * inv * gain
        if rope:
            partner = jnp.where(even, pltpu.roll(y, HEAD_DIM - 1, 1), pltpu.roll(y, 1, 1))
            y = y * cos_ref[...] + partner * sin_ref[...]
        o_ref[:, cols] = (y * scale).astype(o_ref.dtype)


def _prep(p, slot, n_slots, gain, tables, *, halves, scale, seq_len, tm=1024):
    M = p.shape[0]
    ns = math.gcd(slot, n_slots)
    tm = _pick(seq_len, tm, 8)
    tiles_per_seq = seq_len // tm
    rope = tables is not None
    if rope:
        cos, sin = tables
        tab_spec = pl.BlockSpec((tm, HEAD_DIM), lambda i, j: (i % tiles_per_seq, 0))
    else:
        cos = sin = jnp.zeros((8, HEAD_DIM), F32)
        tab_spec = pl.BlockSpec((8, HEAD_DIM), lambda i, j: (0, 0))
    return pl.pallas_call(
        functools.partial(_prep_kernel, ns=ns, halves=halves, rope=rope, scale=scale),
        grid=(M // tm, n_slots // ns),
        in_specs=[pl.BlockSpec((tm, ns * HEAD_DIM), lambda i, j: (i, slot // ns + j)),
                  pl.BlockSpec((1, HEAD_DIM), lambda i, j: (0, 0)), tab_spec, tab_spec],
        out_specs=pl.BlockSpec((tm, ns * HEAD_DIM), lambda i, j: (i, j)),
        out_shape=jax.ShapeDtypeStruct((M, n_slots * HEAD_DIM), BF16),
        compiler_params=_params("parallel", "parallel"),
        name="prep_qk",
    )(p, gain.reshape(1, HEAD_DIM).astype(F32), cos, sin)


def _rope_tables(seq_len, dim):
    rows = seq_len // GRID_W
    row = jnp.repeat(jnp.arange(rows, dtype=F32), GRID_W)
    col = jnp.tile(jnp.arange(GRID_W, dtype=F32), rows)
    half = dim // 2
    inv = ROPE_THETA ** (-jnp.arange(0, half, 2, dtype=F32) / half)
    ang = jnp.concatenate([row[:, None] * inv, col[:, None] * inv], axis=-1)
    cos = jnp.repeat(jnp.cos(ang), 2, axis=-1)
    sin = jnp.repeat(jnp.sin(ang), 2, axis=-1) * jnp.tile(jnp.asarray([-1.0, 1.0], F32), half)
    reps = HEAD_DIM // dim
    return jnp.tile(cos, (1, reps)), jnp.tile(sin, (1, reps))


def _dims(D):
    n = D // HEAD_DIM
    a_heads = 3 * n // 8
    a_kv = a_heads // 3
    b_heads = (n - a_heads) // 2
    c_heads = n - a_heads - b_heads
    return a_heads, a_kv, b_heads, c_heads


def _layer(x2, y2, B, mod, rope_a, rope_c, lb, lam_init, last,
           w_in, w_out, aq_g, ak_g, ho_g, dq_g, dk_g, d_lam, do_g, w_gate, w_up, w_down):
    D = x2.shape[1]
    T = x2.shape[0] // B
    L = y2.shape[0] // B
    a_heads, a_kv, b_heads, c_heads = _dims(D)
    a_group = a_heads // a_kv
    half = HEAD_DIM // 2
    widths = (a_heads, c_heads, b_heads, b_heads, a_kv, a_kv, c_heads, c_heads, b_heads, b_heads, b_heads)
    (s_aq, s_cq, s_bq, s_bg, s_ak, s_av, s_ck, s_cv, s_bi, s_bff, s_bfb) = (
        np.concatenate([[0], np.cumsum(widths)[:-1]]).tolist())

    mx = [m[:, None, :] for m in jnp.split(mod[0:B], 6, axis=-1)]
    my = [m[:, None, :] for m in jnp.split(mod[B:B + 1], 6, axis=-1)]

    px = _mm(_norm_mod(x2, mx[0], mx[1]), w_in, F32)
    py = _mm(_norm_mod(y2, my[0], my[1]), w_in, F32)
    px3 = px.reshape(B, T, -1)
    py3 = py.reshape(B, L, -1)

    a_scale = HEAD_DIM ** -0.5 * LOG2E
    qa_x = _prep(px, s_aq, a_heads, aq_g, rope_a, halves=False, scale=a_scale, seq_len=T).reshape(B, T, -1)
    ka_x = _prep(px, s_ak, a_kv, ak_g, rope_a, halves=False, scale=1.0, seq_len=T).reshape(B, T, -1)
    ka_y = _prep(py, s_ak, a_kv, ak_g, None, halves=False, scale=1.0, seq_len=L).reshape(B, L, -1)
    oa_x = _attention(qa_x, ka_y, py3, ka_x, px3, G=a_group, v_slot=s_av)

    c_scale = half ** -0.5 * LOG2E
    dl = d_lam.astype(F32)
    lam = jnp.exp(jnp.sum(dl[0] * dl[1])) - jnp.exp(jnp.sum(dl[2] * dl[3])) + lam_init
    dq_g2 = jnp.tile(dq_g, 2)
    dk_g2 = jnp.tile(dk_g, 2)
    qc_x = _prep(px, s_cq, c_heads, dq_g2, rope_c, halves=True, scale=c_scale, seq_len=T).reshape(B, T, -1)
    kc_x = _prep(px, s_ck, c_heads, dk_g2, rope_c, halves=True, scale=1.0, seq_len=T).reshape(B, T, -1)
    kc_y = _prep(py, s_ck, c_heads, dk_g2, None, halves=True, scale=1.0, seq_len=L).reshape(B, L, -1)
    oc_x = _attention(qc_x, kc_y, py3, kc_x, px3, G=2, v_slot=s_cv, diff=True, lam=lam, gain=do_g,
                      post_scale=1.0 - lam_init)

    llb = jnp.log(lb).reshape(2, 1, -1)
    l1p = jnp.log1p(-lb).reshape(2, 1, -1)
    s0 = jnp.zeros((B, b_heads, HEAD_DIM, HEAD_DIM), F32)
    f_slots = (s_bq, s_bi, s_bff, s_bg)
    b_slots = (s_bq, s_bi, s_bfb, s_bg)
    of_y, s_f = _hgrn_scan(py3, f_slots, llb[0], l1p[0], s0, reverse=False)
    ob_y, s_b = _hgrn_scan(py3, b_slots, llb[1], l1p[1], s0, reverse=True, o_fwd=of_y, o_gain=ho_g)
    of_x, _ = _hgrn_scan(px3, f_slots, llb[0], l1p[0], s_f, reverse=False)
    ob_x, _ = _hgrn_scan(px3, b_slots, llb[1], l1p[1], s_b, reverse=True, o_fwd=of_x, o_gain=ho_g)

    x2 = _mm_res([oa_x.reshape(B * T, -1), ob_x.reshape(B * T, -1), oc_x.reshape(B * T, -1)], w_out, x2, mx[2])
    x2 = _mm_res([_mm_swiglu(_norm_mod(x2, mx[3], mx[4]), w_gate, w_up)], [w_down], x2, mx[5],
                 tn=FFN_DOWN_TN, tk_max=FFN_DOWN_TK)
    if last:
        return x2, None

    qa_y = _prep(py, s_aq, a_heads, aq_g, None, halves=False, scale=a_scale, seq_len=L).reshape(B, L, -1)
    oa_y = _attention(qa_y, ka_y, py3, None, None, G=a_group, v_slot=s_av)
    qc_y = _prep(py, s_cq, c_heads, dq_g2, None, halves=True, scale=c_scale, seq_len=L).reshape(B, L, -1)
    oc_y = _attention(qc_y, kc_y, py3, None, None, G=2, v_slot=s_cv, diff=True, lam=lam, gain=do_g,
                      post_scale=1.0 - lam_init)
    y2 = _mm_res([oa_y.reshape(B * L, -1), ob_y.reshape(B * L, -1), oc_y.reshape(B * L, -1)], w_out, y2, my[2])
    y2 = _mm_res([_mm_swiglu(_norm_mod(y2, my[3], my[4]), w_gate, w_up)], [w_down], y2, my[5],
                 tn=FFN_DOWN_TN, tk_max=FFN_DOWN_TK)
    return x2, y2


def kernel(x, c, ctx, c_ctx, w_ada, b_ada, w_in, w_out, attn_q_gain, attn_k_gain,
           hgrn_lb_logits, hgrn_o_gain, diff_q_gain, diff_k_gain, diff_lambda, diff_o_gain,
           w_gate, w_up, w_down):
    B, T, D = x.shape
    L = ctx.shape[1]
    depth = w_ada.shape[0]
    a_heads, _, b_heads, _ = _dims(D)
    rope_a = _rope_tables(T, HEAD_DIM)
    rope_c = _rope_tables(T, HEAD_DIM // 2)

    sm = jax.nn.softmax(hgrn_lb_logits.astype(F32), axis=0)
    lb = jnp.concatenate([jnp.zeros_like(sm[:1]), jnp.cumsum(sm[1:], axis=0)], axis=0)

    cs = jnp.concatenate([c, c_ctx[None, :]], axis=0)
    cs = jax.nn.silu(jnp.pad(cs, ((0, 8 - (B + 1)), (0, 0))))

    d_ff = w_gate.shape[-1]
    ff_pad = (-d_ff) % FF_PAD
    a_w = a_heads * HEAD_DIM
    b_w = b_heads * HEAD_DIM

    x2 = x.reshape(B * T, D)
    y2 = ctx.reshape(B * L, D)
    for l in range(depth):
        lam_init = 0.8 - 0.6 * math.exp(-0.3 * l)
        mod = _ada(cs, w_ada, b_ada.reshape(depth, 1, -1), l)
        wg = _cast_w(w_gate, l, 0, D, D, d_ff + ff_pad)
        wu = _cast_w(w_up, l, 0, D, D, d_ff + ff_pad)
        wd = _cast_w(w_down, l, 0, d_ff, d_ff + ff_pad, D)
        wo = [_cast_w(w_out, l, r0, n, n, D) for r0, n in ((0, a_w), (a_w, b_w), (a_w + b_w, D - a_w - b_w))]
        x2, y2 = _layer(x2, y2, B, mod, rope_a, rope_c, lb[l], lam_init, l == depth - 1,
                        _cast_w(w_in, l, 0, D, D, w_in.shape[2]), wo,
                        attn_q_gain[l], attn_k_gain[l], hgrn_o_gain[l], diff_q_gain[l], diff_k_gain[l],
                        diff_lambda[l], diff_o_gain[l], wg, wu, wd)
    return x2.reshape(B, T, D)
```

```python
import functools
import math

import numpy as np
import jax
import jax.numpy as jnp
from jax import lax
from jax.experimental import pallas as pl
from jax.experimental.pallas import tpu as pltpu

F32 = jnp.float32
BF16 = jnp.bfloat16

HEAD_DIM = 128
GRID_W = 64
HG_CHUNK = 64
HG_UNROLL = 8
ROPE_THETA = 10000.0
EPS = 1e-6
FF_PAD = 256
FFN_UP_TILES = (2048, 256)
FFN_DOWN_TILES = (512, 512)
VMEM_LIMIT = 56 * 1024 * 1024


def _pick(dim, pref, align):
    t = min(pref, dim)
    t -= t % align
    while t >= align:
        if dim % t == 0:
            return t
        t -= align
    return dim


def _params(*sem):
    return pltpu.CompilerParams(dimension_semantics=sem, vmem_limit_bytes=VMEM_LIMIT)


def _mm_kernel(a_ref, w_ref, o_ref):
    o_ref[...] = jnp.dot(a_ref[...], w_ref[...], preferred_element_type=F32).astype(o_ref.dtype)


def _mm(a, w, out_dtype, tm=1024, tn=512):
    M, K = a.shape
    N = w.shape[1]
    tm = _pick(M, tm, 8)
    tn = _pick(N, tn, 128)
    return pl.pallas_call(
        _mm_kernel,
        grid=(M // tm, N // tn),
        in_specs=[pl.BlockSpec((tm, K), lambda i, j: (i, 0)),
                  pl.BlockSpec((K, tn), lambda i, j: (0, j))],
        out_specs=pl.BlockSpec((tm, tn), lambda i, j: (i, j)),
        out_shape=jax.ShapeDtypeStruct((M, N), out_dtype),
        compiler_params=_params("parallel", "arbitrary"),
        name="proj_in",
    )(a, w)


def _mm_res_kernel(*refs, nk, n_in):
    a_refs, w_refs = refs[:n_in], refs[n_in:2 * n_in]
    r_ref, g_ref, o_ref = refs[2 * n_in:2 * n_in + 3]
    d = jnp.dot(a_refs[0][...], w_refs[0][...], preferred_element_type=F32)
    for a_ref, w_ref in zip(a_refs[1:], w_refs[1:]):
        d = d + jnp.dot(a_ref[...], w_ref[...], preferred_element_type=F32)
    if nk == 1:
        o_ref[...] = r_ref[...] + g_ref[...] * d
        return
    acc_ref, = refs[2 * n_in + 3:]
    k = pl.program_id(2)

    @pl.when(k == 0)
    def _():
        acc_ref[...] = d

    @pl.when(k > 0)
    def _():
        acc_ref[...] += d

    @pl.when(k == nk - 1)
    def _():
        o_ref[...] = r_ref[...] + g_ref[...] * acc_ref[...]


def _mm_res(a_list, w_list, res, gate, tm=1024, tn=1024, tk_max=4096):
    M = a_list[0].shape[0]
    N = w_list[0].shape[1]
    G = gate.shape[0]
    n_in = len(a_list)
    tm = _pick(M // G, tm, 8)
    tn = _pick(N, tn, 128)
    K = a_list[0].shape[1]
    tk = K if (n_in > 1 or K <= tk_max) else _pick(K, tk_max, 256)
    nk = K // tk
    tiles_per_group = (M // G) // tm
    scratch = [] if nk == 1 else [pltpu.VMEM((tm, tn), F32)]
    if n_in == 1:
        a_specs = [pl.BlockSpec((tm, tk), lambda i, j, k: (i, k))]
        w_specs = [pl.BlockSpec((tk, tn), lambda i, j, k: (k, j))]
    else:
        a_specs = [pl.BlockSpec((tm, a.shape[1]), lambda i, j, k: (i, 0)) for a in a_list]
        w_specs = [pl.BlockSpec((w.shape[0], tn), lambda i, j, k: (0, j)) for w in w_list]
    return pl.pallas_call(
        functools.partial(_mm_res_kernel, nk=nk, n_in=n_in),
        grid=(M // tm, N // tn, nk),
        in_specs=a_specs + w_specs + [
            pl.BlockSpec((tm, tn), lambda i, j, k: (i, j)),
            pl.BlockSpec((None, 1, tn), lambda i, j, k: (i // tiles_per_group, 0, j))],
        out_specs=pl.BlockSpec((tm, tn), lambda i, j, k: (i, j)),
        out_shape=jax.ShapeDtypeStruct((M, N), F32),
        scratch_shapes=scratch,
        compiler_params=_params("parallel", "arbitrary", "arbitrary"),
        name="proj_res",
    )(*a_list, *w_list, res, gate)


def _mm_swiglu_kernel(a_ref, wg_ref, wu_ref, o_ref):
    a = a_ref[...]
    g = jnp.dot(a, wg_ref[...], preferred_element_type=F32)
    u = jnp.dot(a, wu_ref[...], preferred_element_type=F32)
    o_ref[...] = (g * jax.nn.sigmoid(g) * u).astype(o_ref.dtype)


def _mm_swiglu(a, wg, wu, tm=FFN_UP_TILES[0], tn=FFN_UP_TILES[1]):
    M, K = a.shape
    N = wg.shape[1]
    tm = _pick(M, tm, 8)
    tn = _pick(N, tn, 128)
    return pl.pallas_call(
        _mm_swiglu_kernel,
        grid=(M // tm, N // tn),
        in_specs=[pl.BlockSpec((tm, K), lambda i, j: (i, 0)),
                  pl.BlockSpec((K, tn), lambda i, j: (0, j)),
                  pl.BlockSpec((K, tn), lambda i, j: (0, j))],
        out_specs=pl.BlockSpec((tm, tn), lambda i, j: (i, j)),
        out_shape=jax.ShapeDtypeStruct((M, N), BF16),
        compiler_params=_params("parallel", "arbitrary"),
        name="ffn_up",
    )(a, wg, wu)


def _ada_kernel(c_ref, w_ref, b_ref, o_ref):
    o_ref[...] = jnp.dot(c_ref[...], w_ref[...], preferred_element_type=F32,
                         precision=lax.Precision.HIGHEST) + b_ref[...]


def _ada(cs, w, b, layer, tn=1024):
    R, D = cs.shape
    N = w.shape[2]
    tn = _pick(N, tn, 128)
    return pl.pallas_call(
        _ada_kernel,
        grid=(N // tn,),
        in_specs=[pl.BlockSpec((R, D), lambda j: (0, 0)),
                  pl.BlockSpec((None, D, tn), lambda j: (layer, 0, j)),
                  pl.BlockSpec((None, 1, tn), lambda j: (layer, 0, j))],
        out_specs=pl.BlockSpec((R, tn), lambda j: (0, j)),
        out_shape=jax.ShapeDtypeStruct((R, N), F32),
        compiler_params=_params("arbitrary"),
        name="adaln",
    )(cs, w, b)


def _cast_kernel(w_ref, o_ref, *, rows_valid, cols_valid):
    tr, tc = o_ref.shape
    r = pl.program_id(0) * tr + lax.broadcasted_iota(jnp.int32, (tr, tc), 0)
    c = pl.program_id(1) * tc + lax.broadcasted_iota(jnp.int32, (tr, tc), 1)
    o_ref[...] = jnp.where((r < rows_valid) & (c < cols_valid), w_ref[...], 0.0).astype(o_ref.dtype)


def _cast_w(w, layer, row0, rows, rows_out, cols_out, tc=2048):
    cols = w.shape[2]
    tr = math.gcd(math.gcd(row0, rows_out), 2048)
    tc = _pick(cols_out, tc, 128)
    assert (rows_out - tr) < rows and (cols_out - tc) < cols and row0 + rows <= w.shape[1]
    return pl.pallas_call(
        functools.partial(_cast_kernel, rows_valid=rows, cols_valid=cols),
        grid=(rows_out // tr, cols_out // tc),
        in_specs=[pl.BlockSpec((None, tr, tc), lambda i, j: (layer, row0 // tr + i, j))],
        out_specs=pl.BlockSpec((tr, tc), lambda i, j: (i, j)),
        out_shape=jax.ShapeDtypeStruct((rows_out, cols_out), BF16),
        compiler_params=_params("parallel", "parallel"),
        name="cast_w",
    )(w)


def _norm_mod_kernel(x_ref, sh_ref, sc_ref, o_ref):
    x = x_ref[...]
    y = x * lax.rsqrt(jnp.mean(x * x, axis=-1, keepdims=True) + EPS)
    o_ref[...] = (y * (1.0 + sc_ref[...]) + sh_ref[...]).astype(o_ref.dtype)


def _norm_mod(x, shift, scale, tm=512):
    M, D = x.shape
    G = shift.shape[0]
    tm = _pick(M // G, tm, 8)
    tiles_per_group = (M // G) // tm
    mod_spec = pl.BlockSpec((None, 1, D), lambda i: (i // tiles_per_group, 0, 0))
    return pl.pallas_call(
        _norm_mod_kernel,
        grid=(M // tm,),
        in_specs=[pl.BlockSpec((tm, D), lambda i: (i, 0)), mod_spec, mod_spec],
        out_specs=pl.BlockSpec((tm, D), lambda i: (i, 0)),
        out_shape=jax.ShapeDtypeStruct((M, D), BF16),
        compiler_params=_params("parallel"),
        name="norm_mod",
    )(x, shift, scale)


ATTN_PANEL = 512
LOG2E = math.log2(math.e)


def _attn_kernel(*refs, G, tq, tk, n_kv, diff, post_scale):
    if n_kv:
        q_ref, kc_ref, vc_ref, kx_ref, vx_ref, lam_ref, gain_ref, o_ref, qt_ref, m_ref, l_ref, acc_ref, s_ref = refs
    else:
        q_ref, kc_ref, vc_ref, lam_ref, gain_ref, o_ref, qt_ref, m_ref, l_ref, acc_ref = refs
    n_rows = 2 if diff else G
    width = n_rows * tq
    pw = min(ATTN_PANEL, width)
    starts = list(range(0, width, pw))
    tn = (((0,), (0,)), ((), ()))

    def scores(k, c0):
        return jnp.dot(k, qt_ref[:, c0:c0 + pw], preferred_element_type=F32)

    def update(s, v, c0, first):
        cols = slice(c0, c0 + pw)
        m_cur = jnp.max(s, axis=0, keepdims=True)
        if first:
            m_new = m_cur
            p = jnp.exp2(s - m_new)
            l_ref[:, cols] = jnp.sum(p, axis=0, keepdims=True)
            acc_ref[:, cols] = lax.dot_general(v, p.astype(BF16), tn, preferred_element_type=F32)
        else:
            m_prev = m_ref[:, cols]
            m_new = jnp.maximum(m_prev, m_cur)
            alpha = jnp.exp2(m_prev - m_new)
            p = jnp.exp2(s - m_new)
            l_ref[:, cols] = alpha * l_ref[:, cols] + jnp.sum(p, axis=0, keepdims=True)
            acc_ref[:, cols] = alpha * acc_ref[:, cols] + lax.dot_general(
                v, p.astype(BF16), tn, preferred_element_type=F32)
        m_ref[:, cols] = m_new

    q = q_ref[...].astype(F32)
    if diff:
        lane = lax.broadcasted_iota(jnp.int32, q.shape, 1)
        qt_ref[:, 0:tq] = jnp.where(lane < HEAD_DIM // 2, q, 0.0).T.astype(BF16)
        qt_ref[:, tq:2 * tq] = jnp.where(lane >= HEAD_DIM // 2, q, 0.0).T.astype(BF16)
    else:
        for g in range(G):
            qt_ref[:, g * tq:(g + 1) * tq] = q[:, g * HEAD_DIM:(g + 1) * HEAD_DIM].T.astype(BF16)

    kc = kc_ref[...]
    vc = vc_ref[...].astype(BF16)
    s = scores(kc, starts[0])
    for n, c0 in enumerate(starts):
        if n + 1 < len(starts):
            s_next = scores(kc, starts[n + 1])
        elif n_kv:
            s_next = scores(kx_ref[0:tk, :], starts[0])
        else:
            s_next = None
        update(s, vc, c0, True)
        s = s_next

    if n_kv:
        s_ref[...] = s

        def block(jb, carry):
            r0 = pl.multiple_of(jb * tk, tk)
            r1 = pl.multiple_of(jnp.minimum(jb + 1, n_kv - 1) * tk, tk)
            k = kx_ref[pl.ds(r0, tk), :]
            v = vx_ref[pl.ds(r0, tk), :].astype(BF16)
            s = s_ref[...]
            for n, c0 in enumerate(starts):
                if n + 1 < len(starts):
                    s_next = scores(k, starts[n + 1])
                else:
                    s_next = scores(kx_ref[pl.ds(r1, tk), :], starts[0])
                update(s, v, c0, False)
                s = s_next
            s_ref[...] = s
            return carry

        lax.fori_loop(0, n_kv, block, 0)

    o = acc_ref[...] * (1.0 / l_ref[...])
    if diff:
        d = o[:, 0:tq] - lam_ref[0] * o[:, tq:2 * tq]
        d = d * lax.rsqrt(jnp.mean(d * d, axis=0, keepdims=True) + EPS)
        o_ref[...] = (d * gain_ref[...] * post_scale).T.astype(o_ref.dtype)
    else:
        for g in range(G):
            o_ref[:, g * HEAD_DIM:(g + 1) * HEAD_DIM] = o[:, g * tq:(g + 1) * tq].T.astype(o_ref.dtype)


def _attention(q, kc, vc, kx, vx, *, G, v_slot, diff=False, lam=None, gain=None, post_scale=1.0, tq=2048, tk=1024):
    B, T, _ = q.shape
    L = kc.shape[1]
    H = kc.shape[2] // HEAD_DIM
    gq = 1 if diff else G
    n_rows = 2 if diff else G
    tq = _pick(T, tq, 128)
    if lam is None:
        lam = jnp.zeros((1,), F32)
    if gain is None:
        gain = jnp.ones((HEAD_DIM,), F32)
    in_specs = [pl.BlockSpec((None, tq, gq * HEAD_DIM), lambda b, h, i: (b, i, h)),
                pl.BlockSpec((None, L, HEAD_DIM), lambda b, h, i: (b, 0, h)),
                pl.BlockSpec((None, L, HEAD_DIM), lambda b, h, i: (b, 0, v_slot + h))]
    args = [q, kc, vc]
    scratch = [pltpu.VMEM((HEAD_DIM, n_rows * tq), BF16),
               pltpu.VMEM((1, n_rows * tq), F32),
               pltpu.VMEM((1, n_rows * tq), F32),
               pltpu.VMEM((HEAD_DIM, n_rows * tq), F32)]
    if kx is not None:
        S = kx.shape[1]
        tk = _pick(S, tk, 128)
        n_kv = S // tk
        in_specs += [pl.BlockSpec((None, S, HEAD_DIM), lambda b, h, i: (b, 0, h)),
                     pl.BlockSpec((None, S, HEAD_DIM), lambda b, h, i: (b, 0, v_slot + h))]
        args += [kx, vx]
        scratch += [pltpu.VMEM((tk, min(ATTN_PANEL, n_rows * tq)), F32)]
    else:
        n_kv = 0
    in_specs += [pl.BlockSpec(memory_space=pltpu.SMEM),
                 pl.BlockSpec((HEAD_DIM, 1), lambda b, h, i: (0, 0))]
    args += [lam.reshape(1).astype(F32), gain.reshape(HEAD_DIM, 1).astype(F32)]
    return pl.pallas_call(
        functools.partial(_attn_kernel, G=G, tq=tq, tk=tk, n_kv=n_kv, diff=diff, post_scale=post_scale),
        grid=(B, H, T // tq),
        in_specs=in_specs,
        out_specs=pl.BlockSpec((None, tq, gq * HEAD_DIM), lambda b, h, i: (b, i, h)),
        out_shape=jax.ShapeDtypeStruct((B, T, H * gq * HEAD_DIM), BF16),
        scratch_shapes=scratch,
        compiler_params=_params("parallel", "parallel", "arbitrary"),
        name="attn_diff" if diff else "attn_gqa",
    )(*args)


_HG_LEVELS = (32, 16, 8, 4, 2, 1)


def _hgrn_consts(reverse):
    C = HG_CHUNK
    mst = np.zeros((2 + len(_HG_LEVELS), C, C), np.float32)
    msk = np.zeros((len(_HG_LEVELS) + 1, C, C), np.float32)
    for t in range(C):
        if not reverse:
            mst[0, t, :t + 1] = 1.0
            mst[1, t, t + 1:] = 1.0
        else:
            mst[0, t, t:] = 1.0
            mst[1, t, :t] = 1.0
        msk[len(_HG_LEVELS), t, t] = 1.0
    for l, m in enumerate(_HG_LEVELS):
        for t in range(C):
            start = (t // (2 * m)) * 2 * m
            mid = start + m
            second = t >= mid
            if not reverse:
                if second:
                    mst[2 + l, t, mid:t + 1] = 1.0
                    msk[l, t, start:mid] = 1.0
                else:
                    mst[2 + l, t, t + 1:mid] = 1.0
            else:
                if not second:
                    mst[2 + l, t, t:mid] = 1.0
                    msk[l, t, mid:start + 2 * m] = 1.0
                else:
                    mst[2 + l, t, mid:t] = 1.0
    return mst.reshape(-1, C), msk


def _hgrn_kernel(*refs, n_chunks, unroll, reverse, nh, final):
    q_ref, v_ref, z_ref, llb_ref, l1p_ref, s0_ref, mst_ref, msk_ref = refs[:8]
    if final:
        of_ref, gate_ref, hog_ref, o_ref, sf_ref, st_ref = refs[8:]
    else:
        o_ref, sf_ref, st_ref = refs[8:]
    C = HG_CHUNK
    tb = pl.program_id(2)
    ntb = pl.num_programs(2)

    @pl.when(tb == 0)
    def _():
        st_ref[...] = s0_ref[...]

    mst = mst_ref[...]
    row = lax.broadcasted_iota(jnp.int32, (C, HEAD_DIM), 0)
    is_q = [((row // m) % 2) == (0 if reverse else 1) for m in _HG_LEVELS]
    nt = (((1,), (1,)), ((), ()))
    tn = (((0,), (0,)), ((), ()))

    def chunks(ci, carry):
        lanes = []
        for u in range(unroll):
            c = ci * unroll + u
            c = (n_chunks - 1 - c) if reverse else c
            rows = pl.ds(pl.multiple_of(c * C, C), C)
            lanes += [(rows, hh, slice(hh * HEAD_DIM, (hh + 1) * HEAD_DIM)) for hh in range(nh)]

        qs, vs, ks, gparts = [], [], [], []
        for rows, hh, cols in lanes:
            qr = q_ref[rows, cols]
            qs.append(qr * jax.nn.sigmoid(qr))
            vs.append(v_ref[rows, cols].astype(BF16))
            z = z_ref[rows, cols]
            t = l1p_ref[:, cols] + (jnp.minimum(z, 0.0) - jnp.log1p(jnp.exp(-jnp.abs(z))))
            llb = llb_ref[:, cols]
            g = jnp.maximum(llb, t) + jnp.log1p(jnp.exp(-jnp.abs(llb - t)))
            ks.append(1.0 - jnp.exp(g))
            g_hi = g.astype(BF16)
            gparts += [g_hi, (g - g_hi.astype(F32)).astype(BF16)]

        sums = jnp.dot(mst, jnp.concatenate(gparts, axis=1), preferred_element_type=F32)

        es, amats = [], []
        for n, (q, k) in enumerate(zip(qs, ks)):
            c0 = 2 * n * HEAD_DIM
            e = jnp.exp(sums[:, c0:c0 + HEAD_DIM] + sums[:, c0 + HEAD_DIM:c0 + 2 * HEAD_DIM])
            a = msk_ref[len(_HG_LEVELS)] * lax.dot_general(q.astype(BF16), k.astype(BF16), nt,
                                                           preferred_element_type=F32)
            for l in range(len(_HG_LEVELS)):
                x = (jnp.where(is_q[l], q, k) * e[(2 + l) * C:(3 + l) * C]).astype(BF16)
                a = a + msk_ref[l] * lax.dot_general(x, x, nt, preferred_element_type=F32)
            es.append(e)
            amats.append(a.astype(BF16))

        for (rows, hh, cols), q, v, k, e, a in zip(lanes, qs, vs, ks, es, amats):
            e_in = e[0:C]
            e_out = e[C:2 * C]
            e_tot = e_in[0:1] if reverse else e_in[C - 1:C]
            st = st_ref[hh]
            o = jnp.dot(a, v, preferred_element_type=F32)
            o = o + lax.dot_general((q * e_in).astype(BF16), st.astype(BF16), nt, preferred_element_type=F32)
            st_ref[hh] = st * e_tot + lax.dot_general(v, (k * e_out).astype(BF16), tn,
                                                     preferred_element_type=F32)
            if final:
                o = o + of_ref[rows, cols]
                o = o * lax.rsqrt(jnp.mean(o * o, axis=-1, keepdims=True) + EPS) * hog_ref[...]
                gt = gate_ref[rows, cols]
                o = o * (gt * jax.nn.sigmoid(gt))
            o_ref[rows, cols] = o.astype(o_ref.dtype)
        return carry

    lax.fori_loop(0, n_chunks // unroll, chunks, 0)

    @pl.when(tb == ntb - 1)
    def _():
        sf_ref[...] = st_ref[...]


def _hgrn_scan(p, slots, llb, l1p, s0, *, reverse, o_fwd=None, o_gain=None, tb=512):
    B, T, _ = p.shape
    W = llb.shape[1]
    H = W // HEAD_DIM
    final = o_fwd is not None
    nh = 2 if (H % 2 == 0 and all(s % 2 == 0 for s in slots)) else 1
    wb = nh * HEAD_DIM
    tb = _pick(T, tb, HG_CHUNK)
    n_chunks = tb // HG_CHUNK
    ntb = T // tb
    mst, msk = _hgrn_consts(reverse)

    def seq_spec(slot):
        s = slot // nh
        if reverse:
            return pl.BlockSpec((None, tb, wb), lambda b, h, t: (b, ntb - 1 - t, s + h))
        return pl.BlockSpec((None, tb, wb), lambda b, h, t: (b, t, s + h))

    vec_spec = pl.BlockSpec((1, wb), lambda b, h, t: (0, h))
    st_spec = pl.BlockSpec((None, nh, HEAD_DIM, HEAD_DIM), lambda b, h, t: (b, h, 0, 0))
    in_specs = [seq_spec(slots[0]), seq_spec(slots[1]), seq_spec(slots[2]), vec_spec, vec_spec, st_spec,
                pl.BlockSpec(mst.shape, lambda b, h, t: (0, 0)),
                pl.BlockSpec(msk.shape, lambda b, h, t: (0, 0, 0))]
    args = [p, p, p, llb, l1p, s0, jnp.asarray(mst, BF16), jnp.asarray(msk, F32)]
    if final:
        in_specs += [seq_spec(0), seq_spec(slots[3]), pl.BlockSpec((1, HEAD_DIM), lambda b, h, t: (0, 0))]
        args += [o_fwd, p, o_gain.reshape(1, HEAD_DIM).astype(F32)]
    return pl.pallas_call(
        functools.partial(_hgrn_kernel, n_chunks=n_chunks, unroll=math.gcd(n_chunks, HG_UNROLL), reverse=reverse,
                          nh=nh, final=final),
        grid=(B, H // nh, ntb),
        in_specs=in_specs,
        out_specs=[seq_spec(0), st_spec],
        out_shape=[jax.ShapeDtypeStruct((B, T, W), BF16 if final else F32),
                   jax.ShapeDtypeStruct((B, H, HEAD_DIM, HEAD_DIM), F32)],
        scratch_shapes=[pltpu.VMEM((nh, HEAD_DIM, HEAD_DIM), F32)],
        compiler_params=_params("parallel", "parallel", "arbitrary"),
        name="hgrn_bwd" if reverse else "hgrn_fwd",
    )(*args)


def _prep_kernel(x_ref, gain_ref, cos_ref, sin_ref, o_ref, *, ns, halves, rope, scale):
    gain = gain_ref[...]
    lane = lax.broadcasted_iota(jnp.int32, (x_ref.shape[0], HEAD_DIM), 1)
    low = lane < HEAD_DIM // 2
    even = (lane % 2) == 0
    for s in range(ns):
        cols = slice(s * HEAD_DIM, (s + 1) * HEAD_DIM)
        x = x_ref[:, cols]
        xx = x * x
        if halves:
            m0 = jnp.sum(jnp.where(low, xx, 0.0), axis=-1, keepdims=True) * (2.0 / HEAD_DIM)
            m1 = jnp.sum(jnp.where(low, 0.0, xx), axis=-1, keepdims=True) * (2.0 / HEAD_DIM)
            inv = jnp.where(low, lax.rsqrt(m0 + EPS), lax.rsqrt(m1 + EPS))
        else:
            inv = lax.rsqrt(jnp.mean(xx, axis=-1, keepdims=True) + EPS)
        y = x * inv * gain
        if rope:
            partner = jnp.where(even, pltpu.roll(y, HEAD_DIM - 1, 1), pltpu.roll(y, 1, 1))
            y = y * cos_ref[...] + partner * sin_ref[...]
        o_ref[:, cols] = (y * scale).astype(o_ref.dtype)


def _prep(p, slot, n_slots, gain, tables, *, halves, scale, seq_len, tm=1024):
    M = p.shape[0]
    ns = math.gcd(slot, n_slots)
    tm = _pick(seq_len, tm, 8)
    tiles_per_seq = seq_len // tm
    rope = tables is not None
    if rope:
        cos, sin = tables
        tab_spec = pl.BlockSpec((tm, HEAD_DIM), lambda i, j: (i % tiles_per_seq, 0))
    else:
        cos = sin = jnp.zeros((8, HEAD_DIM), F32)
        tab_spec = pl.BlockSpec((8, HEAD_DIM), lambda i, j: (0, 0))
    return pl.pallas_call(
        functools.partial(_prep_kernel, ns=ns, halves=halves, rope=rope, scale=scale),
        grid=(M // tm, n_slots // ns),
        in_specs=[pl.BlockSpec((tm, ns * HEAD_DIM), lambda i, j: (i, slot // ns + j)),
                  pl.BlockSpec((1, HEAD_DIM), lambda i, j: (0, 0)), tab_spec, tab_spec],
        out_specs=pl.BlockSpec((tm, ns * HEAD_DIM), lambda i, j: (i, j)),
        out_shape=jax.ShapeDtypeStruct((M, n_slots * HEAD_DIM), BF16),
        compiler_params=_params("parallel", "parallel"),
        name="prep_qk",
    )(p, gain.reshape(1, HEAD_DIM).astype(F32), cos, sin)


def _rope_tables(seq_len, dim):
    rows = seq_len // GRID_W
    row = jnp.repeat(jnp.arange(rows, dtype=F32), GRID_W)
    col = jnp.tile(jnp.arange(GRID_W, dtype=F32), rows)
    half = dim // 2
    inv = ROPE_THETA ** (-jnp.arange(0, half, 2, dtype=F32) / half)
    ang = jnp.concatenate([row[:, None] * inv, col[:, None] * inv], axis=-1)
    cos = jnp.repeat(jnp.cos(ang), 2, axis=-1)
    sin = jnp.repeat(jnp.sin(ang), 2, axis=-1) * jnp.tile(jnp.asarray([-1.0, 1.0], F32), half)
    reps = HEAD_DIM // dim
    return jnp.tile(cos, (1, reps)), jnp.tile(sin, (1, reps))


def _dims(D):
    n = D // HEAD_DIM
    a_heads = 3 * n // 8
    a_kv = a_heads // 3
    b_heads = (n - a_heads) // 2
    c_heads = n - a_heads - b_heads
    return a_heads, a_kv, b_heads, c_heads


def _layer(x2, y2, B, mod, rope_a, rope_c, lb, lam_init, last,
           w_in, w_out, aq_g, ak_g, ho_g, dq_g, dk_g, d_lam, do_g, w_gate, w_up, w_down):
    D = x2.shape[1]
    T = x2.shape[0] // B
    L = y2.shape[0] // B
    a_heads, a_kv, b_heads, c_heads = _dims(D)
    a_group = a_heads // a_kv
    half = HEAD_DIM // 2
    widths = (a_heads, c_heads, b_heads, b_heads, a_kv, a_kv, c_heads, c_heads, b_heads, b_heads, b_heads)
    (s_aq, s_cq, s_bq, s_bg, s_ak, s_av, s_ck, s_cv, s_bi, s_bff, s_bfb) = (
        np.concatenate([[0], np.cumsum(widths)[:-1]]).tolist())

    mx = [m[:, None, :] for m in jnp.split(mod[0:B], 6, axis=-1)]
    my = [m[:, None, :] for m in jnp.split(mod[B:B + 1], 6, axis=-1)]

    px = _mm(_norm_mod(x2, mx[0], mx[1]), w_in, F32)
    py = _mm(_norm_mod(y2, my[0], my[1]), w_in, F32)
    px3 = px.reshape(B, T, -1)
    py3 = py.reshape(B, L, -1)

    a_scale = HEAD_DIM ** -0.5 * LOG2E
    qa_x = _prep(px, s_aq, a_heads, aq_g, rope_a, halves=False, scale=a_scale, seq_len=T).reshape(B, T, -1)
    ka_x = _prep(px, s_ak, a_kv, ak_g, rope_a, halves=False, scale=1.0, seq_len=T).reshape(B, T, -1)
    ka_y = _prep(py, s_ak, a_kv, ak_g, None, halves=False, scale=1.0, seq_len=L).reshape(B, L, -1)
    oa_x = _attention(qa_x, ka_y, py3, ka_x, px3, G=a_group, v_slot=s_av)

    c_scale = half ** -0.5 * LOG2E
    dl = d_lam.astype(F32)
    lam = jnp.exp(jnp.sum(dl[0] * dl[1])) - jnp.exp(jnp.sum(dl[2] * dl[3])) + lam_init
    dq_g2 = jnp.tile(dq_g, 2)
    dk_g2 = jnp.tile(dk_g, 2)
    qc_x = _prep(px, s_cq, c_heads, dq_g2, rope_c, halves=True, scale=c_scale, seq_len=T).reshape(B, T, -1)
    kc_x = _prep(px, s_ck, c_heads, dk_g2, rope_c, halves=True, scale=1.0, seq_len=T).reshape(B, T, -1)
    kc_y = _prep(py, s_ck, c_heads, dk_g2, None, halves=True, scale=1.0, seq_len=L).reshape(B, L, -1)
    oc_x = _attention(qc_x, kc_y, py3, kc_x, px3, G=2, v_slot=s_cv, diff=True, lam=lam, gain=do_g,
                      post_scale=1.0 - lam_init)

    llb = jnp.log(lb).reshape(2, 1, -1)
    l1p = jnp.log1p(-lb).reshape(2, 1, -1)
    s0 = jnp.zeros((B, b_heads, HEAD_DIM, HEAD_DIM), F32)
    f_slots = (s_bq, s_bi, s_bff, s_bg)
    b_slots = (s_bq, s_bi, s_bfb, s_bg)
    of_y, s_f = _hgrn_scan(py3, f_slots, llb[0], l1p[0], s0, reverse=False)
    ob_y, s_b = _hgrn_scan(py3, b_slots, llb[1], l1p[1], s0, reverse=True, o_fwd=of_y, o_gain=ho_g)
    of_x, _ = _hgrn_scan(px3, f_slots, llb[0], l1p[0], s_f, reverse=False)
    ob_x, _ = _hgrn_scan(px3, b_slots, llb[1], l1p[1], s_b, reverse=True, o_fwd=of_x, o_gain=ho_g)

    x2 = _mm_res([oa_x.reshape(B * T, -1), ob_x.reshape(B * T, -1), oc_x.reshape(B * T, -1)], w_out, x2, mx[2])
    x2 = _mm_res([_mm_swiglu(_norm_mod(x2, mx[3], mx[4]), w_gate, w_up)], [w_down], x2, mx[5],
                 tm=FFN_DOWN_TILES[0], tn=FFN_DOWN_TILES[1], tk_max=w_down.shape[0])
    if last:
        return x2, None

    qa_y = _prep(py, s_aq, a_heads, aq_g, None, halves=False, scale=a_scale, seq_len=L).reshape(B, L, -1)
    oa_y = _attention(qa_y, ka_y, py3, None, None, G=a_group, v_slot=s_av)
    qc_y = _prep(py, s_cq, c_heads, dq_g2, None, halves=True, scale=c_scale, seq_len=L).reshape(B, L, -1)
    oc_y = _attention(qc_y, kc_y, py3, None, None, G=2, v_slot=s_cv, diff=True, lam=lam, gain=do_g,
                      post_scale=1.0 - lam_init)
    y2 = _mm_res([oa_y.reshape(B * L, -1), ob_y.reshape(B * L, -1), oc_y.reshape(B * L, -1)], w_out, y2, my[2])
    y2 = _mm_res([_mm_swiglu(_norm_mod(y2, my[3], my[4]), w_gate, w_up)], [w_down], y2, my[5],
                 tm=FFN_DOWN_TILES[0], tn=FFN_DOWN_TILES[1], tk_max=w_down.shape[0])
    return x2, y2


def kernel(x, c, ctx, c_ctx, w_ada, b_ada, w_in, w_out, attn_q_gain, attn_k_gain,
           hgrn_lb_logits, hgrn_o_gain, diff_q_gain, diff_k_gain, diff_lambda, diff_o_gain,
           w_gate, w_up, w_down):
    B, T, D = x.shape
    L = ctx.shape[1]
    depth = w_ada.shape[0]
    a_heads, _, b_heads, _ = _dims(D)
    rope_a = _rope_tables(T, HEAD_DIM)
    rope_c = _rope_tables(T, HEAD_DIM // 2)

    sm = jax.nn.softmax(hgrn_lb_logits.astype(F32), axis=0)
    lb = jnp.concatenate([jnp.zeros_like(sm[:1]), jnp.cumsum(sm[1:], axis=0)], axis=0)

    cs = jnp.concatenate([c, c_ctx[None, :]], axis=0)
    cs = jax.nn.silu(jnp.pad(cs, ((0, 8 - (B + 1)), (0, 0))))

    d_ff = w_gate.shape[-1]
    ff_pad = (-d_ff) % FF_PAD
    a_w = a_heads * HEAD_DIM
    b_w = b_heads * HEAD_DIM

    x2 = x.reshape(B * T, D)
    y2 = ctx.reshape(B * L, D)
    for l in range(depth):
        lam_init = 0.8 - 0.6 * math.exp(-0.3 * l)
        mod = _ada(cs, w_ada, b_ada.reshape(depth, 1, -1), l)
        wg = _cast_w(w_gate, l, 0, D, D, d_ff + ff_pad)
        wu = _cast_w(w_up, l, 0, D, D, d_ff + ff_pad)
        wd = _cast_w(w_down, l, 0, d_ff, d_ff + ff_pad, D)
        wo = [_cast_w(w_out, l, r0, n, n, D) for r0, n in ((0, a_w), (a_w, b_w), (a_w + b_w, D - a_w - b_w))]
        x2, y2 = _layer(x2, y2, B, mod, rope_a, rope_c, lb[l], lam_init, l == depth - 1,
                        _cast_w(w_in, l, 0, D, D, w_in.shape[2]), wo,
                        attn_q_gain[l], attn_k_gain[l], hgrn_o_gain[l], diff_q_gain[l], diff_k_gain[l],
                        diff_lambda[l], diff_o_gain[l], wg, wu, wd)
    return x2.reshape(B, T, D)
```

```python
import functools
import math

import numpy as np
import jax
import jax.numpy as jnp
from jax import lax
from jax.experimental import pallas as pl
from jax.experimental.pallas import tpu as pltpu

F32 = jnp.float32
BF16 = jnp.bfloat16

HEAD_DIM = 128
GRID_W = 64
HG_CHUNK = 64
HG_UNROLL = 8
HG_GROUP = 4
ROPE_THETA = 10000.0
EPS = 1e-6
FF_PAD = 256
FFN_UP_TILES = (2048, 256)
FFN_DOWN_TILES = (512, 512)
VMEM_LIMIT = 56 * 1024 * 1024


def _pick(dim, pref, align):
    t = min(pref, dim)
    t -= t % align
    while t >= align:
        if dim % t == 0:
            return t
        t -= align
    return dim


def _params(*sem):
    return pltpu.CompilerParams(dimension_semantics=sem, vmem_limit_bytes=VMEM_LIMIT)


def _mm_kernel(a_ref, w_ref, o_ref):
    o_ref[...] = jnp.dot(a_ref[...], w_ref[...], preferred_element_type=F32).astype(o_ref.dtype)


def _mm(a, w, out_dtype, tm=1024, tn=512):
    M, K = a.shape
    N = w.shape[1]
    tm = _pick(M, tm, 8)
    tn = _pick(N, tn, 128)
    return pl.pallas_call(
        _mm_kernel,
        grid=(M // tm, N // tn),
        in_specs=[pl.BlockSpec((tm, K), lambda i, j: (i, 0)),
                  pl.BlockSpec((K, tn), lambda i, j: (0, j))],
        out_specs=pl.BlockSpec((tm, tn), lambda i, j: (i, j)),
        out_shape=jax.ShapeDtypeStruct((M, N), out_dtype),
        compiler_params=_params("parallel", "arbitrary"),
        name="proj_in",
    )(a, w)


def _mm_res_kernel(*refs, nk, n_in):
    a_refs, w_refs = refs[:n_in], refs[n_in:2 * n_in]
    r_ref, g_ref, o_ref = refs[2 * n_in:2 * n_in + 3]
    d = jnp.dot(a_refs[0][...], w_refs[0][...], preferred_element_type=F32)
    for a_ref, w_ref in zip(a_refs[1:], w_refs[1:]):
        d = d + jnp.dot(a_ref[...], w_ref[...], preferred_element_type=F32)
    if nk == 1:
        o_ref[...] = r_ref[...] + g_ref[...] * d
        return
    acc_ref, = refs[2 * n_in + 3:]
    k = pl.program_id(2)

    @pl.when(k == 0)
    def _():
        acc_ref[...] = d

    @pl.when(k > 0)
    def _():
        acc_ref[...] += d

    @pl.when(k == nk - 1)
    def _():
        o_ref[...] = r_ref[...] + g_ref[...] * acc_ref[...]


def _mm_res(a_list, w_list, res, gate, tm=1024, tn=1024, tk_max=4096):
    M = a_list[0].shape[0]
    N = w_list[0].shape[1]
    G = gate.shape[0]
    n_in = len(a_list)
    tm = _pick(M // G, tm, 8)
    tn = _pick(N, tn, 128)
    K = a_list[0].shape[1]
    tk = K if (n_in > 1 or K <= tk_max) else _pick(K, tk_max, 256)
    nk = K // tk
    tiles_per_group = (M // G) // tm
    scratch = [] if nk == 1 else [pltpu.VMEM((tm, tn), F32)]
    if n_in == 1:
        a_specs = [pl.BlockSpec((tm, tk), lambda i, j, k: (i, k))]
        w_specs = [pl.BlockSpec((tk, tn), lambda i, j, k: (k, j))]
    else:
        a_specs = [pl.BlockSpec((tm, a.shape[1]), lambda i, j, k: (i, 0)) for a in a_list]
        w_specs = [pl.BlockSpec((w.shape[0], tn), lambda i, j, k: (0, j)) for w in w_list]
    return pl.pallas_call(
        functools.partial(_mm_res_kernel, nk=nk, n_in=n_in),
        grid=(M // tm, N // tn, nk),
        in_specs=a_specs + w_specs + [
            pl.BlockSpec((tm, tn), lambda i, j, k: (i, j)),
            pl.BlockSpec((None, 1, tn), lambda i, j, k: (i // tiles_per_group, 0, j))],
        out_specs=pl.BlockSpec((tm, tn), lambda i, j, k: (i, j)),
        out_shape=jax.ShapeDtypeStruct((M, N), F32),
        scratch_shapes=scratch,
        compiler_params=_params("parallel", "arbitrary", "arbitrary"),
        name="proj_res",
    )(*a_list, *w_list, res, gate)


def _mm_swiglu_kernel(a_ref, wg_ref, wu_ref, o_ref):
    a = a_ref[...]
    g = jnp.dot(a, wg_ref[...], preferred_element_type=F32)
    u = jnp.dot(a, wu_ref[...], preferred_element_type=F32)
    o_ref[...] = (g * jax.nn.sigmoid(g) * u).astype(o_ref.dtype)


def _mm_swiglu(a, wg, wu, tm=FFN_UP_TILES[0], tn=FFN_UP_TILES[1]):
    M, K = a.shape
    N = wg.shape[1]
    tm = _pick(M, tm, 8)
    tn = _pick(N, tn, 128)
    return pl.pallas_call(
        _mm_swiglu_kernel,
        grid=(M // tm, N // tn),
        in_specs=[pl.BlockSpec((tm, K), lambda i, j: (i, 0)),
                  pl.BlockSpec((K, tn), lambda i, j: (0, j)),
                  pl.BlockSpec((K, tn), lambda i, j: (0, j))],
        out_specs=pl.BlockSpec((tm, tn), lambda i, j: (i, j)),
        out_shape=jax.ShapeDtypeStruct((M, N), BF16),
        compiler_params=_params("parallel", "arbitrary"),
        name="ffn_up",
    )(a, wg, wu)


def _ada_kernel(c_ref, w_ref, b_ref, o_ref):
    o_ref[...] = jnp.dot(c_ref[...], w_ref[...], preferred_element_type=F32,
                         precision=lax.Precision.HIGHEST) + b_ref[...]


def _ada(cs, w, b, layer, tn=1024):
    R, D = cs.shape
    N = w.shape[2]
    tn = _pick(N, tn, 128)
    return pl.pallas_call(
        _ada_kernel,
        grid=(N // tn,),
        in_specs=[pl.BlockSpec((R, D), lambda j: (0, 0)),
                  pl.BlockSpec((None, D, tn), lambda j: (layer, 0, j)),
                  pl.BlockSpec((None, 1, tn), lambda j: (layer, 0, j))],
        out_specs=pl.BlockSpec((R, tn), lambda j: (0, j)),
        out_shape=jax.ShapeDtypeStruct((R, N), F32),
        compiler_params=_params("arbitrary"),
        name="adaln",
    )(cs, w, b)


def _cast_kernel(w_ref, o_ref, *, rows_valid, cols_valid):
    tr, tc = o_ref.shape
    r = pl.program_id(0) * tr + lax.broadcasted_iota(jnp.int32, (tr, tc), 0)
    c = pl.program_id(1) * tc + lax.broadcasted_iota(jnp.int32, (tr, tc), 1)
    o_ref[...] = jnp.where((r < rows_valid) & (c < cols_valid), w_ref[...], 0.0).astype(o_ref.dtype)


def _cast_w(w, layer, row0, rows, rows_out, cols_out, tc=2048):
    cols = w.shape[2]
    tr = math.gcd(math.gcd(row0, rows_out), 2048)
    tc = _pick(cols_out, tc, 128)
    assert (rows_out - tr) < rows and (cols_out - tc) < cols and row0 + rows <= w.shape[1]
    return pl.pallas_call(
        functools.partial(_cast_kernel, rows_valid=rows, cols_valid=cols),
        grid=(rows_out // tr, cols_out // tc),
        in_specs=[pl.BlockSpec((None, tr, tc), lambda i, j: (layer, row0 // tr + i, j))],
        out_specs=pl.BlockSpec((tr, tc), lambda i, j: (i, j)),
        out_shape=jax.ShapeDtypeStruct((rows_out, cols_out), BF16),
        compiler_params=_params("parallel", "parallel"),
        name="cast_w",
    )(w)


def _norm_mod_kernel(x_ref, sh_ref, sc_ref, o_ref):
    x = x_ref[...]
    y = x * lax.rsqrt(jnp.mean(x * x, axis=-1, keepdims=True) + EPS)
    o_ref[...] = (y * (1.0 + sc_ref[...]) + sh_ref[...]).astype(o_ref.dtype)


def _norm_mod(x, shift, scale, tm=512):
    M, D = x.shape
    G = shift.shape[0]
    tm = _pick(M // G, tm, 8)
    tiles_per_group = (M // G) // tm
    mod_spec = pl.BlockSpec((None, 1, D), lambda i: (i // tiles_per_group, 0, 0))
    return pl.pallas_call(
        _norm_mod_kernel,
        grid=(M // tm,),
        in_specs=[pl.BlockSpec((tm, D), lambda i: (i, 0)), mod_spec, mod_spec],
        out_specs=pl.BlockSpec((tm, D), lambda i: (i, 0)),
        out_shape=jax.ShapeDtypeStruct((M, D), BF16),
        compiler_params=_params("parallel"),
        name="norm_mod",
    )(x, shift, scale)


ATTN_PANEL = 512
ATTN_FAST_RANGE = 64.0
LOG2E = math.log2(math.e)


def _attn_kernel(*refs, G, tq, tk, n_kv, diff, post_scale):
    if n_kv:
        (q_ref, kc_ref, vc_ref, kx_ref, vx_ref, lam_ref, fast_ref, gain_ref,
         o_ref, qt_ref, m_ref, l_ref, acc_ref, s_ref) = refs
    else:
        q_ref, kc_ref, vc_ref, lam_ref, fast_ref, gain_ref, o_ref, qt_ref, m_ref, l_ref, acc_ref = refs
    n_rows = 2 if diff else G
    width = n_rows * tq
    pw = min(ATTN_PANEL, width)
    starts = list(range(0, width, pw))
    tn = (((0,), (0,)), ((), ()))

    def scores(k, c0):
        return jnp.dot(k, qt_ref[:, c0:c0 + pw], preferred_element_type=F32)

    def update(s, v, c0, mode):
        cols = slice(c0, c0 + pw)
        m_cur = jnp.max(s, axis=0, keepdims=True)
        if mode == "first":
            m_new = m_cur
            p = jnp.exp2(s - m_new)
            l_ref[:, cols] = jnp.sum(p, axis=0, keepdims=True)
            acc_ref[:, cols] = lax.dot_general(v, p.astype(BF16), tn, preferred_element_type=F32)
        elif mode == "safe":
            m_prev = m_ref[:, cols]
            m_new = jnp.maximum(m_prev, m_cur)
            alpha = jnp.exp2(m_prev - m_new)
            p = jnp.exp2(s - m_new)
            l_ref[:, cols] = alpha * l_ref[:, cols] + jnp.sum(p, axis=0, keepdims=True)
            acc_ref[:, cols] = alpha * acc_ref[:, cols] + lax.dot_general(
                v, p.astype(BF16), tn, preferred_element_type=F32)
        else:
            m_prev = m_ref[:, cols]
            p = jnp.exp2(s - m_prev)
            m_new = jnp.maximum(m_prev, m_cur)
            alpha = jnp.exp2(m_prev - m_new)
            l_ref[:, cols] = alpha * (l_ref[:, cols] + jnp.sum(p, axis=0, keepdims=True))
            acc_ref[:, cols] = alpha * (acc_ref[:, cols] + lax.dot_general(
                v, p.astype(BF16), tn, preferred_element_type=F32))
        m_ref[:, cols] = m_new

    q = q_ref[...].astype(F32)
    if diff:
        lane = lax.broadcasted_iota(jnp.int32, q.shape, 1)
        qt_ref[:, 0:tq] = jnp.where(lane < HEAD_DIM // 2, q, 0.0).T.astype(BF16)
        qt_ref[:, tq:2 * tq] = jnp.where(lane >= HEAD_DIM // 2, q, 0.0).T.astype(BF16)
    else:
        for g in range(G):
            qt_ref[:, g * tq:(g + 1) * tq] = q[:, g * HEAD_DIM:(g + 1) * HEAD_DIM].T.astype(BF16)

    kc = kc_ref[...]
    vc = vc_ref[...].astype(BF16)
    s = scores(kc, starts[0])
    for n, c0 in enumerate(starts):
        if n + 1 < len(starts):
            s_next = scores(kc, starts[n + 1])
        elif n_kv:
            s_next = scores(kx_ref[0:tk, :], starts[0])
        else:
            s_next = None
        update(s, vc, c0, "first")
        s = s_next

    if n_kv:
        s_ref[...] = s

        def block_two_pass(jb, carry):
            r0 = pl.multiple_of(jb * tk, tk)
            r1 = pl.multiple_of(jnp.minimum(jb + 1, n_kv - 1) * tk, tk)
            k = kx_ref[pl.ds(r0, tk), :]
            v = vx_ref[pl.ds(r0, tk), :].astype(BF16)
            s = s_ref[...]
            for n, c0 in enumerate(starts):
                if n + 1 < len(starts):
                    s_next = scores(k, starts[n + 1])
                else:
                    s_next = scores(kx_ref[pl.ds(r1, tk), :], starts[0])
                update(s, v, c0, "safe")
                s = s_next
            s_ref[...] = s
            return carry

        def block_one_pass(jb, carry):
            r0 = pl.multiple_of(jb * tk, tk)
            k = kx_ref[pl.ds(r0, tk), :]
            v = vx_ref[pl.ds(r0, tk), :].astype(BF16)
            for c0 in starts:
                update(scores(k, c0), v, c0, "fast")
            return carry

        @pl.when(fast_ref[0] > 0)
        def _():
            lax.fori_loop(0, n_kv, block_one_pass, 0)

        @pl.when(fast_ref[0] <= 0)
        def _():
            lax.fori_loop(0, n_kv, block_two_pass, 0)

    o = acc_ref[...] * (1.0 / l_ref[...])
    if diff:
        d = o[:, 0:tq] - lam_ref[0] * o[:, tq:2 * tq]
        d = d * lax.rsqrt(jnp.mean(d * d, axis=0, keepdims=True) + EPS)
        o_ref[...] = (d * gain_ref[...] * post_scale).T.astype(o_ref.dtype)
    else:
        for g in range(G):
            o_ref[:, g * HEAD_DIM:(g + 1) * HEAD_DIM] = o[:, g * tq:(g + 1) * tq].T.astype(o_ref.dtype)


def _attention(q, kc, vc, kx, vx, *, G, v_slot, score_bound, diff=False, lam=None, gain=None, post_scale=1.0,
               tq=2048, tk=1024):
    B, T, _ = q.shape
    L = kc.shape[1]
    H = kc.shape[2] // HEAD_DIM
    gq = 1 if diff else G
    n_rows = 2 if diff else G
    tq = _pick(T, tq, 128)
    if lam is None:
        lam = jnp.zeros((1,), F32)
    if gain is None:
        gain = jnp.ones((HEAD_DIM,), F32)
    in_specs = [pl.BlockSpec((None, tq, gq * HEAD_DIM), lambda b, h, i: (b, i, h)),
                pl.BlockSpec((None, L, HEAD_DIM), lambda b, h, i: (b, 0, h)),
                pl.BlockSpec((None, L, HEAD_DIM), lambda b, h, i: (b, 0, v_slot + h))]
    args = [q, kc, vc]
    scratch = [pltpu.VMEM((HEAD_DIM, n_rows * tq), BF16),
               pltpu.VMEM((1, n_rows * tq), F32),
               pltpu.VMEM((1, n_rows * tq), F32),
               pltpu.VMEM((HEAD_DIM, n_rows * tq), F32)]
    if kx is not None:
        S = kx.shape[1]
        tk = _pick(S, tk, 128)
        n_kv = S // tk
        in_specs += [pl.BlockSpec((None, S, HEAD_DIM), lambda b, h, i: (b, 0, h)),
                     pl.BlockSpec((None, S, HEAD_DIM), lambda b, h, i: (b, 0, v_slot + h))]
        args += [kx, vx]
        scratch += [pltpu.VMEM((tk, min(ATTN_PANEL, n_rows * tq)), F32)]
    else:
        n_kv = 0
    fast = (2.0 * score_bound <= ATTN_FAST_RANGE).astype(jnp.int32).reshape(1)
    in_specs += [pl.BlockSpec(memory_space=pltpu.SMEM), pl.BlockSpec(memory_space=pltpu.SMEM),
                 pl.BlockSpec((HEAD_DIM, 1), lambda b, h, i: (0, 0))]
    args += [lam.reshape(1).astype(F32), fast, gain.reshape(HEAD_DIM, 1).astype(F32)]
    return pl.pallas_call(
        functools.partial(_attn_kernel, G=G, tq=tq, tk=tk, n_kv=n_kv, diff=diff, post_scale=post_scale),
        grid=(B, H, T // tq),
        in_specs=in_specs,
        out_specs=pl.BlockSpec((None, tq, gq * HEAD_DIM), lambda b, h, i: (b, i, h)),
        out_shape=jax.ShapeDtypeStruct((B, T, H * gq * HEAD_DIM), BF16),
        scratch_shapes=scratch,
        compiler_params=_params("parallel", "parallel", "arbitrary"),
        name="attn_diff" if diff else "attn_gqa",
    )(*args)


_HG_LEVELS = (32, 16, 8, 4, 2, 1)


def _hgrn_consts(reverse):
    C = HG_CHUNK
    mst = np.zeros((2 + len(_HG_LEVELS), C, C), np.float32)
    msk = np.zeros((len(_HG_LEVELS) + 1, C, C), np.float32)
    for t in range(C):
        if not reverse:
            mst[0, t, :t + 1] = 1.0
            mst[1, t, t + 1:] = 1.0
        else:
            mst[0, t, t:] = 1.0
            mst[1, t, :t] = 1.0
        msk[len(_HG_LEVELS), t, t] = 1.0
    for l, m in enumerate(_HG_LEVELS):
        for t in range(C):
            start = (t // (2 * m)) * 2 * m
            mid = start + m
            second = t >= mid
            if not reverse:
                if second:
                    mst[2 + l, t, mid:t + 1] = 1.0
                    msk[l, t, start:mid] = 1.0
                else:
                    mst[2 + l, t, t + 1:mid] = 1.0
            else:
                if not second:
                    mst[2 + l, t, t:mid] = 1.0
                    msk[l, t, mid:start + 2 * m] = 1.0
                else:
                    mst[2 + l, t, mid:t] = 1.0
    mst = mst.reshape(-1, C)
    return np.concatenate([mst, mst], axis=1), msk


def _hgrn_kernel(*refs, n_chunks, unroll, reverse, nh, final):
    q_ref, v_ref, z_ref, llb_ref, l1p_ref, s0_ref, mst_ref, msk_ref = refs[:8]
    if final:
        of_ref, gate_ref, hog_ref, o_ref, sf_ref, st_ref = refs[8:]
    else:
        o_ref, sf_ref, st_ref = refs[8:]
    C = HG_CHUNK
    tb = pl.program_id(2)
    ntb = pl.num_programs(2)

    @pl.when(tb == 0)
    def _():
        st_ref[...] = s0_ref[...]

    mst = mst_ref[...]
    row = lax.broadcasted_iota(jnp.int32, (C, HEAD_DIM), 0)
    is_q = [((row // m) % 2) == (0 if reverse else 1) for m in _HG_LEVELS]
    pair = [msk_ref[l] > 0.5 for l in range(len(_HG_LEVELS) + 1)]
    nt = (((1,), (1,)), ((), ()))
    tn = (((0,), (0,)), ((), ()))

    def chunks(ci, carry):
        lanes = []
        for u in range(unroll):
            c = ci * unroll + u
            c = (n_chunks - 1 - c) if reverse else c
            rows = pl.ds(pl.multiple_of(c * C, C), C)
            lanes += [(rows, hh, slice(hh * HEAD_DIM, (hh + 1) * HEAD_DIM)) for hh in range(nh)]

        qs, vs, ks, gparts = [], [], [], []
        for rows, hh, cols in lanes:
            qr = q_ref[rows, cols]
            qs.append(qr * jax.nn.sigmoid(qr))
            vs.append(v_ref[rows, cols].astype(BF16))
            z = z_ref[rows, cols]
            t = l1p_ref[:, cols] + (jnp.minimum(z, 0.0) - jnp.log1p(jnp.exp(-jnp.abs(z))))
            llb = llb_ref[:, cols]
            g = jnp.maximum(llb, t) + jnp.log1p(jnp.exp(-jnp.abs(llb - t)))
            ks.append(1.0 - jnp.exp(g))
            g2 = g * LOG2E
            g_hi = g2.astype(BF16)
            gparts.append(jnp.concatenate([g_hi, (g2 - g_hi.astype(F32)).astype(BF16)], axis=0))

        sums = [jnp.dot(mst, jnp.concatenate(gparts[n:n + HG_GROUP], axis=1), preferred_element_type=F32)
                for n in range(0, len(gparts), HG_GROUP)]

        q_ins, o_intra, incs, e_tots = [], [], [], []
        for n, (q, v, k) in enumerate(zip(qs, vs, ks)):
            c0 = (n % HG_GROUP) * HEAD_DIM
            e = jnp.exp2(sums[n // HG_GROUP][:, c0:c0 + HEAD_DIM])
            a = jnp.where(pair[len(_HG_LEVELS)],
                          lax.dot_general(q.astype(BF16), k.astype(BF16), nt, preferred_element_type=F32), 0.0)
            for l in range(len(_HG_LEVELS)):
                x = (jnp.where(is_q[l], q, k) * e[(2 + l) * C:(3 + l) * C]).astype(BF16)
                a = jnp.where(pair[l], lax.dot_general(x, x, nt, preferred_element_type=F32), a)
            e_in = e[0:C]
            q_ins.append((q * e_in).astype(BF16))
            e_tots.append(e_in[0:1] if reverse else e_in[C - 1:C])
            o_intra.append(jnp.dot(a.astype(BF16), v, preferred_element_type=F32))
            incs.append(lax.dot_general(v, (k * e[C:2 * C]).astype(BF16), tn, preferred_element_type=F32))

        state = [st_ref[hh] for hh in range(nh)]
        st_in = []
        for (rows, hh, cols), e_tot, inc in zip(lanes, e_tots, incs):
            st_in.append(state[hh].astype(BF16))
            state[hh] = state[hh] * e_tot + inc
        for hh in range(nh):
            st_ref[hh] = state[hh]

        for (rows, hh, cols), q_in, st, o in zip(lanes, q_ins, st_in, o_intra):
            o = o + lax.dot_general(q_in, st, nt, preferred_element_type=F32)
            if final:
                o = o + of_ref[rows, cols]
                o = o * lax.rsqrt(jnp.mean(o * o, axis=-1, keepdims=True) + EPS) * hog_ref[...]
                gt = gate_ref[rows, cols]
                o = o * (gt * jax.nn.sigmoid(gt))
            o_ref[rows, cols] = o.astype(o_ref.dtype)
        return carry

    lax.fori_loop(0, n_chunks // unroll, chunks, 0)

    @pl.when(tb == ntb - 1)
    def _():
        sf_ref[...] = st_ref[...]


def _hgrn_scan(p, slots, llb, l1p, s0, *, reverse, o_fwd=None, o_gain=None, tb=512):
    B, T, _ = p.shape
    W = llb.shape[1]
    H = W // HEAD_DIM
    final = o_fwd is not None
    nh = 2 if (H % 2 == 0 and all(s % 2 == 0 for s in slots)) else 1
    wb = nh * HEAD_DIM
    tb = _pick(T, tb, HG_CHUNK)
    n_chunks = tb // HG_CHUNK
    ntb = T // tb
    mst, msk = _hgrn_consts(reverse)

    def seq_spec(slot):
        s = slot // nh
        if reverse:
            return pl.BlockSpec((None, tb, wb), lambda b, h, t: (b, ntb - 1 - t, s + h))
        return pl.BlockSpec((None, tb, wb), lambda b, h, t: (b, t, s + h))

    vec_spec = pl.BlockSpec((1, wb), lambda b, h, t: (0, h))
    st_spec = pl.BlockSpec((None, nh, HEAD_DIM, HEAD_DIM), lambda b, h, t: (b, h, 0, 0))
    in_specs = [seq_spec(slots[0]), seq_spec(slots[1]), seq_spec(slots[2]), vec_spec, vec_spec, st_spec,
                pl.BlockSpec(mst.shape, lambda b, h, t: (0, 0)),
                pl.BlockSpec(msk.shape, lambda b, h, t: (0, 0, 0))]
    args = [p, p, p, llb, l1p, s0, jnp.asarray(mst, BF16), jnp.asarray(msk, F32)]
    if final:
        in_specs += [seq_spec(0), seq_spec(slots[3]), pl.BlockSpec((1, HEAD_DIM), lambda b, h, t: (0, 0))]
        args += [o_fwd, p, o_gain.reshape(1, HEAD_DIM).astype(F32)]
    return pl.pallas_call(
        functools.partial(_hgrn_kernel, n_chunks=n_chunks, unroll=math.gcd(n_chunks, HG_UNROLL), reverse=reverse,
                          nh=nh, final=final),
        grid=(B, H // nh, ntb),
        in_specs=in_specs,
        out_specs=[seq_spec(0), st_spec],
        out_shape=[jax.ShapeDtypeStruct((B, T, W), BF16 if final else F32),
                   jax.ShapeDtypeStruct((B, H, HEAD_DIM, HEAD_DIM), F32)],
        scratch_shapes=[pltpu.VMEM((nh, HEAD_DIM, HEAD_DIM), F32)],
        compiler_params=_params("parallel", "parallel", "arbitrary"),
        name="hgrn_bwd" if reverse else "hgrn_fwd",
    )(*args)


def _prep_kernel(x_ref, gain_ref, cos_ref, sin_ref, o_ref, *, ns, halves, rope, scale):
    gain = gain_ref[...]
    lane = lax.broadcasted_iota(jnp.int32, (x_ref.shape[0], HEAD_DIM), 1)
    low = lane < HEAD_DIM // 2
    even = (lane % 2) == 0
    for s in range(ns):
        cols = slice(s * HEAD_DIM, (s + 1) * HEAD_DIM)
        x = x_ref[:, cols]
        xx = x * x
        if halves:
            m0 = jnp.sum(jnp.where(low, xx, 0.0), axis=-1, keepdims=True) * (2.0 / HEAD_DIM)
            m1 = jnp.sum(jnp.where(low, 0.0, xx), axis=-1, keepdims=True) * (2.0 / HEAD_DIM)
            inv = jnp.where(low, lax.rsqrt(m0 + EPS), lax.rsqrt(m1 + EPS))
        else:
            inv = lax.rsqrt(jnp.mean(xx, axis=-1, keepdims=True) + EPS)
        y = x * inv * gain
        if rope:
            partner = jnp.where(even, pltpu.roll(y, HEAD_DIM - 1, 1), pltpu.roll(y, 1, 1))
            y = y * cos_ref[...] + partner * sin_ref[...]
        o_ref[:, cols] = (y * scale).astype(o_ref.dtype)


def _prep(p, slot, n_slots, gain, tables, *, halves, scale, seq_len, tm=1024):
    M = p.shape[0]
    ns = math.gcd(slot, n_slots)
    tm = _pick(seq_len, tm, 8)
    tiles_per_seq = seq_len // tm
    rope = tables is not None
    if rope:
        cos, sin = tables
        tab_spec = pl.BlockSpec((tm, HEAD_DIM), lambda i, j: (i % tiles_per_seq, 0))
    else:
        cos = sin = jnp.zeros((8, HEAD_DIM), F32)
        tab_spec = pl.BlockSpec((8, HEAD_DIM), lambda i, j: (0, 0))
    return pl.pallas_call(
        functools.partial(_prep_kernel, ns=ns, halves=halves, rope=rope, scale=scale),
        grid=(M // tm, n_slots // ns),
        in_specs=[pl.BlockSpec((tm, ns * HEAD_DIM), lambda i, j: (i, slot // ns + j)),
                  pl.BlockSpec((1, HEAD_DIM), lambda i, j: (0, 0)), tab_spec, tab_spec],
        out_specs=pl.BlockSpec((tm, ns * HEAD_DIM), lambda i, j: (i, j)),
        out_shape=jax.ShapeDtypeStruct((M, n_slots * HEAD_DIM), BF16),
        compiler_params=_params("parallel", "parallel"),
        name="prep_qk",
    )(p, gain.reshape(1, HEAD_DIM).astype(F32), cos, sin)


def _rope_tables(seq_len, dim):
    rows = seq_len // GRID_W
    row = jnp.repeat(jnp.arange(rows, dtype=F32), GRID_W)
    col = jnp.tile(jnp.arange(GRID_W, dtype=F32), rows)
    half = dim // 2
    inv = ROPE_THETA ** (-jnp.arange(0, half, 2, dtype=F32) / half)
    ang = jnp.concatenate([row[:, None] * inv, col[:, None] * inv], axis=-1)
    cos = jnp.repeat(jnp.cos(ang), 2, axis=-1)
    sin = jnp.repeat(jnp.sin(ang), 2, axis=-1) * jnp.tile(jnp.asarray([-1.0, 1.0], F32), half)
    reps = HEAD_DIM // dim
    return jnp.tile(cos, (1, reps)), jnp.tile(sin, (1, reps))


def _dims(D):
    n = D // HEAD_DIM
    a_heads = 3 * n // 8
    a_kv = a_heads // 3
    b_heads = (n - a_heads) // 2
    c_heads = n - a_heads - b_heads
    return a_heads, a_kv, b_heads, c_heads


def _layer(x2, y2, B, mod, rope_a, rope_c, lb, lam_init, last,
           w_in, w_out, aq_g, ak_g, ho_g, dq_g, dk_g, d_lam, do_g, w_gate, w_up, w_down):
    D = x2.shape[1]
    T = x2.shape[0] // B
    L = y2.shape[0] // B
    a_heads, a_kv, b_heads, c_heads = _dims(D)
    a_group = a_heads // a_kv
    half = HEAD_DIM // 2
    widths = (a_heads, c_heads, b_heads, b_heads, a_kv, a_kv, c_heads, c_heads, b_heads, b_heads, b_heads)
    (s_aq, s_cq, s_bq, s_bg, s_ak, s_av, s_ck, s_cv, s_bi, s_bff, s_bfb) = (
        np.concatenate([[0], np.cumsum(widths)[:-1]]).tolist())

    mx = [m[:, None, :] for m in jnp.split(mod[0:B], 6, axis=-1)]
    my = [m[:, None, :] for m in jnp.split(mod[B:B + 1], 6, axis=-1)]

    px = _mm(_norm_mod(x2, mx[0], mx[1]), w_in, F32)
    py = _mm(_norm_mod(y2, my[0], my[1]), w_in, F32)
    px3 = px.reshape(B, T, -1)
    py3 = py.reshape(B, L, -1)

    a_scale = HEAD_DIM ** -0.5 * LOG2E
    qa_x = _prep(px, s_aq, a_heads, aq_g, rope_a, halves=False, scale=a_scale, seq_len=T).reshape(B, T, -1)
    ka_x = _prep(px, s_ak, a_kv, ak_g, rope_a, halves=False, scale=1.0, seq_len=T).reshape(B, T, -1)
    ka_y = _prep(py, s_ak, a_kv, ak_g, None, halves=False, scale=1.0, seq_len=L).reshape(B, L, -1)
    a_bound = 1.02 * HEAD_DIM * a_scale * jnp.max(jnp.abs(aq_g)) * jnp.max(jnp.abs(ak_g))
    oa_x = _attention(qa_x, ka_y, py3, ka_x, px3, G=a_group, v_slot=s_av, score_bound=a_bound)

    c_scale = half ** -0.5 * LOG2E
    dl = d_lam.astype(F32)
    lam = jnp.exp(jnp.sum(dl[0] * dl[1])) - jnp.exp(jnp.sum(dl[2] * dl[3])) + lam_init
    dq_g2 = jnp.tile(dq_g, 2)
    dk_g2 = jnp.tile(dk_g, 2)
    qc_x = _prep(px, s_cq, c_heads, dq_g2, rope_c, halves=True, scale=c_scale, seq_len=T).reshape(B, T, -1)
    kc_x = _prep(px, s_ck, c_heads, dk_g2, rope_c, halves=True, scale=1.0, seq_len=T).reshape(B, T, -1)
    kc_y = _prep(py, s_ck, c_heads, dk_g2, None, halves=True, scale=1.0, seq_len=L).reshape(B, L, -1)
    c_bound = 1.02 * half * c_scale * jnp.max(jnp.abs(dq_g)) * jnp.max(jnp.abs(dk_g))
    oc_x = _attention(qc_x, kc_y, py3, kc_x, px3, G=2, v_slot=s_cv, score_bound=c_bound, diff=True, lam=lam,
                      gain=do_g, post_scale=1.0 - lam_init)

    llb = jnp.log(lb).reshape(2, 1, -1)
    l1p = jnp.log1p(-lb).reshape(2, 1, -1)
    s0 = jnp.zeros((B, b_heads, HEAD_DIM, HEAD_DIM), F32)
    f_slots = (s_bq, s_bi, s_bff, s_bg)
    b_slots = (s_bq, s_bi, s_bfb, s_bg)
    of_y, s_f = _hgrn_scan(py3, f_slots, llb[0], l1p[0], s0, reverse=False)
    ob_y, s_b = _hgrn_scan(py3, b_slots, llb[1], l1p[1], s0, reverse=True, o_fwd=of_y, o_gain=ho_g)
    of_x, _ = _hgrn_scan(px3, f_slots, llb[0], l1p[0], s_f, reverse=False)
    ob_x, _ = _hgrn_scan(px3, b_slots, llb[1], l1p[1], s_b, reverse=True, o_fwd=of_x, o_gain=ho_g)

    x2 = _mm_res([oa_x.reshape(B * T, -1), ob_x.reshape(B * T, -1), oc_x.reshape(B * T, -1)], w_out, x2, mx[2])
    x2 = _mm_res([_mm_swiglu(_norm_mod(x2, mx[3], mx[4]), w_gate, w_up)], [w_down], x2, mx[5],
                 tm=FFN_DOWN_TILES[0], tn=FFN_DOWN_TILES[1], tk_max=w_down.shape[0])
    if last:
        return x2, None

    qa_y = _prep(py, s_aq, a_heads, aq_g, None, halves=False, scale=a_scale, seq_len=L).reshape(B, L, -1)
    oa_y = _attention(qa_y, ka_y, py3, None, None, G=a_group, v_slot=s_av, score_bound=a_bound)
    qc_y = _prep(py, s_cq, c_heads, dq_g2, None, halves=True, scale=c_scale, seq_len=L).reshape(B, L, -1)
    oc_y = _attention(qc_y, kc_y, py3, None, None, G=2, v_slot=s_cv, score_bound=c_bound, diff=True, lam=lam,
                      gain=do_g, post_scale=1.0 - lam_init)
    y2 = _mm_res([oa_y.reshape(B * L, -1), ob_y.reshape(B * L, -1), oc_y.reshape(B * L, -1)], w_out, y2, my[2])
    y2 = _mm_res([_mm_swiglu(_norm_mod(y2, my[3], my[4]), w_gate, w_up)], [w_down], y2, my[5],
                 tm=FFN_DOWN_TILES[0], tn=FFN_DOWN_TILES[1], tk_max=w_down.shape[0])
    return x2, y2


def kernel(x, c, ctx, c_ctx, w_ada, b_ada, w_in, w_out, attn_q_gain, attn_k_gain,
           hgrn_lb_logits, hgrn_o_gain, diff_q_gain, diff_k_gain, diff_lambda, diff_o_gain,
           w_gate, w_up, w_down):
    B, T, D = x.shape
    L = ctx.shape[1]
    depth = w_ada.shape[0]
    a_heads, _, b_heads, _ = _dims(D)
    rope_a = _rope_tables(T, HEAD_DIM)
    rope_c = _rope_tables(T, HEAD_DIM // 2)

    sm = jax.nn.softmax(hgrn_lb_logits.astype(F32), axis=0)
    lb = jnp.concatenate([jnp.zeros_like(sm[:1]), jnp.cumsum(sm[1:], axis=0)], axis=0)

    cs = jnp.concatenate([c, c_ctx[None, :]], axis=0)
    cs = jax.nn.silu(jnp.pad(cs, ((0, 8 - (B + 1)), (0, 0))))

    d_ff = w_gate.shape[-1]
    ff_pad = (-d_ff) % FF_PAD
    a_w = a_heads * HEAD_DIM
    b_w = b_heads * HEAD_DIM

    x2 = x.reshape(B * T, D)
    y2 = ctx.reshape(B * L, D)
    for l in range(depth):
        lam_init = 0.8 - 0.6 * math.exp(-0.3 * l)
        mod = _ada(cs, w_ada, b_ada.reshape(depth, 1, -1), l)
        wg = _cast_w(w_gate, l, 0, D, D, d_ff + ff_pad)
        wu = _cast_w(w_up, l, 0, D, D, d_ff + ff_pad)
        wd = _cast_w(w_down, l, 0, d_ff, d_ff + ff_pad, D)
        wo = [_cast_w(w_out, l, r0, n, n, D) for r0, n in ((0, a_w), (a_w, b_w), (a_w + b_w, D - a_w - b_w))]
        x2, y2 = _layer(x2, y2, B, mod, rope_a, rope_c, lb[l], lam_init, l == depth - 1,
                        _cast_w(w_in, l, 0, D, D, w_in.shape[2]), wo,
                        attn_q_gain[l], attn_k_gain[l], hgrn_o_gain[l], diff_q_gain[l], diff_k_gain[l],
                        diff_lambda[l], diff_o_gain[l], wg, wu, wd)
    return x2.reshape(B, T, D)
```

```python
import functools
import math

import numpy as np
import jax
import jax.numpy as jnp
from jax import lax
from jax.experimental import pallas as pl
from jax.experimental.pallas import tpu as pltpu

F32 = jnp.float32
BF16 = jnp.bfloat16

HEAD_DIM = 128
GRID_W = 64
HG_CHUNK = 64
HG_UNROLL = 8
HG_GROUP = 4
ROPE_THETA = 10000.0
EPS = 1e-6
FF_PAD = 256
FFN_UP_TILES = (2048, 256)
FFN_DOWN_TILES = (512, 512)
VMEM_LIMIT = 56 * 1024 * 1024


def _pick(dim, pref, align):
    t = min(pref, dim)
    t -= t % align
    while t >= align:
        if dim % t == 0:
            return t
        t -= align
    return dim


def _params(*sem):
    return pltpu.CompilerParams(dimension_semantics=sem, vmem_limit_bytes=VMEM_LIMIT)


def _mm_kernel(a_ref, w_ref, o_ref):
    o_ref[...] = jnp.dot(a_ref[...], w_ref[...], preferred_element_type=F32).astype(o_ref.dtype)


def _mm(a, w, out_dtype, tm=1024, tn=512):
    M, K = a.shape
    N = w.shape[1]
    tm = _pick(M, tm, 8)
    tn = _pick(N, tn, 128)
    return pl.pallas_call(
        _mm_kernel,
        grid=(M // tm, N // tn),
        in_specs=[pl.BlockSpec((tm, K), lambda i, j: (i, 0)),
                  pl.BlockSpec((K, tn), lambda i, j: (0, j))],
        out_specs=pl.BlockSpec((tm, tn), lambda i, j: (i, j)),
        out_shape=jax.ShapeDtypeStruct((M, N), out_dtype),
        compiler_params=_params("parallel", "arbitrary"),
        name="proj_in",
    )(a, w)


def _mm_res_kernel(*refs, nk, n_in):
    a_refs, w_refs = refs[:n_in], refs[n_in:2 * n_in]
    r_ref, g_ref, o_ref = refs[2 * n_in:2 * n_in + 3]
    d = jnp.dot(a_refs[0][...], w_refs[0][...], preferred_element_type=F32)
    for a_ref, w_ref in zip(a_refs[1:], w_refs[1:]):
        d = d + jnp.dot(a_ref[...], w_ref[...], preferred_element_type=F32)
    if nk == 1:
        o_ref[...] = r_ref[...] + g_ref[...] * d
        return
    acc_ref, = refs[2 * n_in + 3:]
    k = pl.program_id(2)

    @pl.when(k == 0)
    def _():
        acc_ref[...] = d

    @pl.when(k > 0)
    def _():
        acc_ref[...] += d

    @pl.when(k == nk - 1)
    def _():
        o_ref[...] = r_ref[...] + g_ref[...] * acc_ref[...]


def _mm_res(a_list, w_list, res, gate, tm=1024, tn=1024, tk_max=4096):
    M = a_list[0].shape[0]
    N = w_list[0].shape[1]
    G = gate.shape[0]
    n_in = len(a_list)
    tm = _pick(M // G, tm, 8)
    tn = _pick(N, tn, 128)
    K = a_list[0].shape[1]
    tk = K if (n_in > 1 or K <= tk_max) else _pick(K, tk_max, 256)
    nk = K // tk
    tiles_per_group = (M // G) // tm
    scratch = [] if nk == 1 else [pltpu.VMEM((tm, tn), F32)]
    if n_in == 1:
        a_specs = [pl.BlockSpec((tm, tk), lambda i, j, k: (i, k))]
        w_specs = [pl.BlockSpec((tk, tn), lambda i, j, k: (k, j))]
    else:
        a_specs = [pl.BlockSpec((tm, a.shape[1]), lambda i, j, k: (i, 0)) for a in a_list]
        w_specs = [pl.BlockSpec((w.shape[0], tn), lambda i, j, k: (0, j)) for w in w_list]
    return pl.pallas_call(
        functools.partial(_mm_res_kernel, nk=nk, n_in=n_in),
        grid=(M // tm, N // tn, nk),
        in_specs=a_specs + w_specs + [
            pl.BlockSpec((tm, tn), lambda i, j, k: (i, j)),
            pl.BlockSpec((None, 1, tn), lambda i, j, k: (i // tiles_per_group, 0, j))],
        out_specs=pl.BlockSpec((tm, tn), lambda i, j, k: (i, j)),
        out_shape=jax.ShapeDtypeStruct((M, N), F32),
        scratch_shapes=scratch,
        compiler_params=_params("parallel", "arbitrary", "arbitrary"),
        name="proj_res",
    )(*a_list, *w_list, res, gate)


def _mm_swiglu_kernel(a_ref, wg_ref, wu_ref, o_ref):
    a = a_ref[...]
    g = jnp.dot(a, wg_ref[...], preferred_element_type=F32)
    u = jnp.dot(a, wu_ref[...], preferred_element_type=F32)
    o_ref[...] = (g * jax.nn.sigmoid(g) * u).astype(o_ref.dtype)


def _mm_swiglu(a, wg, wu, tm=FFN_UP_TILES[0], tn=FFN_UP_TILES[1]):
    M, K = a.shape
    N = wg.shape[1]
    tm = _pick(M, tm, 8)
    tn = _pick(N, tn, 128)
    return pl.pallas_call(
        _mm_swiglu_kernel,
        grid=(M // tm, N // tn),
        in_specs=[pl.BlockSpec((tm, K), lambda i, j: (i, 0)),
                  pl.BlockSpec((K, tn), lambda i, j: (0, j)),
                  pl.BlockSpec((K, tn), lambda i, j: (0, j))],
        out_specs=pl.BlockSpec((tm, tn), lambda i, j: (i, j)),
        out_shape=jax.ShapeDtypeStruct((M, N), BF16),
        compiler_params=_params("parallel", "arbitrary"),
        name="ffn_up",
    )(a, wg, wu)


def _ada_kernel(c_ref, w_ref, b_ref, o_ref):
    o_ref[...] = jnp.dot(c_ref[...], w_ref[...], preferred_element_type=F32,
                         precision=lax.Precision.HIGHEST) + b_ref[...]


def _ada(cs, w, b, layer, tn=1024):
    R, D = cs.shape
    N = w.shape[2]
    tn = _pick(N, tn, 128)
    return pl.pallas_call(
        _ada_kernel,
        grid=(N // tn,),
        in_specs=[pl.BlockSpec((R, D), lambda j: (0, 0)),
                  pl.BlockSpec((None, D, tn), lambda j: (layer, 0, j)),
                  pl.BlockSpec((None, 1, tn), lambda j: (layer, 0, j))],
        out_specs=pl.BlockSpec((R, tn), lambda j: (0, j)),
        out_shape=jax.ShapeDtypeStruct((R, N), F32),
        compiler_params=_params("arbitrary"),
        name="adaln",
    )(cs, w, b)


def _cast_kernel(w_ref, o_ref, *, rows_valid, cols_valid):
    tr, tc = o_ref.shape
    r = pl.program_id(0) * tr + lax.broadcasted_iota(jnp.int32, (tr, tc), 0)
    c = pl.program_id(1) * tc + lax.broadcasted_iota(jnp.int32, (tr, tc), 1)
    o_ref[...] = jnp.where((r < rows_valid) & (c < cols_valid), w_ref[...], 0.0).astype(o_ref.dtype)


def _cast_w(w, layer, row0, rows, rows_out, cols_out, tc=2048):
    cols = w.shape[2]
    tr = math.gcd(math.gcd(row0, rows_out), 2048)
    tc = _pick(cols_out, tc, 128)
    assert (rows_out - tr) < rows and (cols_out - tc) < cols and row0 + rows <= w.shape[1]
    return pl.pallas_call(
        functools.partial(_cast_kernel, rows_valid=rows, cols_valid=cols),
        grid=(rows_out // tr, cols_out // tc),
        in_specs=[pl.BlockSpec((None, tr, tc), lambda i, j: (layer, row0 // tr + i, j))],
        out_specs=pl.BlockSpec((tr, tc), lambda i, j: (i, j)),
        out_shape=jax.ShapeDtypeStruct((rows_out, cols_out), BF16),
        compiler_params=_params("parallel", "parallel"),
        name="cast_w",
    )(w)


def _norm_mod_kernel(x_ref, sh_ref, sc_ref, o_ref):
    x = x_ref[...]
    y = x * lax.rsqrt(jnp.mean(x * x, axis=-1, keepdims=True) + EPS)
    o_ref[...] = (y * (1.0 + sc_ref[...]) + sh_ref[...]).astype(o_ref.dtype)


def _norm_mod(x, shift, scale, tm=512):
    M, D = x.shape
    G = shift.shape[0]
    tm = _pick(M // G, tm, 8)
    tiles_per_group = (M // G) // tm
    mod_spec = pl.BlockSpec((None, 1, D), lambda i: (i // tiles_per_group, 0, 0))
    return pl.pallas_call(
        _norm_mod_kernel,
        grid=(M // tm,),
        in_specs=[pl.BlockSpec((tm, D), lambda i: (i, 0)), mod_spec, mod_spec],
        out_specs=pl.BlockSpec((tm, D), lambda i: (i, 0)),
        out_shape=jax.ShapeDtypeStruct((M, D), BF16),
        compiler_params=_params("parallel"),
        name="norm_mod",
    )(x, shift, scale)


ATTN_PANEL = 1024
ATTN_FAST_RANGE = 64.0
LOG2E = math.log2(math.e)


def _attn_kernel(*refs, G, tq, tk, n_kv, diff, post_scale):
    if n_kv:
        (q_ref, kc_ref, vc_ref, kx_ref, vx_ref, lam_ref, fast_ref, gain_ref,
         o_ref, qt_ref, m_ref, l_ref, acc_ref, s_ref) = refs
    else:
        q_ref, kc_ref, vc_ref, lam_ref, fast_ref, gain_ref, o_ref, qt_ref, m_ref, l_ref, acc_ref = refs
    n_rows = 2 if diff else G
    width = n_rows * tq
    pw = min(ATTN_PANEL, width)
    starts = list(range(0, width, pw))
    tn = (((0,), (0,)), ((), ()))

    def scores(k, c0):
        return jnp.dot(k, qt_ref[:, c0:c0 + pw], preferred_element_type=F32)

    def update(s, v, c0, mode):
        cols = slice(c0, c0 + pw)
        m_cur = jnp.max(s, axis=0, keepdims=True)
        if mode == "first":
            m_new = m_cur
            p = jnp.exp2(s - m_new)
            l_ref[:, cols] = jnp.sum(p, axis=0, keepdims=True)
            acc_ref[:, cols] = lax.dot_general(v, p.astype(BF16), tn, preferred_element_type=F32)
        elif mode == "safe":
            m_prev = m_ref[:, cols]
            m_new = jnp.maximum(m_prev, m_cur)
            alpha = jnp.exp2(m_prev - m_new)
            p = jnp.exp2(s - m_new)
            l_ref[:, cols] = alpha * l_ref[:, cols] + jnp.sum(p, axis=0, keepdims=True)
            acc_ref[:, cols] = alpha * acc_ref[:, cols] + lax.dot_general(
                v, p.astype(BF16), tn, preferred_element_type=F32)
        else:
            m_prev = m_ref[:, cols]
            p = jnp.exp2(s - m_prev)
            m_new = jnp.maximum(m_prev, m_cur)
            alpha = jnp.exp2(m_prev - m_new)
            l_ref[:, cols] = alpha * (l_ref[:, cols] + jnp.sum(p, axis=0, keepdims=True))
            acc_ref[:, cols] = alpha * (acc_ref[:, cols] + lax.dot_general(
                v, p.astype(BF16), tn, preferred_element_type=F32))
        m_ref[:, cols] = m_new

    q = q_ref[...].astype(F32)
    if diff:
        lane = lax.broadcasted_iota(jnp.int32, q.shape, 1)
        qt_ref[:, 0:tq] = jnp.where(lane < HEAD_DIM // 2, q, 0.0).T.astype(BF16)
        qt_ref[:, tq:2 * tq] = jnp.where(lane >= HEAD_DIM // 2, q, 0.0).T.astype(BF16)
    else:
        for g in range(G):
            qt_ref[:, g * tq:(g + 1) * tq] = q[:, g * HEAD_DIM:(g + 1) * HEAD_DIM].T.astype(BF16)

    kc = kc_ref[...]
    vc = vc_ref[...].astype(BF16)
    s = scores(kc, starts[0])
    for n, c0 in enumerate(starts):
        if n + 1 < len(starts):
            s_next = scores(kc, starts[n + 1])
        elif n_kv:
            s_next = scores(kx_ref[0:tk, :], starts[0])
        else:
            s_next = None
        update(s, vc, c0, "first")
        s = s_next

    if n_kv:
        s_ref[...] = s

        def block_two_pass(jb, carry):
            r0 = pl.multiple_of(jb * tk, tk)
            r1 = pl.multiple_of(jnp.minimum(jb + 1, n_kv - 1) * tk, tk)
            k = kx_ref[pl.ds(r0, tk), :]
            v = vx_ref[pl.ds(r0, tk), :].astype(BF16)
            s = s_ref[...]
            for n, c0 in enumerate(starts):
                if n + 1 < len(starts):
                    s_next = scores(k, starts[n + 1])
                else:
                    s_next = scores(kx_ref[pl.ds(r1, tk), :], starts[0])
                update(s, v, c0, "safe")
                s = s_next
            s_ref[...] = s
            return carry

        def block_one_pass(jb, carry):
            r0 = pl.multiple_of(jb * tk, tk)
            k = kx_ref[pl.ds(r0, tk), :]
            v = vx_ref[pl.ds(r0, tk), :].astype(BF16)
            for c0 in starts:
                update(scores(k, c0), v, c0, "fast")
            return carry

        @pl.when(fast_ref[0] > 0)
        def _():
            lax.fori_loop(0, n_kv, block_one_pass, 0)

        @pl.when(fast_ref[0] <= 0)
        def _():
            lax.fori_loop(0, n_kv, block_two_pass, 0)

    o = acc_ref[...] * (1.0 / l_ref[...])
    if diff:
        d = o[:, 0:tq] - lam_ref[0] * o[:, tq:2 * tq]
        d = d * lax.rsqrt(jnp.mean(d * d, axis=0, keepdims=True) + EPS)
        o_ref[...] = (d * gain_ref[...] * post_scale).T.astype(o_ref.dtype)
    else:
        for g in range(G):
            o_ref[:, g * HEAD_DIM:(g + 1) * HEAD_DIM] = o[:, g * tq:(g + 1) * tq].T.astype(o_ref.dtype)


def _attention(q, kc, vc, kx, vx, *, G, v_slot, score_bound, diff=False, lam=None, gain=None, post_scale=1.0,
               tq=2048, tk=2048):
    B, T, _ = q.shape
    L = kc.shape[1]
    H = kc.shape[2] // HEAD_DIM
    gq = 1 if diff else G
    n_rows = 2 if diff else G
    tq = _pick(T, tq, 128)
    if lam is None:
        lam = jnp.zeros((1,), F32)
    if gain is None:
        gain = jnp.ones((HEAD_DIM,), F32)
    in_specs = [pl.BlockSpec((None, tq, gq * HEAD_DIM), lambda b, h, i: (b, i, h)),
                pl.BlockSpec((None, L, HEAD_DIM), lambda b, h, i: (b, 0, h)),
                pl.BlockSpec((None, L, HEAD_DIM), lambda b, h, i: (b, 0, v_slot + h))]
    args = [q, kc, vc]
    scratch = [pltpu.VMEM((HEAD_DIM, n_rows * tq), BF16),
               pltpu.VMEM((1, n_rows * tq), F32),
               pltpu.VMEM((1, n_rows * tq), F32),
               pltpu.VMEM((HEAD_DIM, n_rows * tq), F32)]
    if kx is not None:
        S = kx.shape[1]
        tk = _pick(S, tk, 128)
        n_kv = S // tk
        in_specs += [pl.BlockSpec((None, S, HEAD_DIM), lambda b, h, i: (b, 0, h)),
                     pl.BlockSpec((None, S, HEAD_DIM), lambda b, h, i: (b, 0, v_slot + h))]
        args += [kx, vx]
        scratch += [pltpu.VMEM((tk, min(ATTN_PANEL, n_rows * tq)), F32)]
    else:
        n_kv = 0
    fast = (2.0 * score_bound <= ATTN_FAST_RANGE).astype(jnp.int32).reshape(1)
    in_specs += [pl.BlockSpec(memory_space=pltpu.SMEM), pl.BlockSpec(memory_space=pltpu.SMEM),
                 pl.BlockSpec((HEAD_DIM, 1), lambda b, h, i: (0, 0))]
    args += [lam.reshape(1).astype(F32), fast, gain.reshape(HEAD_DIM, 1).astype(F32)]
    return pl.pallas_call(
        functools.partial(_attn_kernel, G=G, tq=tq, tk=tk, n_kv=n_kv, diff=diff, post_scale=post_scale),
        grid=(B, H, T // tq),
        in_specs=in_specs,
        out_specs=pl.BlockSpec((None, tq, gq * HEAD_DIM), lambda b, h, i: (b, i, h)),
        out_shape=jax.ShapeDtypeStruct((B, T, H * gq * HEAD_DIM), BF16),
        scratch_shapes=scratch,
        compiler_params=_params("parallel", "parallel", "arbitrary"),
        name="attn_diff" if diff else "attn_gqa",
    )(*args)


_HG_LEVELS = (32, 16, 8, 4, 2, 1)


def _hgrn_consts(reverse):
    C = HG_CHUNK
    mst = np.zeros((2 + len(_HG_LEVELS), C, C), np.float32)
    msk = np.zeros((len(_HG_LEVELS) + 1, C, C), np.float32)
    for t in range(C):
        if not reverse:
            mst[0, t, :t + 1] = 1.0
            mst[1, t, t + 1:] = 1.0
        else:
            mst[0, t, t:] = 1.0
            mst[1, t, :t] = 1.0
        msk[len(_HG_LEVELS), t, t] = 1.0
    for l, m in enumerate(_HG_LEVELS):
        for t in range(C):
            start = (t // (2 * m)) * 2 * m
            mid = start + m
            second = t >= mid
            if not reverse:
                if second:
                    mst[2 + l, t, mid:t + 1] = 1.0
                    msk[l, t, start:mid] = 1.0
                else:
                    mst[2 + l, t, t + 1:mid] = 1.0
            else:
                if not second:
                    mst[2 + l, t, t:mid] = 1.0
                    msk[l, t, mid:start + 2 * m] = 1.0
                else:
                    mst[2 + l, t, mid:t] = 1.0
    mst = mst.reshape(-1, C)
    return np.concatenate([mst, mst], axis=1), msk


def _hgrn_kernel(*refs, n_chunks, unroll, reverse, nh, final):
    q_ref, v_ref, z_ref, llb_ref, l1p_ref, s0_ref, mst_ref, msk_ref = refs[:8]
    if final:
        of_ref, gate_ref, hog_ref, o_ref, sf_ref, st_ref = refs[8:]
    else:
        o_ref, sf_ref, st_ref = refs[8:]
    C = HG_CHUNK
    tb = pl.program_id(2)
    ntb = pl.num_programs(2)

    @pl.when(tb == 0)
    def _():
        st_ref[...] = s0_ref[...]

    mst = mst_ref[...]
    row = lax.broadcasted_iota(jnp.int32, (C, HEAD_DIM), 0)
    is_q = [((row // m) % 2) == (0 if reverse else 1) for m in _HG_LEVELS]
    pair = [msk_ref[l] > 0.5 for l in range(len(_HG_LEVELS) + 1)]
    nt = (((1,), (1,)), ((), ()))
    tn = (((0,), (0,)), ((), ()))

    def chunks(ci, carry):
        lanes = []
        for u in range(unroll):
            c = ci * unroll + u
            c = (n_chunks - 1 - c) if reverse else c
            rows = pl.ds(pl.multiple_of(c * C, C), C)
            lanes += [(rows, hh, slice(hh * HEAD_DIM, (hh + 1) * HEAD_DIM)) for hh in range(nh)]

        qs, vs, ks, gparts = [], [], [], []
        for rows, hh, cols in lanes:
            qr = q_ref[rows, cols]
            qs.append(qr * jax.nn.sigmoid(qr))
            vs.append(v_ref[rows, cols].astype(BF16))
            z = z_ref[rows, cols]
            t = l1p_ref[:, cols] + (jnp.minimum(z, 0.0) - jnp.log1p(jnp.exp(-jnp.abs(z))))
            llb = llb_ref[:, cols]
            g = jnp.maximum(llb, t) + jnp.log1p(jnp.exp(-jnp.abs(llb - t)))
            ks.append(1.0 - jnp.exp(g))
            g2 = g * LOG2E
            g_hi = g2.astype(BF16)
            gparts.append(jnp.concatenate([g_hi, (g2 - g_hi.astype(F32)).astype(BF16)], axis=0))

        sums = [jnp.dot(mst, jnp.concatenate(gparts[n:n + HG_GROUP], axis=1), preferred_element_type=F32)
                for n in range(0, len(gparts), HG_GROUP)]

        q_ins, o_intra, incs, e_tots = [], [], [], []
        for n, (q, v, k) in enumerate(zip(qs, vs, ks)):
            c0 = (n % HG_GROUP) * HEAD_DIM
            e = jnp.exp2(sums[n // HG_GROUP][:, c0:c0 + HEAD_DIM])
            a = jnp.where(pair[len(_HG_LEVELS)],
                          lax.dot_general(q.astype(BF16), k.astype(BF16), nt, preferred_element_type=F32), 0.0)
            for l in range(len(_HG_LEVELS)):
                x = (jnp.where(is_q[l], q, k) * e[(2 + l) * C:(3 + l) * C]).astype(BF16)
                a = jnp.where(pair[l], lax.dot_general(x, x, nt, preferred_element_type=F32), a)
            e_in = e[0:C]
            q_ins.append((q * e_in).astype(BF16))
            e_tots.append(e_in[0:1] if reverse else e_in[C - 1:C])
            o_intra.append(jnp.dot(a.astype(BF16), v, preferred_element_type=F32))
            incs.append(lax.dot_general(v, (k * e[C:2 * C]).astype(BF16), tn, preferred_element_type=F32))

        state = [st_ref[hh] for hh in range(nh)]
        st_in = []
        for (rows, hh, cols), e_tot, inc in zip(lanes, e_tots, incs):
            st_in.append(state[hh].astype(BF16))
            state[hh] = state[hh] * e_tot + inc
        for hh in range(nh):
            st_ref[hh] = state[hh]

        for (rows, hh, cols), q_in, st, o in zip(lanes, q_ins, st_in, o_intra):
            o = o + lax.dot_general(q_in, st, nt, preferred_element_type=F32)
            if final:
                o = o + of_ref[rows, cols]
                o = o * lax.rsqrt(jnp.mean(o * o, axis=-1, keepdims=True) + EPS) * hog_ref[...]
                gt = gate_ref[rows, cols]
                o = o * (gt * jax.nn.sigmoid(gt))
            o_ref[rows, cols] = o.astype(o_ref.dtype)
        return carry

    lax.fori_loop(0, n_chunks // unroll, chunks, 0)

    @pl.when(tb == ntb - 1)
    def _():
        sf_ref[...] = st_ref[...]


def _hgrn_scan(p, slots, llb, l1p, s0, *, reverse, o_fwd=None, o_gain=None, tb=1024):
    B, T, _ = p.shape
    W = llb.shape[1]
    H = W // HEAD_DIM
    final = o_fwd is not None
    nh = 2 if (H % 2 == 0 and all(s % 2 == 0 for s in slots)) else 1
    wb = nh * HEAD_DIM
    tb = _pick(T, tb, HG_CHUNK)
    n_chunks = tb // HG_CHUNK
    ntb = T // tb
    mst, msk = _hgrn_consts(reverse)

    def seq_spec(slot):
        s = slot // nh
        if reverse:
            return pl.BlockSpec((None, tb, wb), lambda b, h, t: (b, ntb - 1 - t, s + h))
        return pl.BlockSpec((None, tb, wb), lambda b, h, t: (b, t, s + h))

    vec_spec = pl.BlockSpec((1, wb), lambda b, h, t: (0, h))
    st_spec = pl.BlockSpec((None, nh, HEAD_DIM, HEAD_DIM), lambda b, h, t: (b, h, 0, 0))
    in_specs = [seq_spec(slots[0]), seq_spec(slots[1]), seq_spec(slots[2]), vec_spec, vec_spec, st_spec,
                pl.BlockSpec(mst.shape, lambda b, h, t: (0, 0)),
                pl.BlockSpec(msk.shape, lambda b, h, t: (0, 0, 0))]
    args = [p, p, p, llb, l1p, s0, jnp.asarray(mst, BF16), jnp.asarray(msk, F32)]
    if final:
        in_specs += [seq_spec(0), seq_spec(slots[3]), pl.BlockSpec((1, HEAD_DIM), lambda b, h, t: (0, 0))]
        args += [o_fwd, p, o_gain.reshape(1, HEAD_DIM).astype(F32)]
    return pl.pallas_call(
        functools.partial(_hgrn_kernel, n_chunks=n_chunks, unroll=math.gcd(n_chunks, HG_UNROLL), reverse=reverse,
                          nh=nh, final=final),
        grid=(B, H // nh, ntb),
        in_specs=in_specs,
        out_specs=[seq_spec(0), st_spec],
        out_shape=[jax.ShapeDtypeStruct((B, T, W), BF16 if final else F32),
                   jax.ShapeDtypeStruct((B, H, HEAD_DIM, HEAD_DIM), F32)],
        scratch_shapes=[pltpu.VMEM((nh, HEAD_DIM, HEAD_DIM), F32)],
        compiler_params=_params("parallel", "parallel", "arbitrary"),
        name="hgrn_bwd" if reverse else "hgrn_fwd",
    )(*args)


def _prep_kernel(x_ref, gain_ref, cos_ref, sin_ref, o_ref, *, ns, halves, rope, scale):
    gain = gain_ref[...]
    lane = lax.broadcasted_iota(jnp.int32, (x_ref.shape[0], HEAD_DIM), 1)
    low = lane < HEAD_DIM // 2
    even = (lane % 2) == 0
    for s in range(ns):
        cols = slice(s * HEAD_DIM, (s + 1) * HEAD_DIM)
        x = x_ref[:, cols]
        xx = x * x
        if halves:
            m0 = jnp.sum(jnp.where(low, xx, 0.0), axis=-1, keepdims=True) * (2.0 / HEAD_DIM)
            m1 = jnp.sum(jnp.where(low, 0.0, xx), axis=-1, keepdims=True) * (2.0 / HEAD_DIM)
            inv = jnp.where(low, lax.rsqrt(m0 + EPS), lax.rsqrt(m1 + EPS))
        else:
            inv = lax.rsqrt(jnp.mean(xx, axis=-1, keepdims=True) + EPS)
        y = x * inv * gain
        if rope:
            partner = jnp.where(even, pltpu.roll(y, HEAD_DIM - 1, 1), pltpu.roll(y, 1, 1))
            y = y * cos_ref[...] + partner * sin_ref[...]
        o_ref[:, cols] = (y * scale).astype(o_ref.dtype)


def _prep(p, slot, n_slots, gain, tables, *, halves, scale, seq_len, tm=1024):
    M = p.shape[0]
    ns = math.gcd(slot, n_slots)
    tm = _pick(seq_len, tm, 8)
    tiles_per_seq = seq_len // tm
    rope = tables is not None
    if rope:
        cos, sin = tables
        tab_spec = pl.BlockSpec((tm, HEAD_DIM), lambda i, j: (i % tiles_per_seq, 0))
    else:
        cos = sin = jnp.zeros((8, HEAD_DIM), F32)
        tab_spec = pl.BlockSpec((8, HEAD_DIM), lambda i, j: (0, 0))
    return pl.pallas_call(
        functools.partial(_prep_kernel, ns=ns, halves=halves, rope=rope, scale=scale),
        grid=(M // tm, n_slots // ns),
        in_specs=[pl.BlockSpec((tm, ns * HEAD_DIM), lambda i, j: (i, slot // ns + j)),
                  pl.BlockSpec((1, HEAD_DIM), lambda i, j: (0, 0)), tab_spec, tab_spec],
        out_specs=pl.BlockSpec((tm, ns * HEAD_DIM), lambda i, j: (i, j)),
        out_shape=jax.ShapeDtypeStruct((M, n_slots * HEAD_DIM), BF16),
        compiler_params=_params("parallel", "parallel"),
        name="prep_qk",
    )(p, gain.reshape(1, HEAD_DIM).astype(F32), cos, sin)


def _rope_tables(seq_len, dim):
    rows = seq_len // GRID_W
    row = jnp.repeat(jnp.arange(rows, dtype=F32), GRID_W)
    col = jnp.tile(jnp.arange(GRID_W, dtype=F32), rows)
    half = dim // 2
    inv = ROPE_THETA ** (-jnp.arange(0, half, 2, dtype=F32) / half)
    ang = jnp.concatenate([row[:, None] * inv, col[:, None] * inv], axis=-1)
    cos = jnp.repeat(jnp.cos(ang), 2, axis=-1)
    sin = jnp.repeat(jnp.sin(ang), 2, axis=-1) * jnp.tile(jnp.asarray([-1.0, 1.0], F32), half)
    reps = HEAD_DIM // dim
    return jnp.tile(cos, (1, reps)), jnp.tile(sin, (1, reps))


def _dims(D):
    n = D // HEAD_DIM
    a_heads = 3 * n // 8
    a_kv = a_heads // 3
    b_heads = (n - a_heads) // 2
    c_heads = n - a_heads - b_heads
    return a_heads, a_kv, b_heads, c_heads


def _layer(x2, y2, B, mod, rope_a, rope_c, lb, lam_init, last,
           w_in, w_out, aq_g, ak_g, ho_g, dq_g, dk_g, d_lam, do_g, w_gate, w_up, w_down):
    D = x2.shape[1]
    T = x2.shape[0] // B
    L = y2.shape[0] // B
    a_heads, a_kv, b_heads, c_heads = _dims(D)
    a_group = a_heads // a_kv
    half = HEAD_DIM // 2
    widths = (a_heads, c_heads, b_heads, b_heads, a_kv, a_kv, c_heads, c_heads, b_heads, b_heads, b_heads)
    (s_aq, s_cq, s_bq, s_bg, s_ak, s_av, s_ck, s_cv, s_bi, s_bff, s_bfb) = (
        np.concatenate([[0], np.cumsum(widths)[:-1]]).tolist())

    mx = [m[:, None, :] for m in jnp.split(mod[0:B], 6, axis=-1)]
    my = [m[:, None, :] for m in jnp.split(mod[B:B + 1], 6, axis=-1)]

    px = _mm(_norm_mod(x2, mx[0], mx[1]), w_in, F32)
    py = _mm(_norm_mod(y2, my[0], my[1]), w_in, F32)
    px3 = px.reshape(B, T, -1)
    py3 = py.reshape(B, L, -1)

    a_scale = HEAD_DIM ** -0.5 * LOG2E
    qa_x = _prep(px, s_aq, a_heads, aq_g, rope_a, halves=False, scale=a_scale, seq_len=T).reshape(B, T, -1)
    ka_x = _prep(px, s_ak, a_kv, ak_g, rope_a, halves=False, scale=1.0, seq_len=T).reshape(B, T, -1)
    ka_y = _prep(py, s_ak, a_kv, ak_g, None, halves=False, scale=1.0, seq_len=L).reshape(B, L, -1)
    a_bound = 1.02 * HEAD_DIM * a_scale * jnp.max(jnp.abs(aq_g)) * jnp.max(jnp.abs(ak_g))
    oa_x = _attention(qa_x, ka_y, py3, ka_x, px3, G=a_group, v_slot=s_av, score_bound=a_bound)

    c_scale = half ** -0.5 * LOG2E
    dl = d_lam.astype(F32)
    lam = jnp.exp(jnp.sum(dl[0] * dl[1])) - jnp.exp(jnp.sum(dl[2] * dl[3])) + lam_init
    dq_g2 = jnp.tile(dq_g, 2)
    dk_g2 = jnp.tile(dk_g, 2)
    qc_x = _prep(px, s_cq, c_heads, dq_g2, rope_c, halves=True, scale=c_scale, seq_len=T).reshape(B, T, -1)
    kc_x = _prep(px, s_ck, c_heads, dk_g2, rope_c, halves=True, scale=1.0, seq_len=T).reshape(B, T, -1)
    kc_y = _prep(py, s_ck, c_heads, dk_g2, None, halves=True, scale=1.0, seq_len=L).reshape(B, L, -1)
    c_bound = 1.02 * half * c_scale * jnp.max(jnp.abs(dq_g)) * jnp.max(jnp.abs(dk_g))
    oc_x = _attention(qc_x, kc_y, py3, kc_x, px3, G=2, v_slot=s_cv, score_bound=c_bound, diff=True, lam=lam,
                      gain=do_g, post_scale=1.0 - lam_init)

    llb = jnp.log(lb).reshape(2, 1, -1)
    l1p = jnp.log1p(-lb).reshape(2, 1, -1)
    s0 = jnp.zeros((B, b_heads, HEAD_DIM, HEAD_DIM), F32)
    f_slots = (s_bq, s_bi, s_bff, s_bg)
    b_slots = (s_bq, s_bi, s_bfb, s_bg)
    of_y, s_f = _hgrn_scan(py3, f_slots, llb[0], l1p[0], s0, reverse=False)
    ob_y, s_b = _hgrn_scan(py3, b_slots, llb[1], l1p[1], s0, reverse=True, o_fwd=of_y, o_gain=ho_g)
    of_x, _ = _hgrn_scan(px3, f_slots, llb[0], l1p[0], s_f, reverse=False)
    ob_x, _ = _hgrn_scan(px3, b_slots, llb[1], l1p[1], s_b, reverse=True, o_fwd=of_x, o_gain=ho_g)

    x2 = _mm_res([oa_x.reshape(B * T, -1), ob_x.reshape(B * T, -1), oc_x.reshape(B * T, -1)], w_out, x2, mx[2])
    x2 = _mm_res([_mm_swiglu(_norm_mod(x2, mx[3], mx[4]), w_gate, w_up)], [w_down], x2, mx[5],
                 tm=FFN_DOWN_TILES[0], tn=FFN_DOWN_TILES[1], tk_max=w_down.shape[0])
    if last:
        return x2, None

    qa_y = _prep(py, s_aq, a_heads, aq_g, None, halves=False, scale=a_scale, seq_len=L).reshape(B, L, -1)
    oa_y = _attention(qa_y, ka_y, py3, None, None, G=a_group, v_slot=s_av, score_bound=a_bound)
    qc_y = _prep(py, s_cq, c_heads, dq_g2, None, halves=True, scale=c_scale, seq_len=L).reshape(B, L, -1)
    oc_y = _attention(qc_y, kc_y, py3, None, None, G=2, v_slot=s_cv, score_bound=c_bound, diff=True, lam=lam,
                      gain=do_g, post_scale=1.0 - lam_init)
    y2 = _mm_res([oa_y.reshape(B * L, -1), ob_y.reshape(B * L, -1), oc_y.reshape(B * L, -1)], w_out, y2, my[2])
    y2 = _mm_res([_mm_swiglu(_norm_mod(y2, my[3], my[4]), w_gate, w_up)], [w_down], y2, my[5],
                 tm=FFN_DOWN_TILES[0], tn=FFN_DOWN_TILES[1], tk_max=w_down.shape[0])
    return x2, y2


def kernel(x, c, ctx, c_ctx, w_ada, b_ada, w_in, w_out, attn_q_gain, attn_k_gain,
           hgrn_lb_logits, hgrn_o_gain, diff_q_gain, diff_k_gain, diff_lambda, diff_o_gain,
           w_gate, w_up, w_down):
    B, T, D = x.shape
    L = ctx.shape[1]
    depth = w_ada.shape[0]
    a_heads, _, b_heads, _ = _dims(D)
    rope_a = _rope_tables(T, HEAD_DIM)
    rope_c = _rope_tables(T, HEAD_DIM // 2)

    sm = jax.nn.softmax(hgrn_lb_logits.astype(F32), axis=0)
    lb = jnp.concatenate([jnp.zeros_like(sm[:1]), jnp.cumsum(sm[1:], axis=0)], axis=0)

    cs = jnp.concatenate([c, c_ctx[None, :]], axis=0)
    cs = jax.nn.silu(jnp.pad(cs, ((0, 8 - (B + 1)), (0, 0))))

    d_ff = w_gate.shape[-1]
    ff_pad = (-d_ff) % FF_PAD
    a_w = a_heads * HEAD_DIM
    b_w = b_heads * HEAD_DIM

    x2 = x.reshape(B * T, D)
    y2 = ctx.reshape(B * L, D)
    for l in range(depth):
        lam_init = 0.8 - 0.6 * math.exp(-0.3 * l)
        mod = _ada(cs, w_ada, b_ada.reshape(depth, 1, -1), l)
        wg = _cast_w(w_gate, l, 0, D, D, d_ff + ff_pad)
        wu = _cast_w(w_up, l, 0, D, D, d_ff + ff_pad)
        wd = _cast_w(w_down, l, 0, d_ff, d_ff + ff_pad, D)
        wo = [_cast_w(w_out, l, r0, n, n, D) for r0, n in ((0, a_w), (a_w, b_w), (a_w + b_w, D - a_w - b_w))]
        x2, y2 = _layer(x2, y2, B, mod, rope_a, rope_c, lb[l], lam_init, l == depth - 1,
                        _cast_w(w_in, l, 0, D, D, w_in.shape[2]), wo,
                        attn_q_gain[l], attn_k_gain[l], hgrn_o_gain[l], diff_q_gain[l], diff_k_gain[l],
                        diff_lambda[l], diff_o_gain[l], wg, wu, wd)
    return x2.reshape(B, T, D)
```

```python
import functools
import math

import numpy as np
import jax
import jax.numpy as jnp
from jax import lax
from jax.experimental import pallas as pl
from jax.experimental.pallas import tpu as pltpu

F32 = jnp.float32
BF16 = jnp.bfloat16

HEAD_DIM = 128
GRID_W = 64
HG_CHUNK = 64
HG_UNROLL = 8
HG_GROUP = 4
ROPE_THETA = 10000.0
EPS = 1e-6
FFN_UP_TILES = (1024, 256)
FFN_DOWN_TILES = (512, 512)
VMEM_LIMIT = 56 * 1024 * 1024


def _pick(dim, pref, align):
    t = min(pref, dim)
    t -= t % align
    while t >= align:
        if dim % t == 0:
            return t
        t -= align
    return dim


def _params(*sem):
    return pltpu.CompilerParams(dimension_semantics=sem, vmem_limit_bytes=VMEM_LIMIT)


def _mm_kernel(a_ref, w_ref, o_ref):
    o_ref[...] = jnp.dot(a_ref[...], w_ref[...].astype(BF16), preferred_element_type=F32).astype(o_ref.dtype)


def _mm(a, w, layer, out_dtype, tm=1024, tn=512):
    M, K = a.shape
    N = w.shape[2]
    tm = _pick(M, tm, 8)
    tn = _pick(N, tn, 128)
    return pl.pallas_call(
        _mm_kernel,
        grid=(M // tm, N // tn),
        in_specs=[pl.BlockSpec((tm, K), lambda i, j: (i, 0)),
                  pl.BlockSpec((None, K, tn), lambda i, j: (layer, 0, j))],
        out_specs=pl.BlockSpec((tm, tn), lambda i, j: (i, j)),
        out_shape=jax.ShapeDtypeStruct((M, N), out_dtype),
        compiler_params=_params("parallel", "arbitrary"),
        name="proj_in",
    )(a, w)


def _mm_res_kernel(*refs, nk, n_in):
    a_refs, w_refs = refs[:n_in], refs[n_in:2 * n_in]
    r_ref, g_ref, o_ref = refs[2 * n_in:2 * n_in + 3]
    d = jnp.dot(a_refs[0][...], w_refs[0][...], preferred_element_type=F32)
    for a_ref, w_ref in zip(a_refs[1:], w_refs[1:]):
        d = d + jnp.dot(a_ref[...], w_ref[...], preferred_element_type=F32)
    if nk == 1:
        o_ref[...] = r_ref[...] + g_ref[...] * d
        return
    acc_ref, = refs[2 * n_in + 3:]
    k = pl.program_id(2)

    @pl.when(k == 0)
    def _():
        acc_ref[...] = d

    @pl.when(k > 0)
    def _():
        acc_ref[...] += d

    @pl.when(k == nk - 1)
    def _():
        o_ref[...] = r_ref[...] + g_ref[...] * acc_ref[...]


def _mm_res(a_list, w_list, res, gate, tm=1024, tn=1024, tk_max=4096):
    M = a_list[0].shape[0]
    N = w_list[0].shape[1]
    G = gate.shape[0]
    n_in = len(a_list)
    tm = _pick(M // G, tm, 8)
    tn = _pick(N, tn, 128)
    K = a_list[0].shape[1]
    tk = K if (n_in > 1 or K <= tk_max) else _pick(K, tk_max, 256)
    nk = K // tk
    tiles_per_group = (M // G) // tm
    scratch = [] if nk == 1 else [pltpu.VMEM((tm, tn), F32)]
    if n_in == 1:
        a_specs = [pl.BlockSpec((tm, tk), lambda i, j, k: (i, k))]
        w_specs = [pl.BlockSpec((tk, tn), lambda i, j, k: (k, j))]
    else:
        a_specs = [pl.BlockSpec((tm, a.shape[1]), lambda i, j, k: (i, 0)) for a in a_list]
        w_specs = [pl.BlockSpec((w.shape[0], tn), lambda i, j, k: (0, j)) for w in w_list]
    return pl.pallas_call(
        functools.partial(_mm_res_kernel, nk=nk, n_in=n_in),
        grid=(M // tm, N // tn, nk),
        in_specs=a_specs + w_specs + [
            pl.BlockSpec((tm, tn), lambda i, j, k: (i, j)),
            pl.BlockSpec((None, 1, tn), lambda i, j, k: (i // tiles_per_group, 0, j))],
        out_specs=pl.BlockSpec((tm, tn), lambda i, j, k: (i, j)),
        out_shape=jax.ShapeDtypeStruct((M, N), F32),
        scratch_shapes=scratch,
        compiler_params=_params("parallel", "arbitrary", "arbitrary"),
        name="proj_res",
    )(*a_list, *w_list, res, gate)


def _mm_swiglu_kernel(a_ref, wg_ref, wu_ref, o_ref):
    a = a_ref[...]
    g = jnp.dot(a, wg_ref[...].astype(BF16), preferred_element_type=F32)
    u = jnp.dot(a, wu_ref[...].astype(BF16), preferred_element_type=F32)
    o_ref[...] = (g * jax.nn.sigmoid(g) * u).astype(o_ref.dtype)


def _mm_swiglu(a, wg, wu, layer, tm=FFN_UP_TILES[0], tn=FFN_UP_TILES[1]):
    M, K = a.shape
    N = wg.shape[2]
    tm = _pick(M, tm, 8)
    tn = _pick(N, tn, 128)
    return pl.pallas_call(
        _mm_swiglu_kernel,
        grid=(M // tm, N // tn),
        in_specs=[pl.BlockSpec((tm, K), lambda i, j: (i, 0)),
                  pl.BlockSpec((None, K, tn), lambda i, j: (layer, 0, j)),
                  pl.BlockSpec((None, K, tn), lambda i, j: (layer, 0, j))],
        out_specs=pl.BlockSpec((tm, tn), lambda i, j: (i, j)),
        out_shape=jax.ShapeDtypeStruct((M, N), BF16),
        compiler_params=_params("parallel", "arbitrary"),
        name="ffn_up",
    )(a, wg, wu)


def _ada_kernel(c_ref, w_ref, b_ref, o_ref):
    o_ref[...] = jnp.dot(c_ref[...], w_ref[...], preferred_element_type=F32,
                         precision=lax.Precision.HIGHEST) + b_ref[...]


def _ada(cs, w, b, layer, tn=1024):
    R, D = cs.shape
    N = w.shape[2]
    tn = _pick(N, tn, 128)
    return pl.pallas_call(
        _ada_kernel,
        grid=(N // tn,),
        in_specs=[pl.BlockSpec((R, D), lambda j: (0, 0)),
                  pl.BlockSpec((None, D, tn), lambda j: (layer, 0, j)),
                  pl.BlockSpec((None, 1, tn), lambda j: (layer, 0, j))],
        out_specs=pl.BlockSpec((R, tn), lambda j: (0, j)),
        out_shape=jax.ShapeDtypeStruct((R, N), F32),
        compiler_params=_params("arbitrary"),
        name="adaln",
    )(cs, w, b)


def _cast_kernel(w_ref, o_ref, *, rows_valid, cols_valid):
    tr, tc = o_ref.shape
    r = pl.program_id(0) * tr + lax.broadcasted_iota(jnp.int32, (tr, tc), 0)
    c = pl.program_id(1) * tc + lax.broadcasted_iota(jnp.int32, (tr, tc), 1)
    o_ref[...] = jnp.where((r < rows_valid) & (c < cols_valid), w_ref[...], 0.0).astype(o_ref.dtype)


def _cast_w(w, layer, row0, rows, rows_out, cols_out, tc=2048):
    cols = w.shape[2]
    tr = math.gcd(math.gcd(row0, rows_out), 2048)
    tc = _pick(cols_out, tc, 128)
    assert (rows_out - tr) < rows and (cols_out - tc) < cols and row0 + rows <= w.shape[1]
    return pl.pallas_call(
        functools.partial(_cast_kernel, rows_valid=rows, cols_valid=cols),
        grid=(rows_out // tr, cols_out // tc),
        in_specs=[pl.BlockSpec((None, tr, tc), lambda i, j: (layer, row0 // tr + i, j))],
        out_specs=pl.BlockSpec((tr, tc), lambda i, j: (i, j)),
        out_shape=jax.ShapeDtypeStruct((rows_out, cols_out), BF16),
        compiler_params=_params("parallel", "parallel"),
        name="cast_w",
    )(w)


def _norm_mod_kernel(x_ref, sh_ref, sc_ref, o_ref):
    x = x_ref[...]
    y = x * lax.rsqrt(jnp.mean(x * x, axis=-1, keepdims=True) + EPS)
    o_ref[...] = (y * (1.0 + sc_ref[...]) + sh_ref[...]).astype(o_ref.dtype)


def _norm_mod(x, shift, scale, tm=512):
    M, D = x.shape
    G = shift.shape[0]
    tm = _pick(M // G, tm, 8)
    tiles_per_group = (M // G) // tm
    mod_spec = pl.BlockSpec((None, 1, D), lambda i: (i // tiles_per_group, 0, 0))
    return pl.pallas_call(
        _norm_mod_kernel,
        grid=(M // tm,),
        in_specs=[pl.BlockSpec((tm, D), lambda i: (i, 0)), mod_spec, mod_spec],
        out_specs=pl.BlockSpec((tm, D), lambda i: (i, 0)),
        out_shape=jax.ShapeDtypeStruct((M, D), BF16),
        compiler_params=_params("parallel"),
        name="norm_mod",
    )(x, shift, scale)


ATTN_PANEL = 1024
ATTN_FAST_RANGE = 64.0
LOG2E = math.log2(math.e)


def _attn_kernel(*refs, G, tq, tk, n_kv, diff, post_scale):
    if n_kv:
        (q_ref, kc_ref, vc_ref, kx_ref, vx_ref, lam_ref, fast_ref, gain_ref,
         o_ref, qt_ref, m_ref, l_ref, acc_ref, s_ref) = refs
    else:
        q_ref, kc_ref, vc_ref, lam_ref, fast_ref, gain_ref, o_ref, qt_ref, m_ref, l_ref, acc_ref = refs
    n_rows = 2 if diff else G
    width = n_rows * tq
    pw = min(ATTN_PANEL, width)
    starts = list(range(0, width, pw))
    tn = (((0,), (0,)), ((), ()))

    def scores(k, c0):
        return jnp.dot(k, qt_ref[:, c0:c0 + pw], preferred_element_type=F32)

    def update(s, v, c0, mode):
        cols = slice(c0, c0 + pw)
        m_cur = jnp.max(s, axis=0, keepdims=True)
        if mode == "first":
            m_new = m_cur
            p = jnp.exp2(s - m_new)
            l_ref[:, cols] = jnp.sum(p, axis=0, keepdims=True)
            acc_ref[:, cols] = lax.dot_general(v, p.astype(BF16), tn, preferred_element_type=F32)
        elif mode == "safe":
            m_prev = m_ref[:, cols]
            m_new = jnp.maximum(m_prev, m_cur)
            alpha = jnp.exp2(m_prev - m_new)
            p = jnp.exp2(s - m_new)
            l_ref[:, cols] = alpha * l_ref[:, cols] + jnp.sum(p, axis=0, keepdims=True)
            acc_ref[:, cols] = alpha * acc_ref[:, cols] + lax.dot_general(
                v, p.astype(BF16), tn, preferred_element_type=F32)
        else:
            m_prev = m_ref[:, cols]
            p = jnp.exp2(s - m_prev)
            m_new = jnp.maximum(m_prev, m_cur)
            alpha = jnp.exp2(m_prev - m_new)
            l_ref[:, cols] = alpha * (l_ref[:, cols] + jnp.sum(p, axis=0, keepdims=True))
            acc_ref[:, cols] = alpha * (acc_ref[:, cols] + lax.dot_general(
                v, p.astype(BF16), tn, preferred_element_type=F32))
        m_ref[:, cols] = m_new

    q = q_ref[...].astype(F32)
    if diff:
        lane = lax.broadcasted_iota(jnp.int32, q.shape, 1)
        qt_ref[:, 0:tq] = jnp.where(lane < HEAD_DIM // 2, q, 0.0).T.astype(BF16)
        qt_ref[:, tq:2 * tq] = jnp.where(lane >= HEAD_DIM // 2, q, 0.0).T.astype(BF16)
    else:
        for g in range(G):
            qt_ref[:, g * tq:(g + 1) * tq] = q[:, g * HEAD_DIM:(g + 1) * HEAD_DIM].T.astype(BF16)

    kc = kc_ref[...]
    vc = vc_ref[...].astype(BF16)
    s = scores(kc, starts[0])
    for n, c0 in enumerate(starts):
        if n + 1 < len(starts):
            s_next = scores(kc, starts[n + 1])
        elif n_kv:
            s_next = scores(kx_ref[0:tk, :], starts[0])
        else:
            s_next = None
        update(s, vc, c0, "first")
        s = s_next

    if n_kv:
        s_ref[...] = s

        def block_two_pass(jb, carry):
            r0 = pl.multiple_of(jb * tk, tk)
            r1 = pl.multiple_of(jnp.minimum(jb + 1, n_kv - 1) * tk, tk)
            k = kx_ref[pl.ds(r0, tk), :]
            v = vx_ref[pl.ds(r0, tk), :].astype(BF16)
            s = s_ref[...]
            for n, c0 in enumerate(starts):
                if n + 1 < len(starts):
                    s_next = scores(k, starts[n + 1])
                else:
                    s_next = scores(kx_ref[pl.ds(r1, tk), :], starts[0])
                update(s, v, c0, "safe")
                s = s_next
            s_ref[...] = s
            return carry

        def block_one_pass(jb, carry):
            r0 = pl.multiple_of(jb * tk, tk)
            k = kx_ref[pl.ds(r0, tk), :]
            v = vx_ref[pl.ds(r0, tk), :].astype(BF16)
            for c0 in starts:
                update(scores(k, c0), v, c0, "fast")
            return carry

        @pl.when(fast_ref[0] > 0)
        def _():
            lax.fori_loop(0, n_kv, block_one_pass, 0)

        @pl.when(fast_ref[0] <= 0)
        def _():
            lax.fori_loop(0, n_kv, block_two_pass, 0)

    o = acc_ref[...] * (1.0 / l_ref[...])
    if diff:
        d = o[:, 0:tq] - lam_ref[0] * o[:, tq:2 * tq]
        d = d * lax.rsqrt(jnp.mean(d * d, axis=0, keepdims=True) + EPS)
        o_ref[...] = (d * gain_ref[...] * post_scale).T.astype(o_ref.dtype)
    else:
        for g in range(G):
            o_ref[:, g * HEAD_DIM:(g + 1) * HEAD_DIM] = o[:, g * tq:(g + 1) * tq].T.astype(o_ref.dtype)


def _attention(q, kc, vc, kx, vx, *, G, v_slot, score_bound, diff=False, lam=None, gain=None, post_scale=1.0,
               tq=2048, tk=2048):
    B, T, _ = q.shape
    L = kc.shape[1]
    H = kc.shape[2] // HEAD_DIM
    gq = 1 if diff else G
    n_rows = 2 if diff else G
    tq = _pick(T, tq, 128)
    if lam is None:
        lam = jnp.zeros((1,), F32)
    if gain is None:
        gain = jnp.ones((HEAD_DIM,), F32)
    in_specs = [pl.BlockSpec((None, tq, gq * HEAD_DIM), lambda b, h, i: (b, i, h)),
                pl.BlockSpec((None, L, HEAD_DIM), lambda b, h, i: (b, 0, h)),
                pl.BlockSpec((None, L, HEAD_DIM), lambda b, h, i: (b, 0, v_slot + h))]
    args = [q, kc, vc]
    scratch = [pltpu.VMEM((HEAD_DIM, n_rows * tq), BF16),
               pltpu.VMEM((1, n_rows * tq), F32),
               pltpu.VMEM((1, n_rows * tq), F32),
               pltpu.VMEM((HEAD_DIM, n_rows * tq), F32)]
    if kx is not None:
        S = kx.shape[1]
        tk = _pick(S, tk, 128)
        n_kv = S // tk
        in_specs += [pl.BlockSpec((None, S, HEAD_DIM), lambda b, h, i: (b, 0, h)),
                     pl.BlockSpec((None, S, HEAD_DIM), lambda b, h, i: (b, 0, v_slot + h))]
        args += [kx, vx]
        scratch += [pltpu.VMEM((tk, min(ATTN_PANEL, n_rows * tq)), F32)]
    else:
        n_kv = 0
    fast = (2.0 * score_bound <= ATTN_FAST_RANGE).astype(jnp.int32).reshape(1)
    in_specs += [pl.BlockSpec(memory_space=pltpu.SMEM), pl.BlockSpec(memory_space=pltpu.SMEM),
                 pl.BlockSpec((HEAD_DIM, 1), lambda b, h, i: (0, 0))]
    args += [lam.reshape(1).astype(F32), fast, gain.reshape(HEAD_DIM, 1).astype(F32)]
    return pl.pallas_call(
        functools.partial(_attn_kernel, G=G, tq=tq, tk=tk, n_kv=n_kv, diff=diff, post_scale=post_scale),
        grid=(B, H, T // tq),
        in_specs=in_specs,
        out_specs=pl.BlockSpec((None, tq, gq * HEAD_DIM), lambda b, h, i: (b, i, h)),
        out_shape=jax.ShapeDtypeStruct((B, T, H * gq * HEAD_DIM), BF16),
        scratch_shapes=scratch,
        compiler_params=_params("parallel", "parallel", "arbitrary"),
        name="attn_diff" if diff else "attn_gqa",
    )(*args)


_HG_LEVELS = (32, 16, 8, 4, 2, 1)


def _hgrn_consts(reverse):
    C = HG_CHUNK
    mst = np.zeros((2 + len(_HG_LEVELS), C, C), np.float32)
    msk = np.zeros((len(_HG_LEVELS) + 1, C, C), np.float32)
    for t in range(C):
        if not reverse:
            mst[0, t, :t + 1] = 1.0
            mst[1, t, t + 1:] = 1.0
        else:
            mst[0, t, t:] = 1.0
            mst[1, t, :t] = 1.0
        msk[len(_HG_LEVELS), t, t] = 1.0
    for l, m in enumerate(_HG_LEVELS):
        for t in range(C):
            start = (t // (2 * m)) * 2 * m
            mid = start + m
            second = t >= mid
            if not reverse:
                if second:
                    mst[2 + l, t, mid:t + 1] = 1.0
                    msk[l, t, start:mid] = 1.0
                else:
                    mst[2 + l, t, t + 1:mid] = 1.0
            else:
                if not second:
                    mst[2 + l, t, t:mid] = 1.0
                    msk[l, t, mid:start + 2 * m] = 1.0
                else:
                    mst[2 + l, t, mid:t] = 1.0
    mst = mst.reshape(-1, C)
    return np.concatenate([mst, mst], axis=1), msk


def _hgrn_kernel(*refs, n_chunks, unroll, reverse, nh, final):
    q_ref, v_ref, z_ref, llb_ref, l1p_ref, s0_ref, mst_ref, msk_ref = refs[:8]
    if final:
        of_ref, gate_ref, hog_ref, o_ref, sf_ref, st_ref = refs[8:]
    else:
        o_ref, sf_ref, st_ref = refs[8:]
    C = HG_CHUNK
    tb = pl.program_id(2)
    ntb = pl.num_programs(2)

    @pl.when(tb == 0)
    def _():
        st_ref[...] = s0_ref[...]

    mst = mst_ref[...]
    row = lax.broadcasted_iota(jnp.int32, (C, HEAD_DIM), 0)
    is_q = [((row // m) % 2) == (0 if reverse else 1) for m in _HG_LEVELS]
    pair = [msk_ref[l] > 0.5 for l in range(len(_HG_LEVELS) + 1)]
    nt = (((1,), (1,)), ((), ()))
    tn = (((0,), (0,)), ((), ()))

    def chunks(ci, carry):
        lanes = []
        for u in range(unroll):
            c = ci * unroll + u
            c = (n_chunks - 1 - c) if reverse else c
            rows = pl.ds(pl.multiple_of(c * C, C), C)
            lanes += [(rows, hh, slice(hh * HEAD_DIM, (hh + 1) * HEAD_DIM)) for hh in range(nh)]

        qs, vs, ks, gparts = [], [], [], []
        for rows, hh, cols in lanes:
            qr = q_ref[rows, cols]
            qs.append(qr * jax.nn.sigmoid(qr))
            vs.append(v_ref[rows, cols].astype(BF16))
            z = z_ref[rows, cols]
            t = l1p_ref[:, cols] + (jnp.minimum(z, 0.0) - jnp.log1p(jnp.exp(-jnp.abs(z))))
            llb = llb_ref[:, cols]
            g = jnp.maximum(llb, t) + jnp.log1p(jnp.exp(-jnp.abs(llb - t)))
            ks.append(1.0 - jnp.exp(g))
            g2 = g * LOG2E
            g_hi = g2.astype(BF16)
            gparts.append(jnp.concatenate([g_hi, (g2 - g_hi.astype(F32)).astype(BF16)], axis=0))

        sums = [jnp.dot(mst, jnp.concatenate(gparts[n:n + HG_GROUP], axis=1), preferred_element_type=F32)
                for n in range(0, len(gparts), HG_GROUP)]

        q_ins, o_intra, incs, e_tots = [], [], [], []
        for n, (q, v, k) in enumerate(zip(qs, vs, ks)):
            c0 = (n % HG_GROUP) * HEAD_DIM
            e = jnp.exp2(sums[n // HG_GROUP][:, c0:c0 + HEAD_DIM])
            a = jnp.where(pair[len(_HG_LEVELS)],
                          lax.dot_general(q.astype(BF16), k.astype(BF16), nt, preferred_element_type=F32), 0.0)
            for l in range(len(_HG_LEVELS)):
                x = (jnp.where(is_q[l], q, k) * e[(2 + l) * C:(3 + l) * C]).astype(BF16)
                a = jnp.where(pair[l], lax.dot_general(x, x, nt, preferred_element_type=F32), a)
            e_in = e[0:C]
            q_ins.append((q * e_in).astype(BF16))
            e_tots.append(e_in[0:1] if reverse else e_in[C - 1:C])
            o_intra.append(jnp.dot(a.astype(BF16), v, preferred_element_type=F32))
            incs.append(lax.dot_general(v, (k * e[C:2 * C]).astype(BF16), tn, preferred_element_type=F32))

        state = [st_ref[hh] for hh in range(nh)]
        st_in = []
        for (rows, hh, cols), e_tot, inc in zip(lanes, e_tots, incs):
            st_in.append(state[hh].astype(BF16))
            state[hh] = state[hh] * e_tot + inc
        for hh in range(nh):
            st_ref[hh] = state[hh]

        for (rows, hh, cols), q_in, st, o in zip(lanes, q_ins, st_in, o_intra):
            o = o + lax.dot_general(q_in, st, nt, preferred_element_type=F32)
            if final:
                o = o + of_ref[rows, cols]
                o = o * lax.rsqrt(jnp.mean(o * o, axis=-1, keepdims=True) + EPS) * hog_ref[...]
                gt = gate_ref[rows, cols]
                o = o * (gt * jax.nn.sigmoid(gt))
            o_ref[rows, cols] = o.astype(o_ref.dtype)
        return carry

    lax.fori_loop(0, n_chunks // unroll, chunks, 0)

    @pl.when(tb == ntb - 1)
    def _():
        sf_ref[...] = st_ref[...]


def _hgrn_scan(p, slots, llb, l1p, s0, *, reverse, o_fwd=None, o_gain=None, tb=1024):
    B, T, _ = p.shape
    W = llb.shape[1]
    H = W // HEAD_DIM
    final = o_fwd is not None
    nh = 2 if (H % 2 == 0 and all(s % 2 == 0 for s in slots)) else 1
    wb = nh * HEAD_DIM
    tb = _pick(T, tb, HG_CHUNK)
    n_chunks = tb // HG_CHUNK
    ntb = T // tb
    mst, msk = _hgrn_consts(reverse)

    def seq_spec(slot):
        s = slot // nh
        if reverse:
            return pl.BlockSpec((None, tb, wb), lambda b, h, t: (b, ntb - 1 - t, s + h))
        return pl.BlockSpec((None, tb, wb), lambda b, h, t: (b, t, s + h))

    vec_spec = pl.BlockSpec((1, wb), lambda b, h, t: (0, h))
    st_spec = pl.BlockSpec((None, nh, HEAD_DIM, HEAD_DIM), lambda b, h, t: (b, h, 0, 0))
    in_specs = [seq_spec(slots[0]), seq_spec(slots[1]), seq_spec(slots[2]), vec_spec, vec_spec, st_spec,
                pl.BlockSpec(mst.shape, lambda b, h, t: (0, 0)),
                pl.BlockSpec(msk.shape, lambda b, h, t: (0, 0, 0))]
    args = [p, p, p, llb, l1p, s0, jnp.asarray(mst, BF16), jnp.asarray(msk, F32)]
    if final:
        in_specs += [seq_spec(0), seq_spec(slots[3]), pl.BlockSpec((1, HEAD_DIM), lambda b, h, t: (0, 0))]
        args += [o_fwd, p, o_gain.reshape(1, HEAD_DIM).astype(F32)]
    return pl.pallas_call(
        functools.partial(_hgrn_kernel, n_chunks=n_chunks, unroll=math.gcd(n_chunks, HG_UNROLL), reverse=reverse,
                          nh=nh, final=final),
        grid=(B, H // nh, ntb),
        in_specs=in_specs,
        out_specs=[seq_spec(0), st_spec],
        out_shape=[jax.ShapeDtypeStruct((B, T, W), BF16 if final else F32),
                   jax.ShapeDtypeStruct((B, H, HEAD_DIM, HEAD_DIM), F32)],
        scratch_shapes=[pltpu.VMEM((nh, HEAD_DIM, HEAD_DIM), F32)],
        compiler_params=_params("parallel", "parallel", "arbitrary"),
        name="hgrn_bwd" if reverse else "hgrn_fwd",
    )(*args)


def _prep_kernel(x_ref, gain_ref, cos_ref, sin_ref, o_ref, *, ns, halves, rope, scale):
    gain = gain_ref[...]
    lane = lax.broadcasted_iota(jnp.int32, (x_ref.shape[0], HEAD_DIM), 1)
    low = lane < HEAD_DIM // 2
    even = (lane % 2) == 0
    for s in range(ns):
        cols = slice(s * HEAD_DIM, (s + 1) * HEAD_DIM)
        x = x_ref[:, cols]
        xx = x * x
        if halves:
            m0 = jnp.sum(jnp.where(low, xx, 0.0), axis=-1, keepdims=True) * (2.0 / HEAD_DIM)
            m1 = jnp.sum(jnp.where(low, 0.0, xx), axis=-1, keepdims=True) * (2.0 / HEAD_DIM)
            inv = jnp.where(low, lax.rsqrt(m0 + EPS), lax.rsqrt(m1 + EPS))
        else:
            inv = lax.rsqrt(jnp.mean(xx, axis=-1, keepdims=True) + EPS)
        y = x * inv * gain
        if rope:
            partner = jnp.where(even, pltpu.roll(y, HEAD_DIM - 1, 1), pltpu.roll(y, 1, 1))
            y = y * cos_ref[...] + partner * sin_ref[...]
        o_ref[:, cols] = (y * scale).astype(o_ref.dtype)


def _prep(p, slot, n_slots, gain, tables, *, halves, scale, seq_len, tm=1024):
    M = p.shape[0]
    ns = math.gcd(slot, n_slots)
    tm = _pick(seq_len, tm, 8)
    tiles_per_seq = seq_len // tm
    rope = tables is not None
    if rope:
        cos, sin = tables
        tab_spec = pl.BlockSpec((tm, HEAD_DIM), lambda i, j: (i % tiles_per_seq, 0))
    else:
        cos = sin = jnp.zeros((8, HEAD_DIM), F32)
        tab_spec = pl.BlockSpec((8, HEAD_DIM), lambda i, j: (0, 0))
    return pl.pallas_call(
        functools.partial(_prep_kernel, ns=ns, halves=halves, rope=rope, scale=scale),
        grid=(M // tm, n_slots // ns),
        in_specs=[pl.BlockSpec((tm, ns * HEAD_DIM), lambda i, j: (i, slot // ns + j)),
                  pl.BlockSpec((1, HEAD_DIM), lambda i, j: (0, 0)), tab_spec, tab_spec],
        out_specs=pl.BlockSpec((tm, ns * HEAD_DIM), lambda i, j: (i, j)),
        out_shape=jax.ShapeDtypeStruct((M, n_slots * HEAD_DIM), BF16),
        compiler_params=_params("parallel", "parallel"),
        name="prep_qk",
    )(p, gain.reshape(1, HEAD_DIM).astype(F32), cos, sin)


def _rope_tables(seq_len, dim):
    rows = seq_len // GRID_W
    row = jnp.repeat(jnp.arange(rows, dtype=F32), GRID_W)
    col = jnp.tile(jnp.arange(GRID_W, dtype=F32), rows)
    half = dim // 2
    inv = ROPE_THETA ** (-jnp.arange(0, half, 2, dtype=F32) / half)
    ang = jnp.concatenate([row[:, None] * inv, col[:, None] * inv], axis=-1)
    cos = jnp.repeat(jnp.cos(ang), 2, axis=-1)
    sin = jnp.repeat(jnp.sin(ang), 2, axis=-1) * jnp.tile(jnp.asarray([-1.0, 1.0], F32), half)
    reps = HEAD_DIM // dim
    return jnp.tile(cos, (1, reps)), jnp.tile(sin, (1, reps))


def _dims(D):
    n = D // HEAD_DIM
    a_heads = 3 * n // 8
    a_kv = a_heads // 3
    b_heads = (n - a_heads) // 2
    c_heads = n - a_heads - b_heads
    return a_heads, a_kv, b_heads, c_heads


def _layer(x2, y2, B, layer, mod, rope_a, rope_c, lb, lam_init, last,
           w_in, w_out, aq_g, ak_g, ho_g, dq_g, dk_g, d_lam, do_g, w_gate, w_up, w_down):
    D = x2.shape[1]
    T = x2.shape[0] // B
    L = y2.shape[0] // B
    a_heads, a_kv, b_heads, c_heads = _dims(D)
    a_group = a_heads // a_kv
    half = HEAD_DIM // 2
    widths = (a_heads, c_heads, b_heads, b_heads, a_kv, a_kv, c_heads, c_heads, b_heads, b_heads, b_heads)
    (s_aq, s_cq, s_bq, s_bg, s_ak, s_av, s_ck, s_cv, s_bi, s_bff, s_bfb) = (
        np.concatenate([[0], np.cumsum(widths)[:-1]]).tolist())

    mx = [m[:, None, :] for m in jnp.split(mod[0:B], 6, axis=-1)]
    my = [m[:, None, :] for m in jnp.split(mod[B:B + 1], 6, axis=-1)]

    px = _mm(_norm_mod(x2, mx[0], mx[1]), w_in, layer, F32)
    py = _mm(_norm_mod(y2, my[0], my[1]), w_in, layer, F32)
    px3 = px.reshape(B, T, -1)
    py3 = py.reshape(B, L, -1)

    a_scale = HEAD_DIM ** -0.5 * LOG2E
    qa_x = _prep(px, s_aq, a_heads, aq_g, rope_a, halves=False, scale=a_scale, seq_len=T).reshape(B, T, -1)
    ka_x = _prep(px, s_ak, a_kv, ak_g, rope_a, halves=False, scale=1.0, seq_len=T).reshape(B, T, -1)
    ka_y = _prep(py, s_ak, a_kv, ak_g, None, halves=False, scale=1.0, seq_len=L).reshape(B, L, -1)
    a_bound = 1.02 * HEAD_DIM * a_scale * jnp.max(jnp.abs(aq_g)) * jnp.max(jnp.abs(ak_g))
    oa_x = _attention(qa_x, ka_y, py3, ka_x, px3, G=a_group, v_slot=s_av, score_bound=a_bound)

    c_scale = half ** -0.5 * LOG2E
    dl = d_lam.astype(F32)
    lam = jnp.exp(jnp.sum(dl[0] * dl[1])) - jnp.exp(jnp.sum(dl[2] * dl[3])) + lam_init
    dq_g2 = jnp.tile(dq_g, 2)
    dk_g2 = jnp.tile(dk_g, 2)
    qc_x = _prep(px, s_cq, c_heads, dq_g2, rope_c, halves=True, scale=c_scale, seq_len=T).reshape(B, T, -1)
    kc_x = _prep(px, s_ck, c_heads, dk_g2, rope_c, halves=True, scale=1.0, seq_len=T).reshape(B, T, -1)
    kc_y = _prep(py, s_ck, c_heads, dk_g2, None, halves=True, scale=1.0, seq_len=L).reshape(B, L, -1)
    c_bound = 1.02 * half * c_scale * jnp.max(jnp.abs(dq_g)) * jnp.max(jnp.abs(dk_g))
    oc_x = _attention(qc_x, kc_y, py3, kc_x, px3, G=2, v_slot=s_cv, score_bound=c_bound, diff=True, lam=lam,
                      gain=do_g, post_scale=1.0 - lam_init)

    llb = jnp.log(lb).reshape(2, 1, -1)
    l1p = jnp.log1p(-lb).reshape(2, 1, -1)
    s0 = jnp.zeros((B, b_heads, HEAD_DIM, HEAD_DIM), F32)
    f_slots = (s_bq, s_bi, s_bff, s_bg)
    b_slots = (s_bq, s_bi, s_bfb, s_bg)
    of_y, s_f = _hgrn_scan(py3, f_slots, llb[0], l1p[0], s0, reverse=False)
    ob_y, s_b = _hgrn_scan(py3, b_slots, llb[1], l1p[1], s0, reverse=True, o_fwd=of_y, o_gain=ho_g)
    of_x, _ = _hgrn_scan(px3, f_slots, llb[0], l1p[0], s_f, reverse=False)
    ob_x, _ = _hgrn_scan(px3, b_slots, llb[1], l1p[1], s_b, reverse=True, o_fwd=of_x, o_gain=ho_g)

    x2 = _mm_res([oa_x.reshape(B * T, -1), ob_x.reshape(B * T, -1), oc_x.reshape(B * T, -1)], w_out, x2, mx[2])
    x2 = _mm_res([_mm_swiglu(_norm_mod(x2, mx[3], mx[4]), w_gate, w_up, layer)], [w_down], x2, mx[5],
                 tm=FFN_DOWN_TILES[0], tn=FFN_DOWN_TILES[1], tk_max=w_down.shape[0])
    if last:
        return x2, None

    qa_y = _prep(py, s_aq, a_heads, aq_g, None, halves=False, scale=a_scale, seq_len=L).reshape(B, L, -1)
    oa_y = _attention(qa_y, ka_y, py3, None, None, G=a_group, v_slot=s_av, score_bound=a_bound)
    qc_y = _prep(py, s_cq, c_heads, dq_g2, None, halves=True, scale=c_scale, seq_len=L).reshape(B, L, -1)
    oc_y = _attention(qc_y, kc_y, py3, None, None, G=2, v_slot=s_cv, score_bound=c_bound, diff=True, lam=lam,
                      gain=do_g, post_scale=1.0 - lam_init)
    y2 = _mm_res([oa_y.reshape(B * L, -1), ob_y.reshape(B * L, -1), oc_y.reshape(B * L, -1)], w_out, y2, my[2])
    y2 = _mm_res([_mm_swiglu(_norm_mod(y2, my[3], my[4]), w_gate, w_up, layer)], [w_down], y2, my[5],
                 tm=FFN_DOWN_TILES[0], tn=FFN_DOWN_TILES[1], tk_max=w_down.shape[0])
    return x2, y2


def kernel(x, c, ctx, c_ctx, w_ada, b_ada, w_in, w_out, attn_q_gain, attn_k_gain,
           hgrn_lb_logits, hgrn_o_gain, diff_q_gain, diff_k_gain, diff_lambda, diff_o_gain,
           w_gate, w_up, w_down):
    B, T, D = x.shape
    L = ctx.shape[1]
    depth = w_ada.shape[0]
    a_heads, _, b_heads, _ = _dims(D)
    rope_a = _rope_tables(T, HEAD_DIM)
    rope_c = _rope_tables(T, HEAD_DIM // 2)

    sm = jax.nn.softmax(hgrn_lb_logits.astype(F32), axis=0)
    lb = jnp.concatenate([jnp.zeros_like(sm[:1]), jnp.cumsum(sm[1:], axis=0)], axis=0)

    cs = jnp.concatenate([c, c_ctx[None, :]], axis=0)
    cs = jax.nn.silu(jnp.pad(cs, ((0, 8 - (B + 1)), (0, 0))))

    d_ff = w_gate.shape[-1]
    a_w = a_heads * HEAD_DIM
    b_w = b_heads * HEAD_DIM

    x2 = x.reshape(B * T, D)
    y2 = ctx.reshape(B * L, D)
    for l in range(depth):
        lam_init = 0.8 - 0.6 * math.exp(-0.3 * l)
        mod = _ada(cs, w_ada, b_ada.reshape(depth, 1, -1), l)
        wd = _cast_w(w_down, l, 0, d_ff, d_ff, D)
        wo = [_cast_w(w_out, l, r0, n, n, D) for r0, n in ((0, a_w), (a_w, b_w), (a_w + b_w, D - a_w - b_w))]
        x2, y2 = _layer(x2, y2, B, l, mod, rope_a, rope_c, lb[l], lam_init, l == depth - 1,
                        w_in, wo,
                        attn_q_gain[l], attn_k_gain[l], hgrn_o_gain[l], diff_q_gain[l], diff_k_gain[l],
                        diff_lambda[l], diff_o_gain[l], w_gate, w_up, wd)
    return x2.reshape(B, T, D)
```

```python
import functools
import math

import numpy as np
import jax
import jax.numpy as jnp
from jax import lax
from jax.experimental import pallas as pl
from jax.experimental.pallas import tpu as pltpu

F32 = jnp.float32
BF16 = jnp.bfloat16

HEAD_DIM = 128
GRID_W = 64
HG_CHUNK = 64
HG_UNROLL = 8
ROPE_THETA = 10000.0
EPS = 1e-6
FFN_UP_TILES = (1024, 256)
FFN_DOWN_TILES = (512, 512)
VMEM_LIMIT = 56 * 1024 * 1024


def _pick(dim, pref, align):
    t = min(pref, dim)
    t -= t % align
    while t >= align:
        if dim % t == 0:
            return t
        t -= align
    return dim


def _params(*sem):
    return pltpu.CompilerParams(dimension_semantics=sem, vmem_limit_bytes=VMEM_LIMIT)


def _mm_kernel(a_ref, w_ref, o_ref):
    o_ref[...] = jnp.dot(a_ref[...], w_ref[...].astype(BF16), preferred_element_type=F32).astype(o_ref.dtype)


def _mm(a, w, layer, out_dtype, tm=1024, tn=512):
    M, K = a.shape
    N = w.shape[2]
    tm = _pick(M, tm, 8)
    tn = _pick(N, tn, 128)
    return pl.pallas_call(
        _mm_kernel,
        grid=(M // tm, N // tn),
        in_specs=[pl.BlockSpec((tm, K), lambda i, j: (i, 0)),
                  pl.BlockSpec((None, K, tn), lambda i, j: (layer, 0, j))],
        out_specs=pl.BlockSpec((tm, tn), lambda i, j: (i, j)),
        out_shape=jax.ShapeDtypeStruct((M, N), out_dtype),
        compiler_params=_params("parallel", "arbitrary"),
        name="proj_in",
    )(a, w)


def _mm_res_kernel(*refs, nk, n_in):
    a_refs, w_refs = refs[:n_in], refs[n_in:2 * n_in]
    r_ref, g_ref, o_ref = refs[2 * n_in:2 * n_in + 3]
    d = jnp.dot(a_refs[0][...], w_refs[0][...], preferred_element_type=F32)
    for a_ref, w_ref in zip(a_refs[1:], w_refs[1:]):
        d = d + jnp.dot(a_ref[...], w_ref[...], preferred_element_type=F32)
    if nk == 1:
        o_ref[...] = r_ref[...] + g_ref[...] * d
        return
    acc_ref, = refs[2 * n_in + 3:]
    k = pl.program_id(2)

    @pl.when(k == 0)
    def _():
        acc_ref[...] = d

    @pl.when(k > 0)
    def _():
        acc_ref[...] += d

    @pl.when(k == nk - 1)
    def _():
        o_ref[...] = r_ref[...] + g_ref[...] * acc_ref[...]


def _mm_res(a_list, w_list, res, gate, tm=1024, tn=1024, tk_max=4096):
    M = a_list[0].shape[0]
    N = w_list[0].shape[1]
    G = gate.shape[0]
    n_in = len(a_list)
    tm = _pick(M // G, tm, 8)
    tn = _pick(N, tn, 128)
    K = a_list[0].shape[1]
    tk = K if (n_in > 1 or K <= tk_max) else _pick(K, tk_max, 256)
    nk = K // tk
    tiles_per_group = (M // G) // tm
    scratch = [] if nk == 1 else [pltpu.VMEM((tm, tn), F32)]
    if n_in == 1:
        a_specs = [pl.BlockSpec((tm, tk), lambda i, j, k: (i, k))]
        w_specs = [pl.BlockSpec((tk, tn), lambda i, j, k: (k, j))]
    else:
        a_specs = [pl.BlockSpec((tm, a.shape[1]), lambda i, j, k: (i, 0)) for a in a_list]
        w_specs = [pl.BlockSpec((w.shape[0], tn), lambda i, j, k: (0, j)) for w in w_list]
    return pl.pallas_call(
        functools.partial(_mm_res_kernel, nk=nk, n_in=n_in),
        grid=(M // tm, N // tn, nk),
        in_specs=a_specs + w_specs + [
            pl.BlockSpec((tm, tn), lambda i, j, k: (i, j)),
            pl.BlockSpec((None, 1, tn), lambda i, j, k: (i // tiles_per_group, 0, j))],
        out_specs=pl.BlockSpec((tm, tn), lambda i, j, k: (i, j)),
        out_shape=jax.ShapeDtypeStruct((M, N), F32),
        scratch_shapes=scratch,
        compiler_params=_params("parallel", "arbitrary", "arbitrary"),
        name="proj_res",
    )(*a_list, *w_list, res, gate)


def _mm_swiglu_kernel(a_ref, wg_ref, wu_ref, o_ref):
    a = a_ref[...]
    g = jnp.dot(a, wg_ref[...].astype(BF16), preferred_element_type=F32)
    u = jnp.dot(a, wu_ref[...].astype(BF16), preferred_element_type=F32)
    o_ref[...] = (g * jax.nn.sigmoid(g) * u).astype(o_ref.dtype)


def _mm_swiglu(a, wg, wu, layer, tm=FFN_UP_TILES[0], tn=FFN_UP_TILES[1]):
    M, K = a.shape
    N = wg.shape[2]
    tm = _pick(M, tm, 8)
    tn = _pick(N, tn, 128)
    return pl.pallas_call(
        _mm_swiglu_kernel,
        grid=(M // tm, N // tn),
        in_specs=[pl.BlockSpec((tm, K), lambda i, j: (i, 0)),
                  pl.BlockSpec((None, K, tn), lambda i, j: (layer, 0, j)),
                  pl.BlockSpec((None, K, tn), lambda i, j: (layer, 0, j))],
        out_specs=pl.BlockSpec((tm, tn), lambda i, j: (i, j)),
        out_shape=jax.ShapeDtypeStruct((M, N), BF16),
        compiler_params=_params("parallel", "arbitrary"),
        name="ffn_up",
    )(a, wg, wu)


def _ada_kernel(c_ref, w_ref, b_ref, o_ref):
    o_ref[...] = jnp.dot(c_ref[...], w_ref[...], preferred_element_type=F32,
                         precision=lax.Precision.HIGHEST) + b_ref[...]


def _ada(cs, w, b, layer, tn=1024):
    R, D = cs.shape
    N = w.shape[2]
    tn = _pick(N, tn, 128)
    return pl.pallas_call(
        _ada_kernel,
        grid=(N // tn,),
        in_specs=[pl.BlockSpec((R, D), lambda j: (0, 0)),
                  pl.BlockSpec((None, D, tn), lambda j: (layer, 0, j)),
                  pl.BlockSpec((None, 1, tn), lambda j: (layer, 0, j))],
        out_specs=pl.BlockSpec((R, tn), lambda j: (0, j)),
        out_shape=jax.ShapeDtypeStruct((R, N), F32),
        compiler_params=_params("arbitrary"),
        name="adaln",
    )(cs, w, b)


def _cast_kernel(w_ref, o_ref, *, rows_valid, cols_valid):
    tr, tc = o_ref.shape
    r = pl.program_id(0) * tr + lax.broadcasted_iota(jnp.int32, (tr, tc), 0)
    c = pl.program_id(1) * tc + lax.broadcasted_iota(jnp.int32, (tr, tc), 1)
    o_ref[...] = jnp.where((r < rows_valid) & (c < cols_valid), w_ref[...], 0.0).astype(o_ref.dtype)


def _cast_w(w, layer, row0, rows, rows_out, cols_out, tc=2048):
    cols = w.shape[2]
    tr = math.gcd(math.gcd(row0, rows_out), 2048)
    tc = _pick(cols_out, tc, 128)
    assert (rows_out - tr) < rows and (cols_out - tc) < cols and row0 + rows <= w.shape[1]
    return pl.pallas_call(
        functools.partial(_cast_kernel, rows_valid=rows, cols_valid=cols),
        grid=(rows_out // tr, cols_out // tc),
        in_specs=[pl.BlockSpec((None, tr, tc), lambda i, j: (layer, row0 // tr + i, j))],
        out_specs=pl.BlockSpec((tr, tc), lambda i, j: (i, j)),
        out_shape=jax.ShapeDtypeStruct((rows_out, cols_out), BF16),
        compiler_params=_params("parallel", "parallel"),
        name="cast_w",
    )(w)


def _norm_mod_kernel(x_ref, sh_ref, sc_ref, o_ref):
    x = x_ref[...]
    y = x * lax.rsqrt(jnp.mean(x * x, axis=-1, keepdims=True) + EPS)
    o_ref[...] = (y * (1.0 + sc_ref[...]) + sh_ref[...]).astype(o_ref.dtype)


def _norm_mod(x, shift, scale, tm=512):
    M, D = x.shape
    G = shift.shape[0]
    tm = _pick(M // G, tm, 8)
    tiles_per_group = (M // G) // tm
    mod_spec = pl.BlockSpec((None, 1, D), lambda i: (i // tiles_per_group, 0, 0))
    return pl.pallas_call(
        _norm_mod_kernel,
        grid=(M // tm,),
        in_specs=[pl.BlockSpec((tm, D), lambda i: (i, 0)), mod_spec, mod_spec],
        out_specs=pl.BlockSpec((tm, D), lambda i: (i, 0)),
        out_shape=jax.ShapeDtypeStruct((M, D), BF16),
        compiler_params=_params("parallel"),
        name="norm_mod",
    )(x, shift, scale)


ATTN_PANEL = 1024
ATTN_FAST_RANGE = 64.0
LOG2E = math.log2(math.e)


def _attn_kernel(*refs, G, tq, tk, n_kv, diff, post_scale):
    if n_kv:
        (q_ref, kc_ref, vc_ref, kx_ref, vx_ref, lam_ref, fast_ref, gain_ref,
         o_ref, qt_ref, m_ref, l_ref, acc_ref, s_ref) = refs
    else:
        q_ref, kc_ref, vc_ref, lam_ref, fast_ref, gain_ref, o_ref, qt_ref, m_ref, l_ref, acc_ref = refs
    n_rows = 2 if diff else G
    width = n_rows * tq
    pw = min(ATTN_PANEL, width)
    starts = list(range(0, width, pw))
    tn = (((0,), (0,)), ((), ()))

    def scores(k, c0):
        return jnp.dot(k, qt_ref[:, c0:c0 + pw], preferred_element_type=F32)

    def update(s, v, c0, mode):
        cols = slice(c0, c0 + pw)
        m_cur = jnp.max(s, axis=0, keepdims=True)
        if mode == "first":
            m_new = m_cur
            p = jnp.exp2(s - m_new)
            l_ref[:, cols] = jnp.sum(p, axis=0, keepdims=True)
            acc_ref[:, cols] = lax.dot_general(v, p.astype(BF16), tn, preferred_element_type=F32)
        elif mode == "safe":
            m_prev = m_ref[:, cols]
            m_new = jnp.maximum(m_prev, m_cur)
            alpha = jnp.exp2(m_prev - m_new)
            p = jnp.exp2(s - m_new)
            l_ref[:, cols] = alpha * l_ref[:, cols] + jnp.sum(p, axis=0, keepdims=True)
            acc_ref[:, cols] = alpha * acc_ref[:, cols] + lax.dot_general(
                v, p.astype(BF16), tn, preferred_element_type=F32)
        else:
            m_prev = m_ref[:, cols]
            p = jnp.exp2(s - m_prev)
            m_new = jnp.maximum(m_prev, m_cur)
            alpha = jnp.exp2(m_prev - m_new)
            l_ref[:, cols] = alpha * (l_ref[:, cols] + jnp.sum(p, axis=0, keepdims=True))
            acc_ref[:, cols] = alpha * (acc_ref[:, cols] + lax.dot_general(
                v, p.astype(BF16), tn, preferred_element_type=F32))
        m_ref[:, cols] = m_new

    q = q_ref[...].astype(F32)
    if diff:
        lane = lax.broadcasted_iota(jnp.int32, q.shape, 1)
        qt_ref[:, 0:tq] = jnp.where(lane < HEAD_DIM // 2, q, 0.0).T.astype(BF16)
        qt_ref[:, tq:2 * tq] = jnp.where(lane >= HEAD_DIM // 2, q, 0.0).T.astype(BF16)
    else:
        for g in range(G):
            qt_ref[:, g * tq:(g + 1) * tq] = q[:, g * HEAD_DIM:(g + 1) * HEAD_DIM].T.astype(BF16)

    kc = kc_ref[...]
    vc = vc_ref[...].astype(BF16)
    s = scores(kc, starts[0])
    for n, c0 in enumerate(starts):
        if n + 1 < len(starts):
            s_next = scores(kc, starts[n + 1])
        elif n_kv:
            s_next = scores(kx_ref[0:tk, :], starts[0])
        else:
            s_next = None
        update(s, vc, c0, "first")
        s = s_next

    if n_kv:
        s_ref[...] = s

        def block_two_pass(jb, carry):
            r0 = pl.multiple_of(jb * tk, tk)
            r1 = pl.multiple_of(jnp.minimum(jb + 1, n_kv - 1) * tk, tk)
            k = kx_ref[pl.ds(r0, tk), :]
            v = vx_ref[pl.ds(r0, tk), :].astype(BF16)
            s = s_ref[...]
            for n, c0 in enumerate(starts):
                if n + 1 < len(starts):
                    s_next = scores(k, starts[n + 1])
                else:
                    s_next = scores(kx_ref[pl.ds(r1, tk), :], starts[0])
                update(s, v, c0, "safe")
                s = s_next
            s_ref[...] = s
            return carry

        def block_one_pass(jb, carry):
            r0 = pl.multiple_of(jb * tk, tk)
            k = kx_ref[pl.ds(r0, tk), :]
            v = vx_ref[pl.ds(r0, tk), :].astype(BF16)
            for c0 in starts:
                update(scores(k, c0), v, c0, "fast")
            return carry

        @pl.when(fast_ref[0] > 0)
        def _():
            lax.fori_loop(0, n_kv, block_one_pass, 0)

        @pl.when(fast_ref[0] <= 0)
        def _():
            lax.fori_loop(0, n_kv, block_two_pass, 0)

    o = acc_ref[...] * (1.0 / l_ref[...])
    if diff:
        d = o[:, 0:tq] - lam_ref[0] * o[:, tq:2 * tq]
        d = d * lax.rsqrt(jnp.mean(d * d, axis=0, keepdims=True) + EPS)
        o_ref[...] = (d * gain_ref[...] * post_scale).T.astype(o_ref.dtype)
    else:
        for g in range(G):
            o_ref[:, g * HEAD_DIM:(g + 1) * HEAD_DIM] = o[:, g * tq:(g + 1) * tq].T.astype(o_ref.dtype)


def _attention(q, kc, vc, kx, vx, *, G, v_slot, score_bound, diff=False, lam=None, gain=None, post_scale=1.0,
               tq=2048, tk=2048):
    B, T, _ = q.shape
    L = kc.shape[1]
    H = kc.shape[2] // HEAD_DIM
    gq = 1 if diff else G
    n_rows = 2 if diff else G
    tq = _pick(T, tq, 128)
    if lam is None:
        lam = jnp.zeros((1,), F32)
    if gain is None:
        gain = jnp.ones((HEAD_DIM,), F32)
    in_specs = [pl.BlockSpec((None, tq, gq * HEAD_DIM), lambda b, h, i: (b, i, h)),
                pl.BlockSpec((None, L, HEAD_DIM), lambda b, h, i: (b, 0, h)),
                pl.BlockSpec((None, L, HEAD_DIM), lambda b, h, i: (b, 0, v_slot + h))]
    args = [q, kc, vc]
    scratch = [pltpu.VMEM((HEAD_DIM, n_rows * tq), BF16),
               pltpu.VMEM((1, n_rows * tq), F32),
               pltpu.VMEM((1, n_rows * tq), F32),
               pltpu.VMEM((HEAD_DIM, n_rows * tq), F32)]
    if kx is not None:
        S = kx.shape[1]
        tk = _pick(S, tk, 128)
        n_kv = S // tk
        in_specs += [pl.BlockSpec((None, S, HEAD_DIM), lambda b, h, i: (b, 0, h)),
                     pl.BlockSpec((None, S, HEAD_DIM), lambda b, h, i: (b, 0, v_slot + h))]
        args += [kx, vx]
        scratch += [pltpu.VMEM((tk, min(ATTN_PANEL, n_rows * tq)), F32)]
    else:
        n_kv = 0
    fast = (2.0 * score_bound <= ATTN_FAST_RANGE).astype(jnp.int32).reshape(1)
    in_specs += [pl.BlockSpec(memory_space=pltpu.SMEM), pl.BlockSpec(memory_space=pltpu.SMEM),
                 pl.BlockSpec((HEAD_DIM, 1), lambda b, h, i: (0, 0))]
    args += [lam.reshape(1).astype(F32), fast, gain.reshape(HEAD_DIM, 1).astype(F32)]
    return pl.pallas_call(
        functools.partial(_attn_kernel, G=G, tq=tq, tk=tk, n_kv=n_kv, diff=diff, post_scale=post_scale),
        grid=(B, H, T // tq),
        in_specs=in_specs,
        out_specs=pl.BlockSpec((None, tq, gq * HEAD_DIM), lambda b, h, i: (b, i, h)),
        out_shape=jax.ShapeDtypeStruct((B, T, H * gq * HEAD_DIM), BF16),
        scratch_shapes=scratch,
        compiler_params=_params("parallel", "parallel", "arbitrary"),
        name="attn_diff" if diff else "attn_gqa",
    )(*args)


_HG_LEVELS = (32, 16, 8, 4, 2, 1)


def _hgrn_consts(reverse):
    C = HG_CHUNK
    mst = np.zeros((2 + len(_HG_LEVELS), C, C), np.float32)
    msk = np.zeros((len(_HG_LEVELS) + 1, C, C), np.float32)
    for t in range(C):
        if not reverse:
            mst[0, t, :t + 1] = 1.0
            mst[1, t, t + 1:] = 1.0
        else:
            mst[0, t, t:] = 1.0
            mst[1, t, :t] = 1.0
        msk[len(_HG_LEVELS), t, t] = 1.0
    for l, m in enumerate(_HG_LEVELS):
        for t in range(C):
            start = (t // (2 * m)) * 2 * m
            mid = start + m
            second = t >= mid
            if not reverse:
                if second:
                    mst[2 + l, t, mid:t + 1] = 1.0
                    msk[l, t, start:mid] = 1.0
                else:
                    mst[2 + l, t, t + 1:mid] = 1.0
            else:
                if not second:
                    mst[2 + l, t, t:mid] = 1.0
                    msk[l, t, mid:start + 2 * m] = 1.0
                else:
                    mst[2 + l, t, mid:t] = 1.0
    mst = mst.reshape(-1, C)
    return np.concatenate([mst, mst], axis=1), msk


def _hgrn_kernel(*refs, n_chunks, unroll, reverse, nh, final):
    q_ref, v_ref, z_ref, llb_ref, l1p_ref, s0_ref, mst_ref, msk_ref = refs[:8]
    if final:
        of_ref, gate_ref, hog_ref, o_ref, sf_ref, st_ref = refs[8:]
    else:
        o_ref, sf_ref, st_ref = refs[8:]
    C = HG_CHUNK
    tb = pl.program_id(2)
    ntb = pl.num_programs(2)

    @pl.when(tb == 0)
    def _():
        st_ref[...] = s0_ref[...]

    mst = mst_ref[...]
    row = lax.broadcasted_iota(jnp.int32, (C, HEAD_DIM), 0)
    is_q = [((row // m) % 2) == (0 if reverse else 1) for m in _HG_LEVELS]
    pair = [msk_ref[l] > 0.5 for l in range(len(_HG_LEVELS) + 1)]
    nt = (((1,), (1,)), ((), ()))
    tn = (((0,), (0,)), ((), ()))

    def chunks(ci, carry):
        lanes = []
        for u in range(unroll):
            c = ci * unroll + u
            c = (n_chunks - 1 - c) if reverse else c
            rows = pl.ds(pl.multiple_of(c * C, C), C)
            lanes += [(rows, hh, slice(hh * HEAD_DIM, (hh + 1) * HEAD_DIM)) for hh in range(nh)]

        qs, vs, ks, gparts = [], [], [], []
        for rows, hh, cols in lanes:
            qr = q_ref[rows, cols]
            qs.append(qr * jax.nn.sigmoid(qr))
            vs.append(v_ref[rows, cols].astype(BF16))
            z = z_ref[rows, cols]
            t = l1p_ref[:, cols] + (jnp.minimum(z, 0.0) - jnp.log1p(jnp.exp(-jnp.abs(z))))
            llb = llb_ref[:, cols]
            g = jnp.maximum(llb, t) + jnp.log1p(jnp.exp(-jnp.abs(llb - t)))
            ks.append(1.0 - jnp.exp(g))
            g2 = g * LOG2E
            g_hi = g2.astype(BF16)
            gparts.append(jnp.concatenate([g_hi, (g2 - g_hi.astype(F32)).astype(BF16)], axis=0))

        sums = jnp.dot(mst, jnp.concatenate(gparts, axis=1), preferred_element_type=F32)

        q_ins, o_intra, incs, e_tots = [], [], [], []
        for n, (q, v, k) in enumerate(zip(qs, vs, ks)):
            e = jnp.exp2(sums[:, n * HEAD_DIM:(n + 1) * HEAD_DIM])
            a = jnp.where(pair[len(_HG_LEVELS)],
                          lax.dot_general(q.astype(BF16), k.astype(BF16), nt, preferred_element_type=F32), 0.0)
            for l in range(len(_HG_LEVELS)):
                x = (jnp.where(is_q[l], q, k) * e[(2 + l) * C:(3 + l) * C]).astype(BF16)
                a = jnp.where(pair[l], lax.dot_general(x, x, nt, preferred_element_type=F32), a)
            e_in = e[0:C]
            q_ins.append((q * e_in).astype(BF16))
            e_tots.append(e_in[0:1] if reverse else e_in[C - 1:C])
            o_intra.append(jnp.dot(a.astype(BF16), v, preferred_element_type=F32))
            incs.append(lax.dot_general(v, (k * e[C:2 * C]).astype(BF16), tn, preferred_element_type=F32))

        state = [st_ref[hh] for hh in range(nh)]
        st_in = []
        for (rows, hh, cols), e_tot, inc in zip(lanes, e_tots, incs):
            st_in.append(state[hh].astype(BF16))
            state[hh] = state[hh] * e_tot + inc
        for hh in range(nh):
            st_ref[hh] = state[hh]

        for (rows, hh, cols), q_in, st, o in zip(lanes, q_ins, st_in, o_intra):
            o = o + lax.dot_general(q_in, st, nt, preferred_element_type=F32)
            if final:
                o = o + of_ref[rows, cols]
                o = o * lax.rsqrt(jnp.mean(o * o, axis=-1, keepdims=True) + EPS) * hog_ref[...]
                gt = gate_ref[rows, cols]
                o = o * (gt * jax.nn.sigmoid(gt))
            o_ref[rows, cols] = o.astype(o_ref.dtype)
        return carry

    lax.fori_loop(0, n_chunks // unroll, chunks, 0)

    @pl.when(tb == ntb - 1)
    def _():
        sf_ref[...] = st_ref[...]


def _hgrn_scan(p, slots, llb, l1p, s0, *, reverse, o_fwd=None, o_gain=None, tb=1024):
    B, T, _ = p.shape
    W = llb.shape[1]
    H = W // HEAD_DIM
    final = o_fwd is not None
    nh = 2 if (H % 2 == 0 and all(s % 2 == 0 for s in slots)) else 1
    wb = nh * HEAD_DIM
    tb = _pick(T, tb, HG_CHUNK)
    n_chunks = tb // HG_CHUNK
    ntb = T // tb
    mst, msk = _hgrn_consts(reverse)

    def seq_spec(slot):
        s = slot // nh
        if reverse:
            return pl.BlockSpec((None, tb, wb), lambda b, h, t: (b, ntb - 1 - t, s + h))
        return pl.BlockSpec((None, tb, wb), lambda b, h, t: (b, t, s + h))

    vec_spec = pl.BlockSpec((1, wb), lambda b, h, t: (0, h))
    st_spec = pl.BlockSpec((None, nh, HEAD_DIM, HEAD_DIM), lambda b, h, t: (b, h, 0, 0))
    in_specs = [seq_spec(slots[0]), seq_spec(slots[1]), seq_spec(slots[2]), vec_spec, vec_spec, st_spec,
                pl.BlockSpec(mst.shape, lambda b, h, t: (0, 0)),
                pl.BlockSpec(msk.shape, lambda b, h, t: (0, 0, 0))]
    args = [p, p, p, llb, l1p, s0, jnp.asarray(mst, BF16), jnp.asarray(msk, F32)]
    if final:
        in_specs += [seq_spec(0), seq_spec(slots[3]), pl.BlockSpec((1, HEAD_DIM), lambda b, h, t: (0, 0))]
        args += [o_fwd, p, o_gain.reshape(1, HEAD_DIM).astype(F32)]
    return pl.pallas_call(
        functools.partial(_hgrn_kernel, n_chunks=n_chunks, unroll=math.gcd(n_chunks, HG_UNROLL), reverse=reverse,
                          nh=nh, final=final),
        grid=(B, H // nh, ntb),
        in_specs=in_specs,
        out_specs=[seq_spec(0), st_spec],
        out_shape=[jax.ShapeDtypeStruct((B, T, W), BF16 if final else F32),
                   jax.ShapeDtypeStruct((B, H, HEAD_DIM, HEAD_DIM), F32)],
        scratch_shapes=[pltpu.VMEM((nh, HEAD_DIM, HEAD_DIM), F32)],
        compiler_params=_params("parallel", "parallel", "arbitrary"),
        name="hgrn_bwd" if reverse else "hgrn_fwd",
    )(*args)


def _prep_kernel(x_ref, gain_ref, cos_ref, sin_ref, o_ref, *, ns, halves, rope, scale):
    gain = gain_ref[...]
    lane = lax.broadcasted_iota(jnp.int32, (x_ref.shape[0], HEAD_DIM), 1)
    low = lane < HEAD_DIM // 2
    even = (lane % 2) == 0
    for s in range(ns):
        cols = slice(s * HEAD_DIM, (s + 1) * HEAD_DIM)
        x = x_ref[:, cols]
        xx = x * x
        if halves:
            m0 = jnp.sum(jnp.where(low, xx, 0.0), axis=-1, keepdims=True) * (2.0 / HEAD_DIM)
            m1 = jnp.sum(jnp.where(low, 0.0, xx), axis=-1, keepdims=True) * (2.0 / HEAD_DIM)
            inv = jnp.where(low, lax.rsqrt(m0 + EPS), lax.rsqrt(m1 + EPS))
        else:
            inv = lax.rsqrt(jnp.mean(xx, axis=-1, keepdims=True) + EPS)
        y = x * inv * gain
        if rope:
            partner = jnp.where(even, pltpu.roll(y, HEAD_DIM - 1, 1), pltpu.roll(y, 1, 1))
            y = y * cos_ref[...] + partner * sin_ref[...]
        o_ref[:, cols] = (y * scale).astype(o_ref.dtype)


def _prep(p, slot, n_slots, gain, tables, *, halves, scale, seq_len, tm=1024):
    M = p.shape[0]
    ns = math.gcd(slot, n_slots)
    tm = _pick(seq_len, tm, 8)
    tiles_per_seq = seq_len // tm
    rope = tables is not None
    if rope:
        cos, sin = tables
        tab_spec = pl.BlockSpec((tm, HEAD_DIM), lambda i, j: (i % tiles_per_seq, 0))
    else:
        cos = sin = jnp.zeros((8, HEAD_DIM), F32)
        tab_spec = pl.BlockSpec((8, HEAD_DIM), lambda i, j: (0, 0))
    return pl.pallas_call(
        functools.partial(_prep_kernel, ns=ns, halves=halves, rope=rope, scale=scale),
        grid=(M // tm, n_slots // ns),
        in_specs=[pl.BlockSpec((tm, ns * HEAD_DIM), lambda i, j: (i, slot // ns + j)),
                  pl.BlockSpec((1, HEAD_DIM), lambda i, j: (0, 0)), tab_spec, tab_spec],
        out_specs=pl.BlockSpec((tm, ns * HEAD_DIM), lambda i, j: (i, j)),
        out_shape=jax.ShapeDtypeStruct((M, n_slots * HEAD_DIM), BF16),
        compiler_params=_params("parallel", "parallel"),
        name="prep_qk",
    )(p, gain.reshape(1, HEAD_DIM).astype(F32), cos, sin)


def _rope_tables(seq_len, dim):
    rows = seq_len // GRID_W
    row = jnp.repeat(jnp.arange(rows, dtype=F32), GRID_W)
    col = jnp.tile(jnp.arange(GRID_W, dtype=F32), rows)
    half = dim // 2
    inv = ROPE_THETA ** (-jnp.arange(0, half, 2, dtype=F32) / half)
    ang = jnp.concatenate([row[:, None] * inv, col[:, None] * inv], axis=-1)
    cos = jnp.repeat(jnp.cos(ang), 2, axis=-1)
    sin = jnp.repeat(jnp.sin(ang), 2, axis=-1) * jnp.tile(jnp.asarray([-1.0, 1.0], F32), half)
    reps = HEAD_DIM // dim
    return jnp.tile(cos, (1, reps)), jnp.tile(sin, (1, reps))


def _dims(D):
    n = D // HEAD_DIM
    a_heads = 3 * n // 8
    a_kv = a_heads // 3
    b_heads = (n - a_heads) // 2
    c_heads = n - a_heads - b_heads
    return a_heads, a_kv, b_heads, c_heads


def _layer(x2, y2, B, layer, mod, rope_a, rope_c, lb, lam_init, last,
           w_in, w_out, aq_g, ak_g, ho_g, dq_g, dk_g, d_lam, do_g, w_gate, w_up, w_down):
    D = x2.shape[1]
    T = x2.shape[0] // B
    L = y2.shape[0] // B
    a_heads, a_kv, b_heads, c_heads = _dims(D)
    a_group = a_heads // a_kv
    half = HEAD_DIM // 2
    widths = (a_heads, c_heads, b_heads, b_heads, a_kv, a_kv, c_heads, c_heads, b_heads, b_heads, b_heads)
    (s_aq, s_cq, s_bq, s_bg, s_ak, s_av, s_ck, s_cv, s_bi, s_bff, s_bfb) = (
        np.concatenate([[0], np.cumsum(widths)[:-1]]).tolist())

    mx = [m[:, None, :] for m in jnp.split(mod[0:B], 6, axis=-1)]
    my = [m[:, None, :] for m in jnp.split(mod[B:B + 1], 6, axis=-1)]

    px = _mm(_norm_mod(x2, mx[0], mx[1]), w_in, layer, F32)
    py = _mm(_norm_mod(y2, my[0], my[1]), w_in, layer, F32)
    px3 = px.reshape(B, T, -1)
    py3 = py.reshape(B, L, -1)

    a_scale = HEAD_DIM ** -0.5 * LOG2E
    qa_x = _prep(px, s_aq, a_heads, aq_g, rope_a, halves=False, scale=a_scale, seq_len=T).reshape(B, T, -1)
    ka_x = _prep(px, s_ak, a_kv, ak_g, rope_a, halves=False, scale=1.0, seq_len=T).reshape(B, T, -1)
    ka_y = _prep(py, s_ak, a_kv, ak_g, None, halves=False, scale=1.0, seq_len=L).reshape(B, L, -1)
    a_bound = 1.02 * HEAD_DIM * a_scale * jnp.max(jnp.abs(aq_g)) * jnp.max(jnp.abs(ak_g))
    oa_x = _attention(qa_x, ka_y, py3, ka_x, px3, G=a_group, v_slot=s_av, score_bound=a_bound)

    c_scale = half ** -0.5 * LOG2E
    dl = d_lam.astype(F32)
    lam = jnp.exp(jnp.sum(dl[0] * dl[1])) - jnp.exp(jnp.sum(dl[2] * dl[3])) + lam_init
    dq_g2 = jnp.tile(dq_g, 2)
    dk_g2 = jnp.tile(dk_g, 2)
    qc_x = _prep(px, s_cq, c_heads, dq_g2, rope_c, halves=True, scale=c_scale, seq_len=T).reshape(B, T, -1)
    kc_x = _prep(px, s_ck, c_heads, dk_g2, rope_c, halves=True, scale=1.0, seq_len=T).reshape(B, T, -1)
    kc_y = _prep(py, s_ck, c_heads, dk_g2, None, halves=True, scale=1.0, seq_len=L).reshape(B, L, -1)
    c_bound = 1.02 * half * c_scale * jnp.max(jnp.abs(dq_g)) * jnp.max(jnp.abs(dk_g))
    oc_x = _attention(qc_x, kc_y, py3, kc_x, px3, G=2, v_slot=s_cv, score_bound=c_bound, diff=True, lam=lam,
                      gain=do_g, post_scale=1.0 - lam_init)

    llb = jnp.log(lb).reshape(2, 1, -1)
    l1p = jnp.log1p(-lb).reshape(2, 1, -1)
    s0 = jnp.zeros((B, b_heads, HEAD_DIM, HEAD_DIM), F32)
    f_slots = (s_bq, s_bi, s_bff, s_bg)
    b_slots = (s_bq, s_bi, s_bfb, s_bg)
    of_y, s_f = _hgrn_scan(py3, f_slots, llb[0], l1p[0], s0, reverse=False)
    ob_y, s_b = _hgrn_scan(py3, b_slots, llb[1], l1p[1], s0, reverse=True, o_fwd=of_y, o_gain=ho_g)
    of_x, _ = _hgrn_scan(px3, f_slots, llb[0], l1p[0], s_f, reverse=False)
    ob_x, _ = _hgrn_scan(px3, b_slots, llb[1], l1p[1], s_b, reverse=True, o_fwd=of_x, o_gain=ho_g)

    x2 = _mm_res([oa_x.reshape(B * T, -1), ob_x.reshape(B * T, -1), oc_x.reshape(B * T, -1)], w_out, x2, mx[2])
    x2 = _mm_res([_mm_swiglu(_norm_mod(x2, mx[3], mx[4]), w_gate, w_up, layer)], [w_down], x2, mx[5],
                 tm=FFN_DOWN_TILES[0], tn=FFN_DOWN_TILES[1], tk_max=w_down.shape[0])
    if last:
        return x2, None

    qa_y = _prep(py, s_aq, a_heads, aq_g, None, halves=False, scale=a_scale, seq_len=L).reshape(B, L, -1)
    oa_y = _attention(qa_y, ka_y, py3, None, None, G=a_group, v_slot=s_av, score_bound=a_bound)
    qc_y = _prep(py, s_cq, c_heads, dq_g2, None, halves=True, scale=c_scale, seq_len=L).reshape(B, L, -1)
    oc_y = _attention(qc_y, kc_y, py3, None, None, G=2, v_slot=s_cv, score_bound=c_bound, diff=True, lam=lam,
                      gain=do_g, post_scale=1.0 - lam_init)
    y2 = _mm_res([oa_y.reshape(B * L, -1), ob_y.reshape(B * L, -1), oc_y.reshape(B * L, -1)], w_out, y2, my[2])
    y2 = _mm_res([_mm_swiglu(_norm_mod(y2, my[3], my[4]), w_gate, w_up, layer)], [w_down], y2, my[5],
                 tm=FFN_DOWN_TILES[0], tn=FFN_DOWN_TILES[1], tk_max=w_down.shape[0])
    return x2, y2


def kernel(x, c, ctx, c_ctx, w_ada, b_ada, w_in, w_out, attn_q_gain, attn_k_gain,
           hgrn_lb_logits, hgrn_o_gain, diff_q_gain, diff_k_gain, diff_lambda, diff_o_gain,
           w_gate, w_up, w_down):
    B, T, D = x.shape
    L = ctx.shape[1]
    depth = w_ada.shape[0]
    a_heads, _, b_heads, _ = _dims(D)
    rope_a = _rope_tables(T, HEAD_DIM)
    rope_c = _rope_tables(T, HEAD_DIM // 2)

    sm = jax.nn.softmax(hgrn_lb_logits.astype(F32), axis=0)
    lb = jnp.concatenate([jnp.zeros_like(sm[:1]), jnp.cumsum(sm[1:], axis=0)], axis=0)

    cs = jnp.concatenate([c, c_ctx[None, :]], axis=0)
    cs = jax.nn.silu(jnp.pad(cs, ((0, 8 - (B + 1)), (0, 0))))

    d_ff = w_gate.shape[-1]
    a_w = a_heads * HEAD_DIM
    b_w = b_heads * HEAD_DIM

    x2 = x.reshape(B * T, D)
    y2 = ctx.reshape(B * L, D)
    for l in range(depth):
        lam_init = 0.8 - 0.6 * math.exp(-0.3 * l)
        mod = _ada(cs, w_ada, b_ada.reshape(depth, 1, -1), l)
        wd = _cast_w(w_down, l, 0, d_ff, d_ff, D)
        wo = [_cast_w(w_out, l, r0, n, n, D) for r0, n in ((0, a_w), (a_w, b_w), (a_w + b_w, D - a_w - b_w))]
        x2, y2 = _layer(x2, y2, B, l, mod, rope_a, rope_c, lb[l], lam_init, l == depth - 1,
                        w_in, wo,
                        attn_q_gain[l], attn_k_gain[l], hgrn_o_gain[l], diff_q_gain[l], diff_k_gain[l],
                        diff_lambda[l], diff_o_gain[l], w_gate, w_up, wd)
    return x2.reshape(B, T, D)
```

```python
import functools
import math

import numpy as np
import jax
import jax.numpy as jnp
from jax import lax
from jax.experimental import pallas as pl
from jax.experimental.pallas import tpu as pltpu

F32 = jnp.float32
BF16 = jnp.bfloat16

HEAD_DIM = 128
GRID_W = 64
HG_CHUNK = 64
HG_UNROLL = 8
ROPE_THETA = 10000.0
EPS = 1e-6
FFN_UP_TILES = (2048, 256)
FFN_DOWN_TILES = (512, 512)
VMEM_LIMIT = 56 * 1024 * 1024


def _pick(dim, pref, align):
    t = min(pref, dim)
    t -= t % align
    while t >= align:
        if dim % t == 0:
            return t
        t -= align
    return dim


def _params(*sem):
    return pltpu.CompilerParams(dimension_semantics=sem, vmem_limit_bytes=VMEM_LIMIT)


def _mm_kernel(a_ref, w_ref, o_ref):
    o_ref[...] = jnp.dot(a_ref[...], w_ref[...].astype(BF16), preferred_element_type=F32).astype(o_ref.dtype)


def _mm(a, w, layer, out_dtype, tm=2048, tn=512):
    M, K = a.shape
    N = w.shape[2]
    tm = _pick(M, tm, 8)
    tn = _pick(N, tn, 128)
    return pl.pallas_call(
        _mm_kernel,
        grid=(M // tm, N // tn),
        in_specs=[pl.BlockSpec((tm, K), lambda i, j: (i, 0), pipeline_mode=pl.Buffered(1)),
                  pl.BlockSpec((None, K, tn), lambda i, j: (layer, 0, j))],
        out_specs=pl.BlockSpec((tm, tn), lambda i, j: (i, j)),
        out_shape=jax.ShapeDtypeStruct((M, N), out_dtype),
        compiler_params=_params("parallel", "arbitrary"),
        name="proj_in",
    )(a, w)


def _mm_res_kernel(*refs, nk, n_in):
    a_refs, w_refs = refs[:n_in], refs[n_in:2 * n_in]
    r_ref, g_ref, o_ref = refs[2 * n_in:2 * n_in + 3]
    d = jnp.dot(a_refs[0][...], w_refs[0][...], preferred_element_type=F32)
    for a_ref, w_ref in zip(a_refs[1:], w_refs[1:]):
        d = d + jnp.dot(a_ref[...], w_ref[...], preferred_element_type=F32)
    if nk == 1:
        o_ref[...] = r_ref[...] + g_ref[...] * d
        return
    acc_ref, = refs[2 * n_in + 3:]
    k = pl.program_id(2)

    @pl.when(k == 0)
    def _():
        acc_ref[...] = d

    @pl.when(k > 0)
    def _():
        acc_ref[...] += d

    @pl.when(k == nk - 1)
    def _():
        o_ref[...] = r_ref[...] + g_ref[...] * acc_ref[...]


def _mm_res(a_list, w_list, res, gate, tm=1024, tn=1024, tk_max=4096):
    M = a_list[0].shape[0]
    N = w_list[0].shape[1]
    G = gate.shape[0]
    n_in = len(a_list)
    tm = _pick(M // G, tm, 8)
    tn = _pick(N, tn, 128)
    K = a_list[0].shape[1]
    tk = K if (n_in > 1 or K <= tk_max) else _pick(K, tk_max, 256)
    nk = K // tk
    tiles_per_group = (M // G) // tm
    scratch = [] if nk == 1 else [pltpu.VMEM((tm, tn), F32)]
    if n_in == 1:
        a_specs = [pl.BlockSpec((tm, tk), lambda i, j, k: (i, k))]
        w_specs = [pl.BlockSpec((tk, tn), lambda i, j, k: (k, j))]
    else:
        a_specs = [pl.BlockSpec((tm, a.shape[1]), lambda i, j, k: (i, 0)) for a in a_list]
        w_specs = [pl.BlockSpec((w.shape[0], tn), lambda i, j, k: (0, j)) for w in w_list]
    return pl.pallas_call(
        functools.partial(_mm_res_kernel, nk=nk, n_in=n_in),
        grid=(M // tm, N // tn, nk),
        in_specs=a_specs + w_specs + [
            pl.BlockSpec((tm, tn), lambda i, j, k: (i, j)),
            pl.BlockSpec((None, 1, tn), lambda i, j, k: (i // tiles_per_group, 0, j))],
        out_specs=pl.BlockSpec((tm, tn), lambda i, j, k: (i, j)),
        out_shape=jax.ShapeDtypeStruct((M, N), F32),
        scratch_shapes=scratch,
        compiler_params=_params("parallel", "arbitrary", "arbitrary"),
        name="proj_res",
    )(*a_list, *w_list, res, gate)


def _mm_swiglu_kernel(a_ref, wg_ref, wu_ref, o_ref):
    a = a_ref[...]
    g = jnp.dot(a, wg_ref[...].astype(BF16), preferred_element_type=F32)
    u = jnp.dot(a, wu_ref[...].astype(BF16), preferred_element_type=F32)
    o_ref[...] = (g * jax.nn.sigmoid(g) * u).astype(o_ref.dtype)


def _mm_swiglu(a, wg, wu, layer, tm=FFN_UP_TILES[0], tn=FFN_UP_TILES[1]):
    M, K = a.shape
    N = wg.shape[2]
    tm = _pick(M, tm, 8)
    tn = _pick(N, tn, 128)
    return pl.pallas_call(
        _mm_swiglu_kernel,
        grid=(M // tm, N // tn),
        in_specs=[pl.BlockSpec((tm, K), lambda i, j: (i, 0), pipeline_mode=pl.Buffered(1)),
                  pl.BlockSpec((None, K, tn), lambda i, j: (layer, 0, j)),
                  pl.BlockSpec((None, K, tn), lambda i, j: (layer, 0, j))],
        out_specs=pl.BlockSpec((tm, tn), lambda i, j: (i, j)),
        out_shape=jax.ShapeDtypeStruct((M, N), BF16),
        compiler_params=_params("parallel", "arbitrary"),
        name="ffn_up",
    )(a, wg, wu)


def _ada_kernel(c_ref, w_ref, b_ref, o_ref):
    o_ref[...] = jnp.dot(c_ref[...], w_ref[...], preferred_element_type=F32,
                         precision=lax.Precision.HIGHEST) + b_ref[...]


def _ada(cs, w, b, layer, tn=1024):
    R, D = cs.shape
    N = w.shape[2]
    tn = _pick(N, tn, 128)
    return pl.pallas_call(
        _ada_kernel,
        grid=(N // tn,),
        in_specs=[pl.BlockSpec((R, D), lambda j: (0, 0)),
                  pl.BlockSpec((None, D, tn), lambda j: (layer, 0, j)),
                  pl.BlockSpec((None, 1, tn), lambda j: (layer, 0, j))],
        out_specs=pl.BlockSpec((R, tn), lambda j: (0, j)),
        out_shape=jax.ShapeDtypeStruct((R, N), F32),
        compiler_params=_params("arbitrary"),
        name="adaln",
    )(cs, w, b)


def _cast_kernel(w_ref, o_ref, *, rows_valid, cols_valid):
    tr, tc = o_ref.shape
    r = pl.program_id(0) * tr + lax.broadcasted_iota(jnp.int32, (tr, tc), 0)
    c = pl.program_id(1) * tc + lax.broadcasted_iota(jnp.int32, (tr, tc), 1)
    o_ref[...] = jnp.where((r < rows_valid) & (c < cols_valid), w_ref[...], 0.0).astype(o_ref.dtype)


def _cast_w(w, layer, row0, rows, rows_out, cols_out, tc=2048):
    cols = w.shape[2]
    tr = math.gcd(math.gcd(row0, rows_out), 2048)
    tc = _pick(cols_out, tc, 128)
    assert (rows_out - tr) < rows and (cols_out - tc) < cols and row0 + rows <= w.shape[1]
    return pl.pallas_call(
        functools.partial(_cast_kernel, rows_valid=rows, cols_valid=cols),
        grid=(rows_out // tr, cols_out // tc),
        in_specs=[pl.BlockSpec((None, tr, tc), lambda i, j: (layer, row0 // tr + i, j))],
        out_specs=pl.BlockSpec((tr, tc), lambda i, j: (i, j)),
        out_shape=jax.ShapeDtypeStruct((rows_out, cols_out), BF16),
        compiler_params=_params("parallel", "parallel"),
        name="cast_w",
    )(w)


def _norm_mod_kernel(x_ref, sh_ref, sc_ref, o_ref):
    x = x_ref[...]
    y = x * lax.rsqrt(jnp.mean(x * x, axis=-1, keepdims=True) + EPS)
    o_ref[...] = (y * (1.0 + sc_ref[...]) + sh_ref[...]).astype(o_ref.dtype)


def _norm_mod(x, shift, scale, tm=512):
    M, D = x.shape
    G = shift.shape[0]
    tm = _pick(M // G, tm, 8)
    tiles_per_group = (M // G) // tm
    mod_spec = pl.BlockSpec((None, 1, D), lambda i: (i // tiles_per_group, 0, 0))
    return pl.pallas_call(
        _norm_mod_kernel,
        grid=(M // tm,),
        in_specs=[pl.BlockSpec((tm, D), lambda i: (i, 0)), mod_spec, mod_spec],
        out_specs=pl.BlockSpec((tm, D), lambda i: (i, 0)),
        out_shape=jax.ShapeDtypeStruct((M, D), BF16),
        compiler_params=_params("parallel"),
        name="norm_mod",
    )(x, shift, scale)


ATTN_PANEL = 1024
ATTN_FAST_RANGE = 64.0
LOG2E = math.log2(math.e)


def _attn_kernel(*refs, G, tq, tk, n_kv, diff, post_scale):
    if n_kv:
        (q_ref, kc_ref, vc_ref, kx_ref, vx_ref, lam_ref, fast_ref, gain_ref,
         o_ref, qt_ref, m_ref, l_ref, acc_ref, s_ref) = refs
    else:
        q_ref, kc_ref, vc_ref, lam_ref, fast_ref, gain_ref, o_ref, qt_ref, m_ref, l_ref, acc_ref = refs
    n_rows = 2 if diff else G
    width = n_rows * tq
    pw = min(ATTN_PANEL, width)
    starts = list(range(0, width, pw))
    tn = (((0,), (0,)), ((), ()))

    def scores(k, c0):
        return jnp.dot(k, qt_ref[:, c0:c0 + pw], preferred_element_type=F32)

    def update(s, v, c0, mode):
        cols = slice(c0, c0 + pw)
        m_cur = jnp.max(s, axis=0, keepdims=True)
        if mode == "first":
            m_new = m_cur
            p = jnp.exp2(s - m_new)
            l_ref[:, cols] = jnp.sum(p, axis=0, keepdims=True)
            acc_ref[:, cols] = lax.dot_general(v, p.astype(BF16), tn, preferred_element_type=F32)
        elif mode == "safe":
            m_prev = m_ref[:, cols]
            m_new = jnp.maximum(m_prev, m_cur)
            alpha = jnp.exp2(m_prev - m_new)
            p = jnp.exp2(s - m_new)
            l_ref[:, cols] = alpha * l_ref[:, cols] + jnp.sum(p, axis=0, keepdims=True)
            acc_ref[:, cols] = alpha * acc_ref[:, cols] + lax.dot_general(
                v, p.astype(BF16), tn, preferred_element_type=F32)
        else:
            m_prev = m_ref[:, cols]
            p = jnp.exp2(s - m_prev)
            m_new = jnp.maximum(m_prev, m_cur)
            alpha = jnp.exp2(m_prev - m_new)
            l_ref[:, cols] = alpha * (l_ref[:, cols] + jnp.sum(p, axis=0, keepdims=True))
            acc_ref[:, cols] = alpha * (acc_ref[:, cols] + lax.dot_general(
                v, p.astype(BF16), tn, preferred_element_type=F32))
        m_ref[:, cols] = m_new

    q = q_ref[...].astype(F32)
    if diff:
        lane = lax.broadcasted_iota(jnp.int32, q.shape, 1)
        qt_ref[:, 0:tq] = jnp.where(lane < HEAD_DIM // 2, q, 0.0).T.astype(BF16)
        qt_ref[:, tq:2 * tq] = jnp.where(lane >= HEAD_DIM // 2, q, 0.0).T.astype(BF16)
    else:
        for g in range(G):
            qt_ref[:, g * tq:(g + 1) * tq] = q[:, g * HEAD_DIM:(g + 1) * HEAD_DIM].T.astype(BF16)

    kc = kc_ref[...]
    vc = vc_ref[...].astype(BF16)
    s = scores(kc, starts[0])
    for n, c0 in enumerate(starts):
        if n + 1 < len(starts):
            s_next = scores(kc, starts[n + 1])
        elif n_kv:
            s_next = scores(kx_ref[0:tk, :], starts[0])
        else:
            s_next = None
        update(s, vc, c0, "first")
        s = s_next

    if n_kv:
        s_ref[...] = s

        def block_two_pass(jb, carry):
            r0 = pl.multiple_of(jb * tk, tk)
            r1 = pl.multiple_of(jnp.minimum(jb + 1, n_kv - 1) * tk, tk)
            k = kx_ref[pl.ds(r0, tk), :]
            v = vx_ref[pl.ds(r0, tk), :].astype(BF16)
            s = s_ref[...]
            for n, c0 in enumerate(starts):
                if n + 1 < len(starts):
                    s_next = scores(k, starts[n + 1])
                else:
                    s_next = scores(kx_ref[pl.ds(r1, tk), :], starts[0])
                update(s, v, c0, "safe")
                s = s_next
            s_ref[...] = s
            return carry

        def block_one_pass(jb, carry):
            r0 = pl.multiple_of(jb * tk, tk)
            k = kx_ref[pl.ds(r0, tk), :]
            v = vx_ref[pl.ds(r0, tk), :].astype(BF16)
            for c0 in starts:
                update(scores(k, c0), v, c0, "fast")
            return carry

        @pl.when(fast_ref[0] > 0)
        def _():
            lax.fori_loop(0, n_kv, block_one_pass, 0)

        @pl.when(fast_ref[0] <= 0)
        def _():
            lax.fori_loop(0, n_kv, block_two_pass, 0)

    o = acc_ref[...] * (1.0 / l_ref[...])
    if diff:
        d = o[:, 0:tq] - lam_ref[0] * o[:, tq:2 * tq]
        d = d * lax.rsqrt(jnp.mean(d * d, axis=0, keepdims=True) + EPS)
        o_ref[...] = (d * gain_ref[...] * post_scale).T.astype(o_ref.dtype)
    else:
        for g in range(G):
            o_ref[:, g * HEAD_DIM:(g + 1) * HEAD_DIM] = o[:, g * tq:(g + 1) * tq].T.astype(o_ref.dtype)


def _attention(q, kc, vc, kx, vx, *, G, v_slot, score_bound, diff=False, lam=None, gain=None, post_scale=1.0,
               tq=2048, tk=2048):
    B, T, _ = q.shape
    L = kc.shape[1]
    H = kc.shape[2] // HEAD_DIM
    gq = 1 if diff else G
    n_rows = 2 if diff else G
    tq = _pick(T, tq, 128)
    if lam is None:
        lam = jnp.zeros((1,), F32)
    if gain is None:
        gain = jnp.ones((HEAD_DIM,), F32)
    in_specs = [pl.BlockSpec((None, tq, gq * HEAD_DIM), lambda b, h, i: (b, i, h)),
                pl.BlockSpec((None, L, HEAD_DIM), lambda b, h, i: (b, 0, h)),
                pl.BlockSpec((None, L, HEAD_DIM), lambda b, h, i: (b, 0, v_slot + h))]
    args = [q, kc, vc]
    scratch = [pltpu.VMEM((HEAD_DIM, n_rows * tq), BF16),
               pltpu.VMEM((1, n_rows * tq), F32),
               pltpu.VMEM((1, n_rows * tq), F32),
               pltpu.VMEM((HEAD_DIM, n_rows * tq), F32)]
    if kx is not None:
        S = kx.shape[1]
        tk = _pick(S, tk, 128)
        n_kv = S // tk
        in_specs += [pl.BlockSpec((None, S, HEAD_DIM), lambda b, h, i: (b, 0, h)),
                     pl.BlockSpec((None, S, HEAD_DIM), lambda b, h, i: (b, 0, v_slot + h))]
        args += [kx, vx]
        scratch += [pltpu.VMEM((tk, min(ATTN_PANEL, n_rows * tq)), F32)]
    else:
        n_kv = 0
    fast = (2.0 * score_bound <= ATTN_FAST_RANGE).astype(jnp.int32).reshape(1)
    in_specs += [pl.BlockSpec(memory_space=pltpu.SMEM), pl.BlockSpec(memory_space=pltpu.SMEM),
                 pl.BlockSpec((HEAD_DIM, 1), lambda b, h, i: (0, 0))]
    args += [lam.reshape(1).astype(F32), fast, gain.reshape(HEAD_DIM, 1).astype(F32)]
    return pl.pallas_call(
        functools.partial(_attn_kernel, G=G, tq=tq, tk=tk, n_kv=n_kv, diff=diff, post_scale=post_scale),
        grid=(B, H, T // tq),
        in_specs=in_specs,
        out_specs=pl.BlockSpec((None, tq, gq * HEAD_DIM), lambda b, h, i: (b, i, h)),
        out_shape=jax.ShapeDtypeStruct((B, T, H * gq * HEAD_DIM), BF16),
        scratch_shapes=scratch,
        compiler_params=_params("parallel", "parallel", "arbitrary"),
        name="attn_diff" if diff else "attn_gqa",
    )(*args)


_HG_LEVELS = (32, 16, 8, 4, 2, 1)


def _hgrn_consts(reverse):
    C = HG_CHUNK
    mst = np.zeros((2 + len(_HG_LEVELS), C, C), np.float32)
    msk = np.zeros((len(_HG_LEVELS) + 1, C, C), np.float32)
    for t in range(C):
        if not reverse:
            mst[0, t, :t + 1] = 1.0
            mst[1, t, t + 1:] = 1.0
        else:
            mst[0, t, t:] = 1.0
            mst[1, t, :t] = 1.0
        msk[len(_HG_LEVELS), t, t] = 1.0
    for l, m in enumerate(_HG_LEVELS):
        for t in range(C):
            start = (t // (2 * m)) * 2 * m
            mid = start + m
            second = t >= mid
            if not reverse:
                if second:
                    mst[2 + l, t, mid:t + 1] = 1.0
                    msk[l, t, start:mid] = 1.0
                else:
                    mst[2 + l, t, t + 1:mid] = 1.0
            else:
                if not second:
                    mst[2 + l, t, t:mid] = 1.0
                    msk[l, t, mid:start + 2 * m] = 1.0
                else:
                    mst[2 + l, t, mid:t] = 1.0
    mst = mst.reshape(-1, C)
    return np.concatenate([mst, mst], axis=1), msk


def _hgrn_kernel(*refs, n_chunks, unroll, reverse, nh, final):
    q_ref, v_ref, z_ref, llb_ref, l1p_ref, s0_ref, mst_ref, msk_ref = refs[:8]
    if final:
        of_ref, gate_ref, hog_ref, o_ref, sf_ref, st_ref = refs[8:]
    else:
        o_ref, sf_ref, st_ref = refs[8:]
    C = HG_CHUNK
    tb = pl.program_id(2)
    ntb = pl.num_programs(2)

    @pl.when(tb == 0)
    def _():
        st_ref[...] = s0_ref[...]

    mst = mst_ref[...]
    row = lax.broadcasted_iota(jnp.int32, (C, HEAD_DIM), 0)
    is_q = [((row // m) % 2) == (0 if reverse else 1) for m in _HG_LEVELS]
    pair = [msk_ref[l] > 0.5 for l in range(len(_HG_LEVELS) + 1)]
    nt = (((1,), (1,)), ((), ()))
    tn = (((0,), (0,)), ((), ()))

    def chunks(ci, carry):
        lanes = []
        for u in range(unroll):
            c = ci * unroll + u
            c = (n_chunks - 1 - c) if reverse else c
            rows = pl.ds(pl.multiple_of(c * C, C), C)
            lanes += [(rows, hh, slice(hh * HEAD_DIM, (hh + 1) * HEAD_DIM)) for hh in range(nh)]

        qs, vs, ks, gparts = [], [], [], []
        for rows, hh, cols in lanes:
            qr = q_ref[rows, cols]
            qs.append(qr * jax.nn.sigmoid(qr))
            vs.append(v_ref[rows, cols].astype(BF16))
            z = z_ref[rows, cols]
            t = l1p_ref[:, cols] + (jnp.minimum(z, 0.0) - jnp.log1p(jnp.exp(-jnp.abs(z))))
            llb = llb_ref[:, cols]
            g = jnp.maximum(llb, t) + jnp.log1p(jnp.exp(-jnp.abs(llb - t)))
            ks.append(1.0 - jnp.exp(g))
            g2 = g * LOG2E
            g_hi = g2.astype(BF16)
            gparts.append(jnp.concatenate([g_hi, (g2 - g_hi.astype(F32)).astype(BF16)], axis=0))

        sums = jnp.dot(mst, jnp.concatenate(gparts, axis=1), preferred_element_type=F32)

        q_ins, o_intra, incs, e_tots = [], [], [], []
        for n, (q, v, k) in enumerate(zip(qs, vs, ks)):
            e = jnp.exp2(sums[:, n * HEAD_DIM:(n + 1) * HEAD_DIM])
            a = jnp.where(pair[len(_HG_LEVELS)],
                          lax.dot_general(q.astype(BF16), k.astype(BF16), nt, preferred_element_type=F32), 0.0)
            for l in range(len(_HG_LEVELS)):
                x = (jnp.where(is_q[l], q, k) * e[(2 + l) * C:(3 + l) * C]).astype(BF16)
                a = jnp.where(pair[l], lax.dot_general(x, x, nt, preferred_element_type=F32), a)
            e_in = e[0:C]
            q_ins.append((q * e_in).astype(BF16))
            e_tots.append(e_in[0:1] if reverse else e_in[C - 1:C])
            o_intra.append(jnp.dot(a.astype(BF16), v, preferred_element_type=F32))
            incs.append(lax.dot_general(v, (k * e[C:2 * C]).astype(BF16), tn, preferred_element_type=F32))

        state = [st_ref[hh] for hh in range(nh)]
        st_in = []
        for (rows, hh, cols), e_tot, inc in zip(lanes, e_tots, incs):
            st_in.append(state[hh].astype(BF16))
            state[hh] = state[hh] * e_tot + inc
        for hh in range(nh):
            st_ref[hh] = state[hh]

        for (rows, hh, cols), q_in, st, o in zip(lanes, q_ins, st_in, o_intra):
            o = o + lax.dot_general(q_in, st, nt, preferred_element_type=F32)
            if final:
                o = o + of_ref[rows, cols]
                o = o * lax.rsqrt(jnp.mean(o * o, axis=-1, keepdims=True) + EPS) * hog_ref[...]
                gt = gate_ref[rows, cols]
                o = o * (gt * jax.nn.sigmoid(gt))
            o_ref[rows, cols] = o.astype(o_ref.dtype)
        return carry

    lax.fori_loop(0, n_chunks // unroll, chunks, 0)

    @pl.when(tb == ntb - 1)
    def _():
        sf_ref[...] = st_ref[...]


def _hgrn_scan(p, slots, llb, l1p, s0, *, reverse, o_fwd=None, o_gain=None, tb=1024):
    B, T, _ = p.shape
    W = llb.shape[1]
    H = W // HEAD_DIM
    final = o_fwd is not None
    nh = 2 if (H % 2 == 0 and all(s % 2 == 0 for s in slots)) else 1
    wb = nh * HEAD_DIM
    tb = _pick(T, tb, HG_CHUNK)
    n_chunks = tb // HG_CHUNK
    ntb = T // tb
    mst, msk = _hgrn_consts(reverse)

    def seq_spec(slot):
        s = slot // nh
        if reverse:
            return pl.BlockSpec((None, tb, wb), lambda b, h, t: (b, ntb - 1 - t, s + h))
        return pl.BlockSpec((None, tb, wb), lambda b, h, t: (b, t, s + h))

    vec_spec = pl.BlockSpec((1, wb), lambda b, h, t: (0, h))
    st_spec = pl.BlockSpec((None, nh, HEAD_DIM, HEAD_DIM), lambda b, h, t: (b, h, 0, 0))
    in_specs = [seq_spec(slots[0]), seq_spec(slots[1]), seq_spec(slots[2]), vec_spec, vec_spec, st_spec,
                pl.BlockSpec(mst.shape, lambda b, h, t: (0, 0)),
                pl.BlockSpec(msk.shape, lambda b, h, t: (0, 0, 0))]
    args = [p, p, p, llb, l1p, s0, jnp.asarray(mst, BF16), jnp.asarray(msk, F32)]
    if final:
        in_specs += [seq_spec(0), seq_spec(slots[3]), pl.BlockSpec((1, HEAD_DIM), lambda b, h, t: (0, 0))]
        args += [o_fwd, p, o_gain.reshape(1, HEAD_DIM).astype(F32)]
    return pl.pallas_call(
        functools.partial(_hgrn_kernel, n_chunks=n_chunks, unroll=math.gcd(n_chunks, HG_UNROLL), reverse=reverse,
                          nh=nh, final=final),
        grid=(B, H // nh, ntb),
        in_specs=in_specs,
        out_specs=[seq_spec(0), st_spec],
        out_shape=[jax.ShapeDtypeStruct((B, T, W), BF16 if final else F32),
                   jax.ShapeDtypeStruct((B, H, HEAD_DIM, HEAD_DIM), F32)],
        scratch_shapes=[pltpu.VMEM((nh, HEAD_DIM, HEAD_DIM), F32)],
        compiler_params=_params("parallel", "parallel", "arbitrary"),
        name="hgrn_bwd" if reverse else "hgrn_fwd",
    )(*args)


def _prep_kernel(x_ref, gain_ref, cos_ref, sin_ref, o_ref, *, ns, halves, rope, scale):
    gain = gain_ref[...]
    lane = lax.broadcasted_iota(jnp.int32, (x_ref.shape[0], HEAD_DIM), 1)
    low = lane < HEAD_DIM // 2
    even = (lane % 2) == 0
    for s in range(ns):
        cols = slice(s * HEAD_DIM, (s + 1) * HEAD_DIM)
        x = x_ref[:, cols]
        xx = x * x
        if halves:
            m0 = jnp.sum(jnp.where(low, xx, 0.0), axis=-1, keepdims=True) * (2.0 / HEAD_DIM)
            m1 = jnp.sum(jnp.where(low, 0.0, xx), axis=-1, keepdims=True) * (2.0 / HEAD_DIM)
            inv = jnp.where(low, lax.rsqrt(m0 + EPS), lax.rsqrt(m1 + EPS))
        else:
            inv = lax.rsqrt(jnp.mean(xx, axis=-1, keepdims=True) + EPS)
        y = x * inv * gain
        if rope:
            partner = jnp.where(even, pltpu.roll(y, HEAD_DIM - 1, 1), pltpu.roll(y, 1, 1))
            y = y * cos_ref[...] + partner * sin_ref[...]
        o_ref[:, cols] = (y * scale).astype(o_ref.dtype)


def _prep(p, slot, n_slots, gain, tables, *, halves, scale, seq_len, tm=1024):
    M = p.shape[0]
    ns = math.gcd(slot, n_slots)
    tm = _pick(seq_len, tm, 8)
    tiles_per_seq = seq_len // tm
    rope = tables is not None
    if rope:
        cos, sin = tables
        tab_spec = pl.BlockSpec((tm, HEAD_DIM), lambda i, j: (i % tiles_per_seq, 0))
    else:
        cos = sin = jnp.zeros((8, HEAD_DIM), F32)
        tab_spec = pl.BlockSpec((8, HEAD_DIM), lambda i, j: (0, 0))
    return pl.pallas_call(
        functools.partial(_prep_kernel, ns=ns, halves=halves, rope=rope, scale=scale),
        grid=(M // tm, n_slots // ns),
        in_specs=[pl.BlockSpec((tm, ns * HEAD_DIM), lambda i, j: (i, slot // ns + j)),
                  pl.BlockSpec((1, HEAD_DIM), lambda i, j: (0, 0)), tab_spec, tab_spec],
        out_specs=pl.BlockSpec((tm, ns * HEAD_DIM), lambda i, j: (i, j)),
        out_shape=jax.ShapeDtypeStruct((M, n_slots * HEAD_DIM), BF16),
        compiler_params=_params("parallel", "parallel"),
        name="prep_qk",
    )(p, gain.reshape(1, HEAD_DIM).astype(F32), cos, sin)


def _rope_tables(seq_len, dim):
    rows = seq_len // GRID_W
    row = jnp.repeat(jnp.arange(rows, dtype=F32), GRID_W)
    col = jnp.tile(jnp.arange(GRID_W, dtype=F32), rows)
    half = dim // 2
    inv = ROPE_THETA ** (-jnp.arange(0, half, 2, dtype=F32) / half)
    ang = jnp.concatenate([row[:, None] * inv, col[:, None] * inv], axis=-1)
    cos = jnp.repeat(jnp.cos(ang), 2, axis=-1)
    sin = jnp.repeat(jnp.sin(ang), 2, axis=-1) * jnp.tile(jnp.asarray([-1.0, 1.0], F32), half)
    reps = HEAD_DIM // dim
    return jnp.tile(cos, (1, reps)), jnp.tile(sin, (1, reps))


def _dims(D):
    n = D // HEAD_DIM
    a_heads = 3 * n // 8
    a_kv = a_heads // 3
    b_heads = (n - a_heads) // 2
    c_heads = n - a_heads - b_heads
    return a_heads, a_kv, b_heads, c_heads


def _layer(x2, y2, B, layer, mod, rope_a, rope_c, lb, lam_init, last,
           w_in, w_out, aq_g, ak_g, ho_g, dq_g, dk_g, d_lam, do_g, w_gate, w_up, w_down):
    D = x2.shape[1]
    T = x2.shape[0] // B
    L = y2.shape[0] // B
    a_heads, a_kv, b_heads, c_heads = _dims(D)
    a_group = a_heads // a_kv
    half = HEAD_DIM // 2
    widths = (a_heads, c_heads, b_heads, b_heads, a_kv, a_kv, c_heads, c_heads, b_heads, b_heads, b_heads)
    (s_aq, s_cq, s_bq, s_bg, s_ak, s_av, s_ck, s_cv, s_bi, s_bff, s_bfb) = (
        np.concatenate([[0], np.cumsum(widths)[:-1]]).tolist())

    mx = [m[:, None, :] for m in jnp.split(mod[0:B], 6, axis=-1)]
    my = [m[:, None, :] for m in jnp.split(mod[B:B + 1], 6, axis=-1)]

    px = _mm(_norm_mod(x2, mx[0], mx[1]), w_in, layer, F32)
    py = _mm(_norm_mod(y2, my[0], my[1]), w_in, layer, F32)
    px3 = px.reshape(B, T, -1)
    py3 = py.reshape(B, L, -1)

    a_scale = HEAD_DIM ** -0.5 * LOG2E
    qa_x = _prep(px, s_aq, a_heads, aq_g, rope_a, halves=False, scale=a_scale, seq_len=T).reshape(B, T, -1)
    ka_x = _prep(px, s_ak, a_kv, ak_g, rope_a, halves=False, scale=1.0, seq_len=T).reshape(B, T, -1)
    ka_y = _prep(py, s_ak, a_kv, ak_g, None, halves=False, scale=1.0, seq_len=L).reshape(B, L, -1)
    a_bound = 1.02 * HEAD_DIM * a_scale * jnp.max(jnp.abs(aq_g)) * jnp.max(jnp.abs(ak_g))
    oa_x = _attention(qa_x, ka_y, py3, ka_x, px3, G=a_group, v_slot=s_av, score_bound=a_bound)

    c_scale = half ** -0.5 * LOG2E
    dl = d_lam.astype(F32)
    lam = jnp.exp(jnp.sum(dl[0] * dl[1])) - jnp.exp(jnp.sum(dl[2] * dl[3])) + lam_init
    dq_g2 = jnp.tile(dq_g, 2)
    dk_g2 = jnp.tile(dk_g, 2)
    qc_x = _prep(px, s_cq, c_heads, dq_g2, rope_c, halves=True, scale=c_scale, seq_len=T).reshape(B, T, -1)
    kc_x = _prep(px, s_ck, c_heads, dk_g2, rope_c, halves=True, scale=1.0, seq_len=T).reshape(B, T, -1)
    kc_y = _prep(py, s_ck, c_heads, dk_g2, None, halves=True, scale=1.0, seq_len=L).reshape(B, L, -1)
    c_bound = 1.02 * half * c_scale * jnp.max(jnp.abs(dq_g)) * jnp.max(jnp.abs(dk_g))
    oc_x = _attention(qc_x, kc_y, py3, kc_x, px3, G=2, v_slot=s_cv, score_bound=c_bound, diff=True, lam=lam,
                      gain=do_g, post_scale=1.0 - lam_init)

    llb = jnp.log(lb).reshape(2, 1, -1)
    l1p = jnp.log1p(-lb).reshape(2, 1, -1)
    s0 = jnp.zeros((B, b_heads, HEAD_DIM, HEAD_DIM), F32)
    f_slots = (s_bq, s_bi, s_bff, s_bg)
    b_slots = (s_bq, s_bi, s_bfb, s_bg)
    of_y, s_f = _hgrn_scan(py3, f_slots, llb[0], l1p[0], s0, reverse=False)
    ob_y, s_b = _hgrn_scan(py3, b_slots, llb[1], l1p[1], s0, reverse=True, o_fwd=of_y, o_gain=ho_g)
    of_x, _ = _hgrn_scan(px3, f_slots, llb[0], l1p[0], s_f, reverse=False)
    ob_x, _ = _hgrn_scan(px3, b_slots, llb[1], l1p[1], s_b, reverse=True, o_fwd=of_x, o_gain=ho_g)

    x2 = _mm_res([oa_x.reshape(B * T, -1), ob_x.reshape(B * T, -1), oc_x.reshape(B * T, -1)], w_out, x2, mx[2])
    x2 = _mm_res([_mm_swiglu(_norm_mod(x2, mx[3], mx[4]), w_gate, w_up, layer)], [w_down], x2, mx[5],
                 tm=FFN_DOWN_TILES[0], tn=FFN_DOWN_TILES[1], tk_max=w_down.shape[0])
    if last:
        return x2, None

    qa_y = _prep(py, s_aq, a_heads, aq_g, None, halves=False, scale=a_scale, seq_len=L).reshape(B, L, -1)
    oa_y = _attention(qa_y, ka_y, py3, None, None, G=a_group, v_slot=s_av, score_bound=a_bound)
    qc_y = _prep(py, s_cq, c_heads, dq_g2, None, halves=True, scale=c_scale, seq_len=L).reshape(B, L, -1)
    oc_y = _attention(qc_y, kc_y, py3, None, None, G=2, v_slot=s_cv, score_bound=c_bound, diff=True, lam=lam,
                      gain=do_g, post_scale=1.0 - lam_init)
    y2 = _mm_res([oa_y.reshape(B * L, -1), ob_y.reshape(B * L, -1), oc_y.reshape(B * L, -1)], w_out, y2, my[2])
    y2 = _mm_res([_mm_swiglu(_norm_mod(y2, my[3], my[4]), w_gate, w_up, layer)], [w_down], y2, my[5],
                 tm=FFN_DOWN_TILES[0], tn=FFN_DOWN_TILES[1], tk_max=w_down.shape[0])
    return x2, y2


def kernel(x, c, ctx, c_ctx, w_ada, b_ada, w_in, w_out, attn_q_gain, attn_k_gain,
           hgrn_lb_logits, hgrn_o_gain, diff_q_gain, diff_k_gain, diff_lambda, diff_o_gain,
           w_gate, w_up, w_down):
    B, T, D = x.shape
    L = ctx.shape[1]
    depth = w_ada.shape[0]
    a_heads, _, b_heads, _ = _dims(D)
    rope_a = _rope_tables(T, HEAD_DIM)
    rope_c = _rope_tables(T, HEAD_DIM // 2)

    sm = jax.nn.softmax(hgrn_lb_logits.astype(F32), axis=0)
    lb = jnp.concatenate([jnp.zeros_like(sm[:1]), jnp.cumsum(sm[1:], axis=0)], axis=0)

    cs = jnp.concatenate([c, c_ctx[None, :]], axis=0)
    cs = jax.nn.silu(jnp.pad(cs, ((0, 8 - (B + 1)), (0, 0))))

    d_ff = w_gate.shape[-1]
    a_w = a_heads * HEAD_DIM
    b_w = b_heads * HEAD_DIM

    x2 = x.reshape(B * T, D)
    y2 = ctx.reshape(B * L, D)
    for l in range(depth):
        lam_init = 0.8 - 0.6 * math.exp(-0.3 * l)
        mod = _ada(cs, w_ada, b_ada.reshape(depth, 1, -1), l)
        wd = _cast_w(w_down, l, 0, d_ff, d_ff, D)
        wo = [_cast_w(w_out, l, r0, n, n, D) for r0, n in ((0, a_w), (a_w, b_w), (a_w + b_w, D - a_w - b_w))]
        x2, y2 = _layer(x2, y2, B, l, mod, rope_a, rope_c, lb[l], lam_init, l == depth - 1,
                        w_in, wo,
                        attn_q_gain[l], attn_k_gain[l], hgrn_o_gain[l], diff_q_gain[l], diff_k_gain[l],
                        diff_lambda[l], diff_o_gain[l], w_gate, w_up, wd)
    return x2.reshape(B, T, D)
```

```python
import functools
import math

import numpy as np
import jax
import jax.numpy as jnp
from jax import lax
from jax.experimental import pallas as pl
from jax.experimental.pallas import tpu as pltpu

F32 = jnp.float32
BF16 = jnp.bfloat16

HEAD_DIM = 128
GRID_W = 64
HG_CHUNK = 64
HG_UNROLL = 8
ROPE_THETA = 10000.0
EPS = 1e-6
FFN_UP_TILES = (2048, 256)
FFN_DOWN_TILES = (512, 512)
VMEM_LIMIT = 56 * 1024 * 1024


def _pick(dim, pref, align):
    t = min(pref, dim)
    t -= t % align
    while t >= align:
        if dim % t == 0:
            return t
        t -= align
    return dim


def _params(*sem):
    return pltpu.CompilerParams(dimension_semantics=sem, vmem_limit_bytes=VMEM_LIMIT)


def _mm_kernel(a_ref, w_ref, o_ref):
    o_ref[...] = jnp.dot(a_ref[...], w_ref[...].astype(BF16), preferred_element_type=F32).astype(o_ref.dtype)


def _mm(a, w, layer, out_dtype, tm=2048, tn=512):
    M, K = a.shape
    N = w.shape[2]
    tm = _pick(M, tm, 8)
    tn = _pick(N, tn, 128)
    return pl.pallas_call(
        _mm_kernel,
        grid=(M // tm, N // tn),
        in_specs=[pl.BlockSpec((tm, K), lambda i, j: (i, 0), pipeline_mode=pl.Buffered(1)),
                  pl.BlockSpec((None, K, tn), lambda i, j: (layer, 0, j))],
        out_specs=pl.BlockSpec((tm, tn), lambda i, j: (i, j)),
        out_shape=jax.ShapeDtypeStruct((M, N), out_dtype),
        compiler_params=_params("parallel", "arbitrary"),
        name="proj_in",
    )(a, w)


def _mm_res_kernel(*refs, nk, n_in):
    a_refs, w_refs = refs[:n_in], refs[n_in:2 * n_in]
    r_ref, g_ref, o_ref = refs[2 * n_in:2 * n_in + 3]
    d = jnp.dot(a_refs[0][...], w_refs[0][...], preferred_element_type=F32)
    for a_ref, w_ref in zip(a_refs[1:], w_refs[1:]):
        d = d + jnp.dot(a_ref[...], w_ref[...], preferred_element_type=F32)
    if nk == 1:
        o_ref[...] = r_ref[...] + g_ref[...] * d
        return
    acc_ref, = refs[2 * n_in + 3:]
    k = pl.program_id(2)

    @pl.when(k == 0)
    def _():
        acc_ref[...] = d

    @pl.when(k > 0)
    def _():
        acc_ref[...] += d

    @pl.when(k == nk - 1)
    def _():
        o_ref[...] = r_ref[...] + g_ref[...] * acc_ref[...]


def _mm_res(a_list, w_list, res, gate, tm=1024, tn=1024, tk_max=4096):
    M = a_list[0].shape[0]
    N = w_list[0].shape[1]
    G = gate.shape[0]
    n_in = len(a_list)
    tm = _pick(M // G, tm, 8)
    tn = _pick(N, tn, 128)
    K = a_list[0].shape[1]
    tk = K if (n_in > 1 or K <= tk_max) else _pick(K, tk_max, 256)
    nk = K // tk
    tiles_per_group = (M // G) // tm
    scratch = [] if nk == 1 else [pltpu.VMEM((tm, tn), F32)]
    if n_in == 1:
        a_specs = [pl.BlockSpec((tm, tk), lambda i, j, k: (i, k))]
        w_specs = [pl.BlockSpec((tk, tn), lambda i, j, k: (k, j))]
    else:
        a_specs = [pl.BlockSpec((tm, a.shape[1]), lambda i, j, k: (i, 0)) for a in a_list]
        w_specs = [pl.BlockSpec((w.shape[0], tn), lambda i, j, k: (0, j)) for w in w_list]
    return pl.pallas_call(
        functools.partial(_mm_res_kernel, nk=nk, n_in=n_in),
        grid=(M // tm, N // tn, nk),
        in_specs=a_specs + w_specs + [
            pl.BlockSpec((tm, tn), lambda i, j, k: (i, j)),
            pl.BlockSpec((None, 1, tn), lambda i, j, k: (i // tiles_per_group, 0, j))],
        out_specs=pl.BlockSpec((tm, tn), lambda i, j, k: (i, j)),
        out_shape=jax.ShapeDtypeStruct((M, N), F32),
        scratch_shapes=scratch,
        compiler_params=_params("parallel", "arbitrary", "arbitrary"),
        name="proj_res",
    )(*a_list, *w_list, res, gate)


def _mm_swiglu_kernel(a_ref, wg_ref, wu_ref, o_ref):
    a = a_ref[...]
    g = jnp.dot(a, wg_ref[...].astype(BF16), preferred_element_type=F32)
    u = jnp.dot(a, wu_ref[...].astype(BF16), preferred_element_type=F32)
    o_ref[...] = (g * jax.nn.sigmoid(g) * u).astype(o_ref.dtype)


def _mm_swiglu(a, wg, wu, layer, tm=FFN_UP_TILES[0], tn=FFN_UP_TILES[1]):
    M, K = a.shape
    N = wg.shape[2]
    tm = _pick(M, tm, 8)
    tn = _pick(N, tn, 128)
    return pl.pallas_call(
        _mm_swiglu_kernel,
        grid=(M // tm, N // tn),
        in_specs=[pl.BlockSpec((tm, K), lambda i, j: (i, 0), pipeline_mode=pl.Buffered(1)),
                  pl.BlockSpec((None, K, tn), lambda i, j: (layer, 0, j)),
                  pl.BlockSpec((None, K, tn), lambda i, j: (layer, 0, j))],
        out_specs=pl.BlockSpec((tm, tn), lambda i, j: (i, j)),
        out_shape=jax.ShapeDtypeStruct((M, N), BF16),
        compiler_params=_params("parallel", "arbitrary"),
        name="ffn_up",
    )(a, wg, wu)


def _ada_kernel(c_ref, w_ref, b_ref, o_ref):
    c = c_ref[...]
    w = w_ref[...]
    c_hi = c.astype(BF16)
    c_lo = (c - c_hi.astype(F32)).astype(BF16)
    w_hi = w.astype(BF16)
    w_lo = (w - w_hi.astype(F32)).astype(BF16)
    r = c.shape[0]
    top = jnp.dot(jnp.concatenate([c_hi, c_lo], axis=0), w_hi, preferred_element_type=F32)
    o_ref[...] = top[0:r] + top[r:2 * r] + jnp.dot(c_hi, w_lo, preferred_element_type=F32) + b_ref[...]


def _ada(cs, w, b, layer, tn=1024):
    R, D = cs.shape
    N = w.shape[2]
    tn = _pick(N, tn, 128)
    return pl.pallas_call(
        _ada_kernel,
        grid=(N // tn,),
        in_specs=[pl.BlockSpec((R, D), lambda j: (0, 0)),
                  pl.BlockSpec((None, D, tn), lambda j: (layer, 0, j)),
                  pl.BlockSpec((None, 1, tn), lambda j: (layer, 0, j))],
        out_specs=pl.BlockSpec((R, tn), lambda j: (0, j)),
        out_shape=jax.ShapeDtypeStruct((R, N), F32),
        compiler_params=_params("arbitrary"),
        name="adaln",
    )(cs, w, b)


def _cast_kernel(w_ref, o_ref, *, rows_valid, cols_valid):
    tr, tc = o_ref.shape
    r = pl.program_id(0) * tr + lax.broadcasted_iota(jnp.int32, (tr, tc), 0)
    c = pl.program_id(1) * tc + lax.broadcasted_iota(jnp.int32, (tr, tc), 1)
    o_ref[...] = jnp.where((r < rows_valid) & (c < cols_valid), w_ref[...], 0.0).astype(o_ref.dtype)


def _cast_w(w, layer, row0, rows, rows_out, cols_out, tc=2048):
    cols = w.shape[2]
    tr = math.gcd(math.gcd(row0, rows_out), 2048)
    tc = _pick(cols_out, tc, 128)
    assert (rows_out - tr) < rows and (cols_out - tc) < cols and row0 + rows <= w.shape[1]
    return pl.pallas_call(
        functools.partial(_cast_kernel, rows_valid=rows, cols_valid=cols),
        grid=(rows_out // tr, cols_out // tc),
        in_specs=[pl.BlockSpec((None, tr, tc), lambda i, j: (layer, row0 // tr + i, j))],
        out_specs=pl.BlockSpec((tr, tc), lambda i, j: (i, j)),
        out_shape=jax.ShapeDtypeStruct((rows_out, cols_out), BF16),
        compiler_params=_params("parallel", "parallel"),
        name="cast_w",
    )(w)


def _norm_mod_kernel(x_ref, sh_ref, sc_ref, o_ref):
    x = x_ref[...]
    y = x * lax.rsqrt(jnp.mean(x * x, axis=-1, keepdims=True) + EPS)
    o_ref[...] = (y * (1.0 + sc_ref[...]) + sh_ref[...]).astype(o_ref.dtype)


def _norm_mod(x, shift, scale, tm=512):
    M, D = x.shape
    G = shift.shape[0]
    tm = _pick(M // G, tm, 8)
    tiles_per_group = (M // G) // tm
    mod_spec = pl.BlockSpec((None, 1, D), lambda i: (i // tiles_per_group, 0, 0))
    return pl.pallas_call(
        _norm_mod_kernel,
        grid=(M // tm,),
        in_specs=[pl.BlockSpec((tm, D), lambda i: (i, 0)), mod_spec, mod_spec],
        out_specs=pl.BlockSpec((tm, D), lambda i: (i, 0)),
        out_shape=jax.ShapeDtypeStruct((M, D), BF16),
        compiler_params=_params("parallel"),
        name="norm_mod",
    )(x, shift, scale)


ATTN_PANEL = 1024
ATTN_FAST_RANGE = 64.0
LOG2E = math.log2(math.e)


def _attn_kernel(*refs, G, tq, tk, n_kv, diff, post_scale):
    if n_kv:
        (q_ref, kc_ref, vc_ref, kx_ref, vx_ref, lam_ref, fast_ref, gain_ref,
         o_ref, qt_ref, m_ref, l_ref, acc_ref, s_ref) = refs
    else:
        q_ref, kc_ref, vc_ref, lam_ref, fast_ref, gain_ref, o_ref, qt_ref, m_ref, l_ref, acc_ref = refs
    n_rows = 2 if diff else G
    width = n_rows * tq
    pw = min(ATTN_PANEL, width)
    starts = list(range(0, width, pw))
    tn = (((0,), (0,)), ((), ()))

    def scores(k, c0):
        return jnp.dot(k, qt_ref[:, c0:c0 + pw], preferred_element_type=F32)

    def update(s, v, c0, mode):
        cols = slice(c0, c0 + pw)
        m_cur = jnp.max(s, axis=0, keepdims=True)
        if mode == "first":
            m_new = m_cur
            p = jnp.exp2(s - m_new)
            l_ref[:, cols] = jnp.sum(p, axis=0, keepdims=True)
            acc_ref[:, cols] = lax.dot_general(v, p.astype(BF16), tn, preferred_element_type=F32)
        elif mode == "safe":
            m_prev = m_ref[:, cols]
            m_new = jnp.maximum(m_prev, m_cur)
            alpha = jnp.exp2(m_prev - m_new)
            p = jnp.exp2(s - m_new)
            l_ref[:, cols] = alpha * l_ref[:, cols] + jnp.sum(p, axis=0, keepdims=True)
            acc_ref[:, cols] = alpha * acc_ref[:, cols] + lax.dot_general(
                v, p.astype(BF16), tn, preferred_element_type=F32)
        else:
            m_prev = m_ref[:, cols]
            p = jnp.exp2(s - m_prev)
            m_new = jnp.maximum(m_prev, m_cur)
            alpha = jnp.exp2(m_prev - m_new)
            l_ref[:, cols] = alpha * (l_ref[:, cols] + jnp.sum(p, axis=0, keepdims=True))
            acc_ref[:, cols] = alpha * (acc_ref[:, cols] + lax.dot_general(
                v, p.astype(BF16), tn, preferred_element_type=F32))
        m_ref[:, cols] = m_new

    q = q_ref[...].astype(F32)
    if diff:
        lane = lax.broadcasted_iota(jnp.int32, q.shape, 1)
        qt_ref[:, 0:tq] = jnp.where(lane < HEAD_DIM // 2, q, 0.0).T.astype(BF16)
        qt_ref[:, tq:2 * tq] = jnp.where(lane >= HEAD_DIM // 2, q, 0.0).T.astype(BF16)
    else:
        for g in range(G):
            qt_ref[:, g * tq:(g + 1) * tq] = q[:, g * HEAD_DIM:(g + 1) * HEAD_DIM].T.astype(BF16)

    kc = kc_ref[...]
    vc = vc_ref[...].astype(BF16)
    s = scores(kc, starts[0])
    for n, c0 in enumerate(starts):
        if n + 1 < len(starts):
            s_next = scores(kc, starts[n + 1])
        elif n_kv:
            s_next = scores(kx_ref[0:tk, :], starts[0])
        else:
            s_next = None
        update(s, vc, c0, "first")
        s = s_next

    if n_kv:
        s_ref[...] = s

        def block_two_pass(jb, carry):
            r0 = pl.multiple_of(jb * tk, tk)
            r1 = pl.multiple_of(jnp.minimum(jb + 1, n_kv - 1) * tk, tk)
            k = kx_ref[pl.ds(r0, tk), :]
            v = vx_ref[pl.ds(r0, tk), :].astype(BF16)
            s = s_ref[...]
            for n, c0 in enumerate(starts):
                if n + 1 < len(starts):
                    s_next = scores(k, starts[n + 1])
                else:
                    s_next = scores(kx_ref[pl.ds(r1, tk), :], starts[0])
                update(s, v, c0, "safe")
                s = s_next
            s_ref[...] = s
            return carry

        def block_one_pass(jb, carry):
            r0 = pl.multiple_of(jb * tk, tk)
            k = kx_ref[pl.ds(r0, tk), :]
            v = vx_ref[pl.ds(r0, tk), :].astype(BF16)
            for c0 in starts:
                update(scores(k, c0), v, c0, "fast")
            return carry

        @pl.when(fast_ref[0] > 0)
        def _():
            lax.fori_loop(0, n_kv, block_one_pass, 0)

        @pl.when(fast_ref[0] <= 0)
        def _():
            lax.fori_loop(0, n_kv, block_two_pass, 0)

    o = acc_ref[...] * (1.0 / l_ref[...])
    if diff:
        d = o[:, 0:tq] - lam_ref[0] * o[:, tq:2 * tq]
        d = d * lax.rsqrt(jnp.mean(d * d, axis=0, keepdims=True) + EPS)
        o_ref[...] = (d * gain_ref[...] * post_scale).T.astype(o_ref.dtype)
    else:
        for g in range(G):
            o_ref[:, g * HEAD_DIM:(g + 1) * HEAD_DIM] = o[:, g * tq:(g + 1) * tq].T.astype(o_ref.dtype)


def _attention(q, kc, vc, kx, vx, *, G, v_slot, score_bound, diff=False, lam=None, gain=None, post_scale=1.0,
               tq=2048, tk=2048):
    B, T, _ = q.shape
    L = kc.shape[1]
    H = kc.shape[2] // HEAD_DIM
    gq = 1 if diff else G
    n_rows = 2 if diff else G
    tq = _pick(T, tq, 128)
    if lam is None:
        lam = jnp.zeros((1,), F32)
    if gain is None:
        gain = jnp.ones((HEAD_DIM,), F32)
    in_specs = [pl.BlockSpec((None, tq, gq * HEAD_DIM), lambda b, h, i: (b, i, h)),
                pl.BlockSpec((None, L, HEAD_DIM), lambda b, h, i: (b, 0, h)),
                pl.BlockSpec((None, L, HEAD_DIM), lambda b, h, i: (b, 0, v_slot + h))]
    args = [q, kc, vc]
    scratch = [pltpu.VMEM((HEAD_DIM, n_rows * tq), BF16),
               pltpu.VMEM((1, n_rows * tq), F32),
               pltpu.VMEM((1, n_rows * tq), F32),
               pltpu.VMEM((HEAD_DIM, n_rows * tq), F32)]
    if kx is not None:
        S = kx.shape[1]
        tk = _pick(S, tk, 128)
        n_kv = S // tk
        in_specs += [pl.BlockSpec((None, S, HEAD_DIM), lambda b, h, i: (b, 0, h)),
                     pl.BlockSpec((None, S, HEAD_DIM), lambda b, h, i: (b, 0, v_slot + h))]
        args += [kx, vx]
        scratch += [pltpu.VMEM((tk, min(ATTN_PANEL, n_rows * tq)), F32)]
    else:
        n_kv = 0
    fast = (2.0 * score_bound <= ATTN_FAST_RANGE).astype(jnp.int32).reshape(1)
    in_specs += [pl.BlockSpec(memory_space=pltpu.SMEM), pl.BlockSpec(memory_space=pltpu.SMEM),
                 pl.BlockSpec((HEAD_DIM, 1), lambda b, h, i: (0, 0))]
    args += [lam.reshape(1).astype(F32), fast, gain.reshape(HEAD_DIM, 1).astype(F32)]
    return pl.pallas_call(
        functools.partial(_attn_kernel, G=G, tq=tq, tk=tk, n_kv=n_kv, diff=diff, post_scale=post_scale),
        grid=(B, H, T // tq),
        in_specs=in_specs,
        out_specs=pl.BlockSpec((None, tq, gq * HEAD_DIM), lambda b, h, i: (b, i, h)),
        out_shape=jax.ShapeDtypeStruct((B, T, H * gq * HEAD_DIM), BF16),
        scratch_shapes=scratch,
        compiler_params=_params("parallel", "parallel", "arbitrary"),
        name="attn_diff" if diff else "attn_gqa",
    )(*args)


_HG_LEVELS = (32, 16, 8, 4, 2, 1)


def _hgrn_consts(reverse):
    C = HG_CHUNK
    mst = np.zeros((2 + len(_HG_LEVELS), C, C), np.float32)
    msk = np.zeros((len(_HG_LEVELS) + 1, C, C), np.float32)
    for t in range(C):
        if not reverse:
            mst[0, t, :t + 1] = 1.0
            mst[1, t, t + 1:] = 1.0
        else:
            mst[0, t, t:] = 1.0
            mst[1, t, :t] = 1.0
        msk[len(_HG_LEVELS), t, t] = 1.0
    for l, m in enumerate(_HG_LEVELS):
        for t in range(C):
            start = (t // (2 * m)) * 2 * m
            mid = start + m
            second = t >= mid
            if not reverse:
                if second:
                    mst[2 + l, t, mid:t + 1] = 1.0
                    msk[l, t, start:mid] = 1.0
                else:
                    mst[2 + l, t, t + 1:mid] = 1.0
            else:
                if not second:
                    mst[2 + l, t, t:mid] = 1.0
                    msk[l, t, mid:start + 2 * m] = 1.0
                else:
                    mst[2 + l, t, mid:t] = 1.0
    mst = mst.reshape(-1, C)
    return np.concatenate([mst, mst], axis=1), msk


def _hgrn_kernel(*refs, n_chunks, unroll, reverse, nh, final):
    q_ref, v_ref, z_ref, llb_ref, l1p_ref, s0_ref, mst_ref, msk_ref = refs[:8]
    if final:
        of_ref, gate_ref, hog_ref, o_ref, sf_ref, st_ref = refs[8:]
    else:
        o_ref, sf_ref, st_ref = refs[8:]
    C = HG_CHUNK
    tb = pl.program_id(2)
    ntb = pl.num_programs(2)

    @pl.when(tb == 0)
    def _():
        st_ref[...] = s0_ref[...]

    mst = mst_ref[...]
    row = lax.broadcasted_iota(jnp.int32, (C, HEAD_DIM), 0)
    is_q = [((row // m) % 2) == (0 if reverse else 1) for m in _HG_LEVELS]
    pair = [msk_ref[l] > 0.5 for l in range(len(_HG_LEVELS) + 1)]
    nt = (((1,), (1,)), ((), ()))
    tn = (((0,), (0,)), ((), ()))

    def chunks(ci, carry):
        lanes = []
        for u in range(unroll):
            c = ci * unroll + u
            c = (n_chunks - 1 - c) if reverse else c
            rows = pl.ds(pl.multiple_of(c * C, C), C)
            lanes += [(rows, hh, slice(hh * HEAD_DIM, (hh + 1) * HEAD_DIM)) for hh in range(nh)]

        qs, vs, ks, gparts = [], [], [], []
        for rows, hh, cols in lanes:
            qr = q_ref[rows, cols]
            qs.append(qr * jax.nn.sigmoid(qr))
            vs.append(v_ref[rows, cols].astype(BF16))
            z = z_ref[rows, cols]
            t = l1p_ref[:, cols] + (jnp.minimum(z, 0.0) - jnp.log1p(jnp.exp(-jnp.abs(z))))
            llb = llb_ref[:, cols]
            g = jnp.maximum(llb, t) + jnp.log1p(jnp.exp(-jnp.abs(llb - t)))
            ks.append(1.0 - jnp.exp(g))
            g2 = g * LOG2E
            g_hi = g2.astype(BF16)
            gparts.append(jnp.concatenate([g_hi, (g2 - g_hi.astype(F32)).astype(BF16)], axis=0))

        sums = jnp.dot(mst, jnp.concatenate(gparts, axis=1), preferred_element_type=F32)

        q_ins, o_intra, incs, e_tots = [], [], [], []
        for n, (q, v, k) in enumerate(zip(qs, vs, ks)):
            e = jnp.exp2(sums[:, n * HEAD_DIM:(n + 1) * HEAD_DIM])
            a = jnp.where(pair[len(_HG_LEVELS)],
                          lax.dot_general(q.astype(BF16), k.astype(BF16), nt, preferred_element_type=F32), 0.0)
            for l in range(len(_HG_LEVELS)):
                x = (jnp.where(is_q[l], q, k) * e[(2 + l) * C:(3 + l) * C]).astype(BF16)
                a = jnp.where(pair[l], lax.dot_general(x, x, nt, preferred_element_type=F32), a)
            e_in = e[0:C]
            q_ins.append((q * e_in).astype(BF16))
            e_tots.append(e_in[0:1] if reverse else e_in[C - 1:C])
            o_intra.append(jnp.dot(a.astype(BF16), v, preferred_element_type=F32))
            incs.append(lax.dot_general(v, (k * e[C:2 * C]).astype(BF16), tn, preferred_element_type=F32))

        state = [st_ref[hh] for hh in range(nh)]
        st_in = []
        for (rows, hh, cols), e_tot, inc in zip(lanes, e_tots, incs):
            st_in.append(state[hh].astype(BF16))
            state[hh] = state[hh] * e_tot + inc
        for hh in range(nh):
            st_ref[hh] = state[hh]

        for (rows, hh, cols), q_in, st, o in zip(lanes, q_ins, st_in, o_intra):
            o = o + lax.dot_general(q_in, st, nt, preferred_element_type=F32)
            if final:
                o = o + of_ref[rows, cols]
                o = o * lax.rsqrt(jnp.mean(o * o, axis=-1, keepdims=True) + EPS) * hog_ref[...]
                gt = gate_ref[rows, cols]
                o = o * (gt * jax.nn.sigmoid(gt))
            o_ref[rows, cols] = o.astype(o_ref.dtype)
        return carry

    lax.fori_loop(0, n_chunks // unroll, chunks, 0)

    @pl.when(tb == ntb - 1)
    def _():
        sf_ref[...] = st_ref[...]


def _hgrn_scan(p, slots, llb, l1p, s0, *, reverse, o_fwd=None, o_gain=None, tb=1024):
    B, T, _ = p.shape
    W = llb.shape[1]
    H = W // HEAD_DIM
    final = o_fwd is not None
    nh = 2 if (H % 2 == 0 and all(s % 2 == 0 for s in slots)) else 1
    wb = nh * HEAD_DIM
    tb = _pick(T, tb, HG_CHUNK)
    n_chunks = tb // HG_CHUNK
    ntb = T // tb
    mst, msk = _hgrn_consts(reverse)

    def seq_spec(slot):
        s = slot // nh
        if reverse:
            return pl.BlockSpec((None, tb, wb), lambda b, h, t: (b, ntb - 1 - t, s + h))
        return pl.BlockSpec((None, tb, wb), lambda b, h, t: (b, t, s + h))

    vec_spec = pl.BlockSpec((1, wb), lambda b, h, t: (0, h))
    st_spec = pl.BlockSpec((None, nh, HEAD_DIM, HEAD_DIM), lambda b, h, t: (b, h, 0, 0))
    in_specs = [seq_spec(slots[0]), seq_spec(slots[1]), seq_spec(slots[2]), vec_spec, vec_spec, st_spec,
                pl.BlockSpec(mst.shape, lambda b, h, t: (0, 0)),
                pl.BlockSpec(msk.shape, lambda b, h, t: (0, 0, 0))]
    args = [p, p, p, llb, l1p, s0, jnp.asarray(mst, BF16), jnp.asarray(msk, F32)]
    if final:
        in_specs += [seq_spec(0), seq_spec(slots[3]), pl.BlockSpec((1, HEAD_DIM), lambda b, h, t: (0, 0))]
        args += [o_fwd, p, o_gain.reshape(1, HEAD_DIM).astype(F32)]
    return pl.pallas_call(
        functools.partial(_hgrn_kernel, n_chunks=n_chunks, unroll=math.gcd(n_chunks, HG_UNROLL), reverse=reverse,
                          nh=nh, final=final),
        grid=(B, H // nh, ntb),
        in_specs=in_specs,
        out_specs=[seq_spec(0), st_spec],
        out_shape=[jax.ShapeDtypeStruct((B, T, W), BF16 if final else F32),
                   jax.ShapeDtypeStruct((B, H, HEAD_DIM, HEAD_DIM), F32)],
        scratch_shapes=[pltpu.VMEM((nh, HEAD_DIM, HEAD_DIM), F32)],
        compiler_params=_params("parallel", "parallel", "arbitrary"),
        name="hgrn_bwd" if reverse else "hgrn_fwd",
    )(*args)


def _prep_kernel(x_ref, gain_ref, cos_ref, sin_ref, o_ref, *, ns, halves, rope, scale):
    gain = gain_ref[...]
    lane = lax.broadcasted_iota(jnp.int32, (x_ref.shape[0], HEAD_DIM), 1)
    low = lane < HEAD_DIM // 2
    even = (lane % 2) == 0
    for s in range(ns):
        cols = slice(s * HEAD_DIM, (s + 1) * HEAD_DIM)
        x = x_ref[:, cols]
        xx = x * x
        if halves:
            m0 = jnp.sum(jnp.where(low, xx, 0.0), axis=-1, keepdims=True) * (2.0 / HEAD_DIM)
            m1 = jnp.sum(jnp.where(low, 0.0, xx), axis=-1, keepdims=True) * (2.0 / HEAD_DIM)
            inv = jnp.where(low, lax.rsqrt(m0 + EPS), lax.rsqrt(m1 + EPS))
        else:
            inv = lax.rsqrt(jnp.mean(xx, axis=-1, keepdims=True) + EPS)
        y = x * inv * gain
        if rope:
            partner = jnp.where(even, pltpu.roll(y, HEAD_DIM - 1, 1), pltpu.roll(y, 1, 1))
            y = y * cos_ref[...] + partner * sin_ref[...]
        o_ref[:, cols] = (y * scale).astype(o_ref.dtype)


def _prep(p, slot, n_slots, gain, tables, *, halves, scale, seq_len, tm=1024):
    M = p.shape[0]
    ns = math.gcd(slot, n_slots)
    tm = _pick(seq_len, tm, 8)
    tiles_per_seq = seq_len // tm
    rope = tables is not None
    if rope:
        cos, sin = tables
        tab_spec = pl.BlockSpec((tm, HEAD_DIM), lambda i, j: (i % tiles_per_seq, 0))
    else:
        cos = sin = jnp.zeros((8, HEAD_DIM), F32)
        tab_spec = pl.BlockSpec((8, HEAD_DIM), lambda i, j: (0, 0))
    return pl.pallas_call(
        functools.partial(_prep_kernel, ns=ns, halves=halves, rope=rope, scale=scale),
        grid=(M // tm, n_slots // ns),
        in_specs=[pl.BlockSpec((tm, ns * HEAD_DIM), lambda i, j: (i, slot // ns + j)),
                  pl.BlockSpec((1, HEAD_DIM), lambda i, j: (0, 0)), tab_spec, tab_spec],
        out_specs=pl.BlockSpec((tm, ns * HEAD_DIM), lambda i, j: (i, j)),
        out_shape=jax.ShapeDtypeStruct((M, n_slots * HEAD_DIM), BF16),
        compiler_params=_params("parallel", "parallel"),
        name="prep_qk",
    )(p, gain.reshape(1, HEAD_DIM).astype(F32), cos, sin)


def _rope_tables(seq_len, dim):
    rows = seq_len // GRID_W
    row = jnp.repeat(jnp.arange(rows, dtype=F32), GRID_W)
    col = jnp.tile(jnp.arange(GRID_W, dtype=F32), rows)
    half = dim // 2
    inv = ROPE_THETA ** (-jnp.arange(0, half, 2, dtype=F32) / half)
    ang = jnp.concatenate([row[:, None] * inv, col[:, None] * inv], axis=-1)
    cos = jnp.repeat(jnp.cos(ang), 2, axis=-1)
    sin = jnp.repeat(jnp.sin(ang), 2, axis=-1) * jnp.tile(jnp.asarray([-1.0, 1.0], F32), half)
    reps = HEAD_DIM // dim
    return jnp.tile(cos, (1, reps)), jnp.tile(sin, (1, reps))


def _dims(D):
    n = D // HEAD_DIM
    a_heads = 3 * n // 8
    a_kv = a_heads // 3
    b_heads = (n - a_heads) // 2
    c_heads = n - a_heads - b_heads
    return a_heads, a_kv, b_heads, c_heads


def _layer(x2, y2, B, layer, mod, rope_a, rope_c, lb, lam_init, last,
           w_in, w_out, aq_g, ak_g, ho_g, dq_g, dk_g, d_lam, do_g, w_gate, w_up, w_down):
    D = x2.shape[1]
    T = x2.shape[0] // B
    L = y2.shape[0] // B
    a_heads, a_kv, b_heads, c_heads = _dims(D)
    a_group = a_heads // a_kv
    half = HEAD_DIM // 2
    widths = (a_heads, c_heads, b_heads, b_heads, a_kv, a_kv, c_heads, c_heads, b_heads, b_heads, b_heads)
    (s_aq, s_cq, s_bq, s_bg, s_ak, s_av, s_ck, s_cv, s_bi, s_bff, s_bfb) = (
        np.concatenate([[0], np.cumsum(widths)[:-1]]).tolist())

    mx = [m[:, None, :] for m in jnp.split(mod[0:B], 6, axis=-1)]
    my = [m[:, None, :] for m in jnp.split(mod[B:B + 1], 6, axis=-1)]

    px = _mm(_norm_mod(x2, mx[0], mx[1]), w_in, layer, F32)
    py = _mm(_norm_mod(y2, my[0], my[1]), w_in, layer, F32)
    px3 = px.reshape(B, T, -1)
    py3 = py.reshape(B, L, -1)

    a_scale = HEAD_DIM ** -0.5 * LOG2E
    qa_x = _prep(px, s_aq, a_heads, aq_g, rope_a, halves=False, scale=a_scale, seq_len=T).reshape(B, T, -1)
    ka_x = _prep(px, s_ak, a_kv, ak_g, rope_a, halves=False, scale=1.0, seq_len=T).reshape(B, T, -1)
    ka_y = _prep(py, s_ak, a_kv, ak_g, None, halves=False, scale=1.0, seq_len=L).reshape(B, L, -1)
    a_bound = 1.02 * HEAD_DIM * a_scale * jnp.max(jnp.abs(aq_g)) * jnp.max(jnp.abs(ak_g))
    oa_x = _attention(qa_x, ka_y, py3, ka_x, px3, G=a_group, v_slot=s_av, score_bound=a_bound)

    c_scale = half ** -0.5 * LOG2E
    dl = d_lam.astype(F32)
    lam = jnp.exp(jnp.sum(dl[0] * dl[1])) - jnp.exp(jnp.sum(dl[2] * dl[3])) + lam_init
    dq_g2 = jnp.tile(dq_g, 2)
    dk_g2 = jnp.tile(dk_g, 2)
    qc_x = _prep(px, s_cq, c_heads, dq_g2, rope_c, halves=True, scale=c_scale, seq_len=T).reshape(B, T, -1)
    kc_x = _prep(px, s_ck, c_heads, dk_g2, rope_c, halves=True, scale=1.0, seq_len=T).reshape(B, T, -1)
    kc_y = _prep(py, s_ck, c_heads, dk_g2, None, halves=True, scale=1.0, seq_len=L).reshape(B, L, -1)
    c_bound = 1.02 * half * c_scale * jnp.max(jnp.abs(dq_g)) * jnp.max(jnp.abs(dk_g))
    oc_x = _attention(qc_x, kc_y, py3, kc_x, px3, G=2, v_slot=s_cv, score_bound=c_bound, diff=True, lam=lam,
                      gain=do_g, post_scale=1.0 - lam_init)

    llb = jnp.log(lb).reshape(2, 1, -1)
    l1p = jnp.log1p(-lb).reshape(2, 1, -1)
    s0 = jnp.zeros((B, b_heads, HEAD_DIM, HEAD_DIM), F32)
    f_slots = (s_bq, s_bi, s_bff, s_bg)
    b_slots = (s_bq, s_bi, s_bfb, s_bg)
    of_y, s_f = _hgrn_scan(py3, f_slots, llb[0], l1p[0], s0, reverse=False)
    ob_y, s_b = _hgrn_scan(py3, b_slots, llb[1], l1p[1], s0, reverse=True, o_fwd=of_y, o_gain=ho_g)
    of_x, _ = _hgrn_scan(px3, f_slots, llb[0], l1p[0], s_f, reverse=False)
    ob_x, _ = _hgrn_scan(px3, b_slots, llb[1], l1p[1], s_b, reverse=True, o_fwd=of_x, o_gain=ho_g)

    x2 = _mm_res([oa_x.reshape(B * T, -1), ob_x.reshape(B * T, -1), oc_x.reshape(B * T, -1)], w_out, x2, mx[2])
    x2 = _mm_res([_mm_swiglu(_norm_mod(x2, mx[3], mx[4]), w_gate, w_up, layer)], [w_down], x2, mx[5],
                 tm=FFN_DOWN_TILES[0], tn=FFN_DOWN_TILES[1], tk_max=w_down.shape[0])
    if last:
        return x2, None

    qa_y = _prep(py, s_aq, a_heads, aq_g, None, halves=False, scale=a_scale, seq_len=L).reshape(B, L, -1)
    oa_y = _attention(qa_y, ka_y, py3, None, None, G=a_group, v_slot=s_av, score_bound=a_bound)
    qc_y = _prep(py, s_cq, c_heads, dq_g2, None, halves=True, scale=c_scale, seq_len=L).reshape(B, L, -1)
    oc_y = _attention(qc_y, kc_y, py3, None, None, G=2, v_slot=s_cv, score_bound=c_bound, diff=True, lam=lam,
                      gain=do_g, post_scale=1.0 - lam_init)
    y2 = _mm_res([oa_y.reshape(B * L, -1), ob_y.reshape(B * L, -1), oc_y.reshape(B * L, -1)], w_out, y2, my[2])
    y2 = _mm_res([_mm_swiglu(_norm_mod(y2, my[3], my[4]), w_gate, w_up, layer)], [w_down], y2, my[5],
                 tm=FFN_DOWN_TILES[0], tn=FFN_DOWN_TILES[1], tk_max=w_down.shape[0])
    return x2, y2


def kernel(x, c, ctx, c_ctx, w_ada, b_ada, w_in, w_out, attn_q_gain, attn_k_gain,
           hgrn_lb_logits, hgrn_o_gain, diff_q_gain, diff_k_gain, diff_lambda, diff_o_gain,
           w_gate, w_up, w_down):
    B, T, D = x.shape
    L = ctx.shape[1]
    depth = w_ada.shape[0]
    a_heads, _, b_heads, _ = _dims(D)
    rope_a = _rope_tables(T, HEAD_DIM)
    rope_c = _rope_tables(T, HEAD_DIM // 2)

    sm = jax.nn.softmax(hgrn_lb_logits.astype(F32), axis=0)
    lb = jnp.concatenate([jnp.zeros_like(sm[:1]), jnp.cumsum(sm[1:], axis=0)], axis=0)

    cs = jnp.concatenate([c, c_ctx[None, :]], axis=0)
    cs = jax.nn.silu(jnp.pad(cs, ((0, 8 - (B + 1)), (0, 0))))

    d_ff = w_gate.shape[-1]
    a_w = a_heads * HEAD_DIM
    b_w = b_heads * HEAD_DIM

    x2 = x.reshape(B * T, D)
    y2 = ctx.reshape(B * L, D)
    for l in range(depth):
        lam_init = 0.8 - 0.6 * math.exp(-0.3 * l)
        mod = _ada(cs, w_ada, b_ada.reshape(depth, 1, -1), l)
        wd = _cast_w(w_down, l, 0, d_ff, d_ff, D)
        wo = [_cast_w(w_out, l, r0, n, n, D) for r0, n in ((0, a_w), (a_w, b_w), (a_w + b_w, D - a_w - b_w))]
        x2, y2 = _layer(x2, y2, B, l, mod, rope_a, rope_c, lb[l], lam_init, l == depth - 1,
                        w_in, wo,
                        attn_q_gain[l], attn_k_gain[l], hgrn_o_gain[l], diff_q_gain[l], diff_k_gain[l],
                        diff_lambda[l], diff_o_gain[l], w_gate, w_up, wd)
    return x2.reshape(B, T, D)
```
